```python
import math
import jax, jax.numpy as jnp
from jax import lax
import numpy as np

D_MODEL = 1024
BATCH = 8
SEQ = 4096
DEPTH = 1

MIX_WIDTH = D_MODEL
DIFF_WIDTH = MIX_WIDTH // 2
GMLP_WIDTH = MIX_WIDTH - DIFF_WIDTH
DIFF_HEADS = 4
DIFF_HEAD_DIM = DIFF_WIDTH // (2 * DIFF_HEADS)
QK_WIDTH = DIFF_HEADS * DIFF_HEAD_DIM
GMLP_GROUPS = 8
GMLP_GROUP_DIM = GMLP_WIDTH // GMLP_GROUPS
CHUNK = 128
Q_BLOCK = 128
IN_WIDTH = 4 * QK_WIDTH + DIFF_WIDTH + 2 * GMLP_WIDTH
N_EXPERTS = 32
TOP_K = 4
D_FF = D_MODEL
SWIGLU_LIMIT = 7.0
SWIGLU_ALPHA = 1.702
EXPERT_BLOCK = 128
LN_EPS = 1e-5
DEEPNORM_ALPHA = (2.0 * DEPTH) ** 0.25
DEEPNORM_BETA = (8.0 * DEPTH) ** -0.25

kernel_name = "hybrid_diffattn_gmlp_moe_deepnorm"


def layer_norm(x, g, b):
    xf = x.astype(jnp.float32)
    mu = jnp.mean(xf, axis=-1, keepdims=True)
    var = jnp.mean(jnp.square(xf - mu), axis=-1, keepdims=True)
    return ((xf - mu) * lax.rsqrt(var + LN_EPS)).astype(x.dtype) * g + b


def rms_norm(x, g):
    xf = x.astype(jnp.float32)
    ms = jnp.mean(jnp.square(xf), axis=-1, keepdims=True)
    return (xf * lax.rsqrt(ms + LN_EPS)).astype(x.dtype) * g


def diff_attention(q1, q2, k1, k2, v, lam):
    B, S, H, Dh = q1.shape
    nb = S // Q_BLOCK
    scale = Dh ** -0.5
    kpos = jnp.arange(S)
    lam32 = lam.astype(jnp.float32)

    def to_blocks(q):
        return q.reshape(B, nb, Q_BLOCK, H, Dh).transpose(1, 0, 2, 3, 4)

    def one_block(args):
        q1b, q2b, bi = args
        qpos = bi * Q_BLOCK + jnp.arange(Q_BLOCK)
        causal = kpos[None, :] <= qpos[:, None]

        def probs(qb, k):
            s = jnp.einsum('bqhd,bkhd->bhqk', qb, k).astype(jnp.float32) * scale
            return jax.nn.softmax(jnp.where(causal, s, -jnp.inf), axis=-1)

        a = probs(q1b, k1) - lam32 * probs(q2b, k2)
        return jnp.einsum('bhqk,bkhe->bqhe', a.astype(v.dtype), v)

    out = lax.map(one_block, (to_blocks(q1), to_blocks(q2), jnp.arange(nb)))
    return out.transpose(1, 0, 2, 3, 4).reshape(B, S, H, v.shape[-1])


def spatial_gating(gu, gv, ln_g, ln_b, w_s, b_s):
    B, S, _ = gu.shape
    u = jax.nn.gelu(gu, approximate=False)
    v = layer_norm(jax.nn.gelu(gv, approximate=False), ln_g, ln_b)
    vc = v.reshape(B, S // CHUNK, CHUNK, GMLP_GROUPS, GMLP_GROUP_DIM)
    causal = jnp.tril(jnp.ones((CHUNK, CHUNK), dtype=bool))
    w = jnp.where(causal[None], w_s, jnp.zeros_like(w_s))
    z = jnp.einsum('gts,bcsgd->bctgd', w, vc) + b_s.T[None, None, :, :, None]
    return u * z.reshape(B, S, GMLP_WIDTH)


def token_mixer(x, w_in, lq1, lk1, lq2, lk2, subln_g, gln_g, gln_b, w_s, b_s, w_o, lambda_init):
    B, S, _ = x.shape
    proj = x @ w_in
    cuts = [QK_WIDTH, 2 * QK_WIDTH, 3 * QK_WIDTH, 4 * QK_WIDTH,
            4 * QK_WIDTH + DIFF_WIDTH, 4 * QK_WIDTH + DIFF_WIDTH + GMLP_WIDTH]
    q1, q2, k1, k2, v, gu, gv = jnp.split(proj, cuts, axis=-1)
    heads = lambda t: t.reshape(B, S, DIFF_HEADS, -1)
    lam = jnp.exp(jnp.sum(lq1 * lk1)) - jnp.exp(jnp.sum(lq2 * lk2)) + lambda_init
    attn = diff_attention(heads(q1), heads(q2), heads(k1), heads(k2), heads(v), lam)
    attn = (rms_norm(attn, subln_g) * (1.0 - lambda_init)).reshape(B, S, DIFF_WIDTH)
    gated = spatial_gating(gu, gv, gln_g, gln_b, w_s, b_s)
    return jnp.concatenate([attn, gated], axis=-1) @ w_o


def moe_ffn(x, w_router, b_router, w_up, b_up, w_down, b_down):
    B, S, D = x.shape
    N = B * S
    A = N * TOP_K
    P = A + N_EXPERTS * EXPERT_BLOCK
    nblk = P // EXPERT_BLOCK
    xt = x.reshape(N, D)
    logits = xt @ w_router + b_router
    topv, topi = lax.top_k(logits, TOP_K)
    gates = jax.nn.softmax(topv.astype(jnp.float32), axis=-1).astype(x.dtype)

    flat_e = topi.reshape(A)
    flat_tok = jnp.repeat(jnp.arange(N, dtype=jnp.int32), TOP_K)
    flat_g = gates.reshape(A)
    order = jnp.argsort(flat_e, stable=True)
    sorted_e = flat_e[order]
    sorted_tok = flat_tok[order]
    sorted_g = flat_g[order]

    counts = jnp.bincount(flat_e, length=N_EXPERTS)
    padded = ((counts + EXPERT_BLOCK - 1) // EXPERT_BLOCK) * EXPERT_BLOCK
    start_sorted = jnp.cumsum(counts) - counts
    end_padded = jnp.cumsum(padded)
    start_padded = end_padded - padded
    rank = jnp.arange(A) - start_sorted[sorted_e]
    dest = start_padded[sorted_e] + rank

    tok_pad = jnp.zeros((P,), jnp.int32).at[dest].set(sorted_tok)
    x_blocks = xt[tok_pad].reshape(nblk, EXPERT_BLOCK, D)
    blk_start = jnp.arange(nblk) * EXPERT_BLOCK
    blk_expert = jnp.clip(jnp.searchsorted(end_padded, blk_start, side='right'), 0, N_EXPERTS - 1)

    def expert_block(args):
        xb, e = args
        h = xb @ w_up[e] + b_up[e]
        gate, lin = h[:, :D_FF], h[:, D_FF:]
        gate = jnp.minimum(gate, SWIGLU_LIMIT)
        lin = jnp.clip(lin, -SWIGLU_LIMIT, SWIGLU_LIMIT)
        act = (lin + 1.0) * gate * jax.nn.sigmoid(SWIGLU_ALPHA * gate)
        return act @ w_down[e] + b_down[e]

    y_pad = lax.map(expert_block, (x_blocks, blk_expert)).reshape(P, D)
    y_assign = y_pad[dest] * sorted_g[:, None]
    out = jax.ops.segment_sum(y_assign, sorted_tok, num_segments=N)
    return out.reshape(B, S, D)


def setup_inputs(seed: int = 0) -> dict:
    key = jax.random.key(seed)
    ks = jax.random.split(key, 24)
    L = DEPTH
    nrm = lambda k, shape, s: jax.random.normal(k, shape, jnp.float32) * s
    tril = jnp.tril(jnp.ones((CHUNK, CHUNK), jnp.float32))
    return {
        "x": nrm(ks[0], (BATCH, SEQ, D_MODEL), 1.0),
        "w_in": nrm(ks[1], (L, D_MODEL, IN_WIDTH), D_MODEL ** -0.5),
        "lambda_q1": nrm(ks[2], (L, DIFF_HEAD_DIM), 0.1),
        "lambda_k1": nrm(ks[3], (L, DIFF_HEAD_DIM), 0.1),
        "lambda_q2": nrm(ks[4], (L, DIFF_HEAD_DIM), 0.1),
        "lambda_k2": nrm(ks[5], (L, DIFF_HEAD_DIM), 0.1),
        "subln_g": 1.0 + nrm(ks[6], (L, 2 * DIFF_HEAD_DIM), 0.01),
        "gmlp_ln_g": 1.0 + nrm(ks[7], (L, GMLP_WIDTH), 0.01),
        "gmlp_ln_b": nrm(ks[8], (L, GMLP_WIDTH), 0.01),
        "w_spatial": nrm(ks[9], (L, GMLP_GROUPS, CHUNK, CHUNK), CHUNK ** -0.5) * tril,
        "b_spatial": 1.0 + nrm(ks[10], (L, GMLP_GROUPS, CHUNK), 0.01),
        "w_o": nrm(ks[11], (L, MIX_WIDTH, D_MODEL), MIX_WIDTH ** -0.5 * DEEPNORM_BETA),
        "ln1_g": 1.0 + nrm(ks[12], (L, D_MODEL), 0.01),
        "ln1_b": nrm(ks[13], (L, D_MODEL), 0.01),
        "w_router": nrm(ks[14], (L, D_MODEL, N_EXPERTS), D_MODEL ** -0.5),
        "b_router": nrm(ks[15], (L, N_EXPERTS), 0.01),
        "w_up": nrm(ks[16], (L, N_EXPERTS, D_MODEL, 2 * D_FF), D_MODEL ** -0.5),
        "b_up": nrm(ks[17], (L, N_EXPERTS, 2 * D_FF), 0.01),
        "w_down": nrm(ks[18], (L, N_EXPERTS, D_FF, D_MODEL), D_FF ** -0.5 * DEEPNORM_BETA),
        "b_down": nrm(ks[19], (L, N_EXPERTS, D_MODEL), 0.01),
        "ln2_g": 1.0 + nrm(ks[20], (L, D_MODEL), 0.01),
        "ln2_b": nrm(ks[21], (L, D_MODEL), 0.01),
    }


def reference(x, w_in, lambda_q1, lambda_k1, lambda_q2, lambda_k2, subln_g, gmlp_ln_g, gmlp_ln_b,
              w_spatial, b_spatial, w_o, ln1_g, ln1_b, w_router, b_router, w_up, b_up,
              w_down, b_down, ln2_g, ln2_b):
    for l in range(DEPTH):
        lambda_init = 0.8 - 0.6 * math.exp(-0.3 * l)
        mixed = token_mixer(x, w_in[l], lambda_q1[l], lambda_k1[l], lambda_q2[l], lambda_k2[l],
                            subln_g[l], gmlp_ln_g[l], gmlp_ln_b[l], w_spatial[l], b_spatial[l],
                            w_o[l], lambda_init)
        x = layer_norm(DEEPNORM_ALPHA * x + mixed, ln1_g[l], ln1_b[l])
        ffn = moe_ffn(x, w_router[l], b_router[l], w_up[l], b_up[l], w_down[l], b_down[l])
        x = layer_norm(DEEPNORM_ALPHA * x + ffn, ln2_g[l], ln2_b[l])
    return x
```

```python
import functools

import jax
import jax.numpy as jnp
from jax import lax
from jax.experimental import pallas as pl
from jax.experimental.pallas import tpu as pltpu

DIFF_HEADS = 4
DIFF_HEAD_DIM = 64
V_HEAD_DIM = 2 * DIFF_HEAD_DIM
QK_WIDTH = DIFF_HEADS * DIFF_HEAD_DIM
DIFF_WIDTH = DIFF_HEADS * V_HEAD_DIM
GMLP_GROUPS = 8
GMLP_GROUP_DIM = 64
GMLP_WIDTH = GMLP_GROUPS * GMLP_GROUP_DIM
CHUNK = 128
N_EXPERTS = 32
TOP_K = 4
SWIGLU_LIMIT = 7.0
SWIGLU_ALPHA = 1.702
LN_EPS = 1e-5
DEPTH = 1
DEEPNORM_ALPHA = (2.0 * DEPTH) ** 0.25
LAMBDA_INIT = 0.8 - 0.6 * 1.0

LANES = 128

PROJ_TM = 512
ATTN_T = 256
MIX_TM = 256
DISPATCH_TM = 512
FFN_TM = 256
COMBINE_TM = 256

VMEM_LIMIT = 48 * 1024 * 1024

BF16 = jnp.bfloat16
F32 = jnp.float32


def _layer_norm(y, g, b):
    mu = jnp.mean(y, axis=-1, keepdims=True)
    yc = y - mu
    var = jnp.mean(yc * yc, axis=-1, keepdims=True)
    return yc * lax.rsqrt(var + LN_EPS) * g + b


def _gelu(x):
    return 0.5 * x * (1.0 + lax.erf(x * (2.0 ** -0.5)))


def _nt_dot(a, b):
    return lax.dot_general(a, b, (((1,), (1,)), ((), ())), preferred_element_type=F32)


def _proj_kernel(x_ref, wqk_ref, wv_ref, wg_ref, lng_ref, lnb_ref,
                 qq_ref, kk_ref, v_ref, u_ref, vn_ref):
    xb = x_ref[...].astype(BF16)
    qk = jnp.dot(xb, wqk_ref[...], preferred_element_type=F32)
    qq_ref[...] = qk[:, :DIFF_WIDTH].astype(BF16)
    kk_ref[...] = qk[:, DIFF_WIDTH:].astype(BF16)
    v_ref[...] = jnp.dot(xb, wv_ref[...], preferred_element_type=F32).astype(BF16)
    g = jnp.dot(xb, wg_ref[...], preferred_element_type=F32)
    u_ref[...] = _gelu(g[:, :GMLP_WIDTH]).astype(BF16)
    gv = _gelu(g[:, GMLP_WIDTH:])
    vn_ref[...] = _layer_norm(gv, lng_ref[...], lnb_ref[...]).astype(BF16)


def _proj(x2, wqk, wv, wg, lng, lnb):
    n, d = x2.shape
    tm = PROJ_TM
    row = lambda i: (i, 0)
    const = lambda i: (0, 0)
    out = jax.ShapeDtypeStruct((n, DIFF_WIDTH), BF16)
    return pl.pallas_call(
        _proj_kernel,
        out_shape=(out, out, out, out, out),
        grid=(n // tm,),
        in_specs=[
            pl.BlockSpec((tm, d), row),
            pl.BlockSpec(wqk.shape, const),
            pl.BlockSpec(wv.shape, const),
            pl.BlockSpec(wg.shape, const),
            pl.BlockSpec(lng.shape, const),
            pl.BlockSpec(lnb.shape, const),
        ],
        out_specs=[pl.BlockSpec((tm, DIFF_WIDTH), row)] * 5,
        compiler_params=pltpu.CompilerParams(
            dimension_semantics=("arbitrary",), vmem_limit_bytes=VMEM_LIMIT),
        name="proj",
    )(x2, wqk, wv, wg, lng, lnb)


def _attn_kernel(lamv_ref, g_ref, qq_ref, kk_ref, v_ref, o_ref):
    t = ATTN_T
    qi = pl.program_id(2)
    q = qq_ref[0]
    lane = lax.broadcasted_iota(jnp.int32, q.shape, 1)
    first = lane < DIFF_HEAD_DIM
    zero = jnp.zeros_like(q)
    q1 = jnp.where(first, q, zero)
    q2 = jnp.where(first, zero, q)

    def step(j, carry, masked):
        start = pl.multiple_of(j * t, t)
        k = kk_ref[0, pl.ds(start, t), :]
        vt = v_ref[0, pl.ds(start, t), :]
        new = []
        for qh, (m, l, acc) in zip((q1, q2), carry):
            s = _nt_dot(qh, k)
            if masked:
                rowi = lax.broadcasted_iota(jnp.int32, s.shape, 0)
                coli = lax.broadcasted_iota(jnp.int32, s.shape, 1)
                s = jnp.where(coli <= rowi, s, -jnp.inf)
            m_new = jnp.maximum(m, jnp.max(s, axis=-1, keepdims=True))
            alpha = jnp.exp(m - m_new)
            p = jnp.exp(s - m_new)
            l_new = alpha * l + jnp.sum(p, axis=-1, keepdims=True)
            acc_new = alpha * acc + jnp.dot(p.astype(BF16), vt, preferred_element_type=F32)
            new.append((m_new, l_new, acc_new))
        return tuple(new)

    def init():
        return (jnp.full((t, 1), -jnp.inf, F32), jnp.zeros((t, 1), F32),
                jnp.zeros((t, V_HEAD_DIM), F32))

    carry = lax.fori_loop(0, qi, lambda j, c: step(j, c, False), (init(), init()))
    (_, l1, acc1), (_, l2, acc2) = step(qi, carry, True)

    lv = lamv_ref[...]
    lam = (jnp.exp(jnp.sum(lv[0:1] * lv[1:2], axis=-1, keepdims=True))
           - jnp.exp(jnp.sum(lv[2:3] * lv[3:4], axis=-1, keepdims=True)) + LAMBDA_INIT)
    o = acc1 / l1 - lam * (acc2 / l2)
    ms = jnp.mean(o * o, axis=-1, keepdims=True)
    o = o * lax.rsqrt(ms + LN_EPS) * g_ref[...] * (1.0 - LAMBDA_INIT)
    o_ref[0] = o.astype(BF16)


def _attention(qq, kk, v, lamv, subln_g):
    b, s, _ = qq.shape
    t = ATTN_T
    qspec = pl.BlockSpec((1, t, V_HEAD_DIM), lambda bi, h, qi: (bi, qi, h))
    kvspec = pl.BlockSpec((1, s, V_HEAD_DIM), lambda bi, h, qi: (bi, 0, h))
    const = lambda bi, h, qi: (0, 0)
    return pl.pallas_call(
        _attn_kernel,
        out_shape=jax.ShapeDtypeStruct((b, s, DIFF_WIDTH), BF16),
        grid=(b, DIFF_HEADS, s // t),
        in_specs=[pl.BlockSpec(lamv.shape, const), pl.BlockSpec(subln_g.shape, const),
                  qspec, kvspec, kvspec],
        out_specs=qspec,
        compiler_params=pltpu.CompilerParams(
            dimension_semantics=("arbitrary",) * 3, vmem_limit_bytes=VMEM_LIMIT),
        name="attn",
    )(lamv, subln_g, qq, kk, v)


def _mix_kernel(attn_ref, u_ref, vn_ref, x_ref, wsp_ref, bsp_ref, wo_ref, g1_ref, b1_ref,
                wrt_ref, br_ref,
                x1_ref, eid_ref, gate_ref, rank_ref, cnt_ref,
                cat_ref, carry_ref):
    tm = MIX_TM

    @pl.when(pl.program_id(0) == 0)
    def _():
        carry_ref[...] = jnp.zeros_like(carry_ref)

    ri = lax.broadcasted_iota(jnp.int32, (CHUNK, CHUNK), 0)
    ci = lax.broadcasted_iota(jnp.int32, (CHUNK, CHUNK), 1)
    tril = ci <= ri
    first = ci < GMLP_GROUP_DIM
    w = [jnp.where(tril, wsp_ref[g], 0.0).astype(BF16) for g in range(GMLP_GROUPS)]
    cat_ref[:, :DIFF_WIDTH] = attn_ref[...]
    for c in range(tm // CHUNK):
        rows = slice(c * CHUNK, (c + 1) * CHUNK)
        for jb in range(GMLP_WIDTH // LANES):
            cols = slice(jb * LANES, (jb + 1) * LANES)
            vb = vn_ref[rows, cols]
            zero = jnp.zeros_like(vb)
            z = (jnp.dot(w[2 * jb], jnp.where(first, vb, zero), preferred_element_type=F32)
                 + jnp.dot(w[2 * jb + 1], jnp.where(first, zero, vb), preferred_element_type=F32))
            gated = u_ref[rows, cols].astype(F32) * (z + bsp_ref[:, cols])
            cat_ref[rows, DIFF_WIDTH + jb * LANES:DIFF_WIDTH + (jb + 1) * LANES] = gated.astype(BF16)

    mixed = jnp.dot(cat_ref[...], wo_ref[...], preferred_element_type=F32)
    x1 = _layer_norm(DEEPNORM_ALPHA * x_ref[...] + mixed, g1_ref[...], b1_ref[...])
    x1_ref[...] = x1

    xh = x1.astype(BF16)
    xl = (x1 - xh.astype(F32)).astype(BF16)
    wr = wrt_ref[...]
    wh = wr.astype(BF16)
    wl = (wr - wh.astype(F32)).astype(BF16)
    logits = _nt_dot(wh, xh) + _nt_dot(wl, xh) + _nt_dot(wh, xl) + br_ref[...]

    eio = lax.broadcasted_iota(jnp.int32, logits.shape, 0).astype(F32)
    vals, idxs, sels = [], [], []
    cur = logits
    for _ in range(TOP_K):
        mx = jnp.max(cur, axis=0, keepdims=True)
        idx = jnp.min(jnp.where(cur == mx, eio, float(N_EXPERTS)), axis=0, keepdims=True)
        sel = eio == idx
        vals.append(mx)
        idxs.append(idx)
        sels.append(sel)
        cur = jnp.where(sel, -jnp.inf, cur)
    ex = [jnp.exp(vk - vals[0]) for vk in vals]
    denom = ex[0] + ex[1] + ex[2] + ex[3]
    gate_ref[...] = jnp.concatenate([e / denom for e in ex], axis=0)
    eid_ref[...] = jnp.concatenate(idxs, axis=0).astype(jnp.int32)

    chosen = (sels[0] | sels[1] | sels[2] | sels[3])
    onehot = jnp.where(chosen, 1.0, 0.0)
    ti = lax.broadcasted_iota(jnp.int32, (tm, tm), 0)
    tj = lax.broadcasted_iota(jnp.int32, (tm, tm), 1)
    before = jnp.where(ti < tj, 1.0, 0.0).astype(BF16)
    carry = carry_ref[:, 0:1]
    cnt_before = jnp.dot(onehot.astype(BF16), before, preferred_element_type=F32) + carry
    ranks = [jnp.sum(jnp.where(s, cnt_before, 0.0), axis=0, keepdims=True) for s in sels]
    rank_ref[...] = jnp.concatenate(ranks, axis=0).astype(jnp.int32)
    total = carry + jnp.sum(onehot, axis=1, keepdims=True)
    carry_ref[...] = jnp.broadcast_to(total, carry_ref.shape)
    cnt_ref[...] = jnp.broadcast_to(total, cnt_ref.shape)


def _mix(attn, u, vn, x2, wsp, bsp, wo, g1, b1, wrt, br):
    n, d = x2.shape
    tm = MIX_TM
    row = lambda i: (i, 0)
    col = lambda i: (0, i)
    const2 = lambda i: (0, 0)
    const3 = lambda i: (0, 0, 0)
    tok = lambda dt: jax.ShapeDtypeStruct((TOP_K, n), dt)
    return pl.pallas_call(
        _mix_kernel,
        out_shape=(jax.ShapeDtypeStruct((n, d), F32), tok(jnp.int32), tok(F32), tok(jnp.int32),
                   jax.ShapeDtypeStruct((N_EXPERTS, LANES), F32)),
        grid=(n // tm,),
        in_specs=[
            pl.BlockSpec((tm, DIFF_WIDTH), row),
            pl.BlockSpec((tm, GMLP_WIDTH), row),
            pl.BlockSpec((tm, GMLP_WIDTH), row),
            pl.BlockSpec((tm, d), row),
            pl.BlockSpec(wsp.shape, const3),
            pl.BlockSpec(bsp.shape, const2),
            pl.BlockSpec(wo.shape, const2),
            pl.BlockSpec(g1.shape, const2),
            pl.BlockSpec(b1.shape, const2),
            pl.BlockSpec(wrt.shape, const2),
            pl.BlockSpec(br.shape, const2),
        ],
        out_specs=[
            pl.BlockSpec((tm, d), row),
            pl.BlockSpec((TOP_K, tm), col),
            pl.BlockSpec((TOP_K, tm), col),
            pl.BlockSpec((TOP_K, tm), col),
            pl.BlockSpec((N_EXPERTS, LANES), const2),
        ],
        scratch_shapes=[pltpu.VMEM((tm, DIFF_WIDTH + GMLP_WIDTH), BF16),
                        pltpu.VMEM((N_EXPERTS, LANES), F32)],
        compiler_params=pltpu.CompilerParams(
            dimension_semantics=("arbitrary",), vmem_limit_bytes=VMEM_LIMIT),
        name="mix",
    )(attn, u, vn, x2, wsp, bsp, wo, g1, b1, wrt, br)


def _dispatch_kernel(pad_start_ref, pad_len_ref, nused_ref, dest_ref, x1_hbm, xs_hbm, zblk_ref,
                     sem, zsem):
    tm = DISPATCH_TM
    nblk = xs_hbm.shape[0] // FFN_TM
    i = pl.program_id(0)
    base = i * tm

    def row_copy(t, k):
        return pltpu.make_async_copy(x1_hbm.at[pl.ds(base + t, 1)],
                                     xs_hbm.at[pl.ds(dest_ref[t * TOP_K + k], 1)], sem)

    def issue(t, _):
        for k in range(TOP_K):
            row_copy(t, k).start()
        return 0

    lax.fori_loop(0, tm, issue, 0)

    @pl.when(i == 0)
    def _():
        zblk_ref[...] = jnp.zeros_like(zblk_ref)

        def zero_copy(r):
            return pltpu.make_async_copy(zblk_ref.at[pl.ds(0, 1)], xs_hbm.at[pl.ds(r, 1)], zsem)

        def tail_copy(blk):
            return pltpu.make_async_copy(
                zblk_ref, xs_hbm.at[pl.ds(pl.multiple_of(blk * FFN_TM, FFN_TM), FFN_TM)], zsem)

        def tail_start(blk, c):
            tail_copy(blk).start()
            return c

        def tail_wait(blk, c):
            tail_copy(blk).wait()
            return c

        lax.fori_loop(nused_ref[0], nblk, tail_start, 0)
        lax.fori_loop(nused_ref[0], nblk, tail_wait, 0)

        def per_expert(e, _):
            s = pad_start_ref[e]
            cnt = pad_len_ref[e]

            def start(r, c):
                zero_copy(s + r).start()
                return c

            def wait(r, c):
                zero_copy(s + r).wait()
                return c

            lax.fori_loop(0, cnt, start, 0)
            lax.fori_loop(0, cnt, wait, 0)
            return 0

        lax.fori_loop(0, N_EXPERTS, per_expert, 0)

    pltpu.make_async_copy(x1_hbm.at[pl.ds(0, tm * TOP_K)], xs_hbm.at[pl.ds(0, tm * TOP_K)],
                          sem).wait()


def _dispatch(x1, dest_flat, pad_start, pad_len, nused, p_rows):
    n, d = x1.shape
    tm = DISPATCH_TM
    return pl.pallas_call(
        _dispatch_kernel,
        out_shape=jax.ShapeDtypeStruct((p_rows, d), F32),
        grid_spec=pltpu.PrefetchScalarGridSpec(
            num_scalar_prefetch=3,
            grid=(n // tm,),
            in_specs=[
                pl.BlockSpec((tm * TOP_K,), lambda i, *_: (i,), memory_space=pltpu.SMEM),
                pl.BlockSpec(memory_space=pl.ANY),
            ],
            out_specs=pl.BlockSpec(memory_space=pl.ANY),
            scratch_shapes=[pltpu.VMEM((FFN_TM, d), F32), pltpu.SemaphoreType.DMA(()),
                            pltpu.SemaphoreType.DMA(())],
        ),
        compiler_params=pltpu.CompilerParams(
            dimension_semantics=("arbitrary",), vmem_limit_bytes=VMEM_LIMIT),
        name="dispatch",
    )(pad_start, pad_len, nused, dest_flat, x1)


def _ffn_kernel(blk_e_ref, nused_ref, xs_ref, wup_ref, bup_ref, wdn_ref, bdn_ref, y_ref):
    dff = wdn_ref.shape[1]
    used = pl.program_id(0) < nused_ref[0]

    @pl.when(jnp.logical_not(used))
    def _():
        y_ref[...] = jnp.zeros_like(y_ref)

    @pl.when(used)
    def _():
        xb = xs_ref[...].astype(BF16)
        h = jnp.dot(xb, wup_ref[0], preferred_element_type=F32) + bup_ref[0]
        gate = jnp.minimum(h[:, :dff], SWIGLU_LIMIT)
        lin = jnp.clip(h[:, dff:], -SWIGLU_LIMIT, SWIGLU_LIMIT)
        act = (lin + 1.0) * gate * jax.nn.sigmoid(SWIGLU_ALPHA * gate)
        y_ref[...] = jnp.dot(act.astype(BF16), wdn_ref[0], preferred_element_type=F32) + bdn_ref[0]


def _ffn(xs, blk_e, nused, wup, bup, wdn, bdn):
    p_rows, d = xs.shape
    tm = FFN_TM
    e, _, dff2 = wup.shape
    dff = dff2 // 2
    rowblk = lambda i, be, nu: (i, 0)
    wsel = lambda i, be, nu: (be[i], 0, 0)
    return pl.pallas_call(
        _ffn_kernel,
        out_shape=jax.ShapeDtypeStruct((p_rows, d), F32),
        grid_spec=pltpu.PrefetchScalarGridSpec(
            num_scalar_prefetch=2,
            grid=(p_rows // tm,),
            in_specs=[
                pl.BlockSpec((tm, d), rowblk),
                pl.BlockSpec((1, d, dff2), wsel),
                pl.BlockSpec((1, 1, dff2), wsel),
                pl.BlockSpec((1, dff, d), wsel),
                pl.BlockSpec((1, 1, d), wsel),
            ],
            out_specs=pl.BlockSpec((tm, d), rowblk),
        ),
        compiler_params=pltpu.CompilerParams(
            dimension_semantics=("arbitrary",), vmem_limit_bytes=VMEM_LIMIT),
        name="ffn",
    )(blk_e, nused, xs, wup, bup, wdn, bdn)


def _combine_kernel(dcur_ref, dnxt_ref, gate_ref, x1_ref, g2_ref, b2_ref, y_hbm, o_ref, ybuf, sem):
    tm = COMBINE_TM
    i = pl.program_id(0)
    nsteps = pl.num_programs(0)
    slot = lax.rem(i, 2)

    def row_copy(dref, t, k, sl):
        return pltpu.make_async_copy(y_hbm.at[pl.ds(dref[t * TOP_K + k], 1)],
                                     ybuf.at[sl, k, pl.ds(t, 1)], sem.at[sl])

    def issue(dref, sl):
        def body(t, _):
            for k in range(TOP_K):
                row_copy(dref, t, k, sl).start()
            return 0
        lax.fori_loop(0, tm, body, 0)

    @pl.when(i == 0)
    def _():
        issue(dcur_ref, 0)

    @pl.when(i + 1 < nsteps)
    def _():
        issue(dnxt_ref, 1 - slot)

    for k in range(TOP_K):
        pltpu.make_async_copy(y_hbm.at[pl.ds(0, tm)], ybuf.at[slot, k], sem.at[slot]).wait()

    gates = gate_ref[...]
    ffn = gates[:, 0:1] * ybuf[slot, 0]
    for k in range(1, TOP_K):
        ffn = ffn + gates[:, k:k + 1] * ybuf[slot, k]
    o_ref[...] = _layer_norm(DEEPNORM_ALPHA * x1_ref[...] + ffn, g2_ref[...], b2_ref[...])


def _combine(y, dest_flat, gates_tok, x1, g2, b2):
    n, d = x1.shape
    tm = COMBINE_TM
    nsteps = n // tm
    row = lambda i: (i, 0)
    const = lambda i: (0, 0)
    return pl.pallas_call(
        _combine_kernel,
        out_shape=jax.ShapeDtypeStruct((n, d), F32),
        grid=(nsteps,),
        in_specs=[
            pl.BlockSpec((tm * TOP_K,), lambda i: (i,), memory_space=pltpu.SMEM),
            pl.BlockSpec((tm * TOP_K,), lambda i: (jnp.minimum(i + 1, nsteps - 1),),
                         memory_space=pltpu.SMEM),
            pl.BlockSpec((tm, TOP_K), row),
            pl.BlockSpec((tm, d), row),
            pl.BlockSpec(g2.shape, const),
            pl.BlockSpec(b2.shape, const),
            pl.BlockSpec(memory_space=pl.ANY),
        ],
        out_specs=pl.BlockSpec((tm, d), row),
        scratch_shapes=[pltpu.VMEM((2, TOP_K, tm, d), F32), pltpu.SemaphoreType.DMA((2,))],
        compiler_params=pltpu.CompilerParams(
            dimension_semantics=("arbitrary",), vmem_limit_bytes=VMEM_LIMIT),
        name="combine",
    )(dest_flat, dest_flat, gates_tok, x1, g2, b2, y)


def _head_pairs(w_a, w_b):
    d = w_a.shape[0]
    a = w_a.reshape(d, DIFF_HEADS, DIFF_HEAD_DIM)
    b = w_b.reshape(d, DIFF_HEADS, DIFF_HEAD_DIM)
    return jnp.concatenate([a, b], axis=-1).reshape(d, DIFF_HEADS * V_HEAD_DIM)


def kernel(x, w_in, lambda_q1, lambda_k1, lambda_q2, lambda_k2, subln_g, gmlp_ln_g, gmlp_ln_b,
           w_spatial, b_spatial, w_o, ln1_g, ln1_b, w_router, b_router, w_up, b_up,
           w_down, b_down, ln2_g, ln2_b):
    b, s, d = x.shape
    n = b * s
    x2 = x.reshape(n, d)

    w = w_in[0]
    c = QK_WIDTH
    scale = DIFF_HEAD_DIM ** -0.5
    wq = _head_pairs(w[:, 0:c], w[:, c:2 * c]) * scale
    wk = _head_pairs(w[:, 2 * c:3 * c], w[:, 3 * c:4 * c])
    wqk = jnp.concatenate([wq, wk], axis=1).astype(BF16)
    wv = w[:, 4 * c:4 * c + DIFF_WIDTH].astype(BF16)
    wg = w[:, 4 * c + DIFF_WIDTH:].astype(BF16)
    lamv = jnp.concatenate([lambda_q1, lambda_k1, lambda_q2, lambda_k2], axis=0)
    bsp = jnp.repeat(b_spatial[0].T, GMLP_GROUP_DIM, axis=1)

    qq, kk, v, u, vn = _proj(x2, wqk, wv, wg, gmlp_ln_g, gmlp_ln_b)
    attn = _attention(qq.reshape(b, s, -1), kk.reshape(b, s, -1), v.reshape(b, s, -1),
                      lamv, subln_g).reshape(n, DIFF_WIDTH)
    x1, eid, gates, rank, cnt = _mix(attn, u, vn, x2, w_spatial[0], bsp, w_o[0].astype(BF16),
                                     ln1_g, ln1_b, w_router[0].T, b_router[0][:, None])

    counts = cnt[:, 0].astype(jnp.int32)
    padded = ((counts + FFN_TM - 1) // FFN_TM) * FFN_TM
    end_padded = jnp.cumsum(padded)
    start_padded = end_padded - padded
    dest_flat = (start_padded[eid] + rank).T.reshape(n * TOP_K)
    p_rows = n * TOP_K + N_EXPERTS * FFN_TM
    nblk = p_rows // FFN_TM
    nused = (end_padded[-1:] // FFN_TM).astype(jnp.int32)
    blk_start = jnp.arange(nblk, dtype=jnp.int32) * FFN_TM
    blk_e = jnp.minimum(jnp.sum((end_padded[None, :] <= blk_start[:, None]).astype(jnp.int32), axis=1),
                        N_EXPERTS - 1)

    xs = _dispatch(x1, dest_flat, start_padded + counts, padded - counts, nused, p_rows)
    y = _ffn(xs, blk_e, nused, w_up[0].astype(BF16), b_up[0][:, None, :],
             w_down[0].astype(BF16), b_down[0][:, None, :])
    out = _combine(y, dest_flat, gates.T, x1, ln2_g, ln2_b)
    return out.reshape(b, s, d)
```

```python
import functools

import jax
import jax.numpy as jnp
from jax import lax
from jax.experimental import pallas as pl
from jax.experimental.pallas import tpu as pltpu

DIFF_HEADS = 4
DIFF_HEAD_DIM = 64
V_HEAD_DIM = 2 * DIFF_HEAD_DIM
QK_WIDTH = DIFF_HEADS * DIFF_HEAD_DIM
DIFF_WIDTH = DIFF_HEADS * V_HEAD_DIM
GMLP_GROUPS = 8
GMLP_GROUP_DIM = 64
GMLP_WIDTH = GMLP_GROUPS * GMLP_GROUP_DIM
CHUNK = 128
N_EXPERTS = 32
TOP_K = 4
SWIGLU_LIMIT = 7.0
SWIGLU_ALPHA = 1.702
LN_EPS = 1e-5
DEPTH = 1
DEEPNORM_ALPHA = (2.0 * DEPTH) ** 0.25
LAMBDA_INIT = 0.8 - 0.6 * 1.0

LANES = 128

PROJ_TM = 512
ATTN_T = 256
MIX_TM = 256
DISPATCH_TM = 512
FFN_TM = 256
COMBINE_TM = 256

VMEM_LIMIT = 48 * 1024 * 1024

BF16 = jnp.bfloat16
F32 = jnp.float32


def _layer_norm(y, g, b):
    mu = jnp.mean(y, axis=-1, keepdims=True)
    yc = y - mu
    var = jnp.mean(yc * yc, axis=-1, keepdims=True)
    return yc * lax.rsqrt(var + LN_EPS) * g + b


def _gelu(x):
    return 0.5 * x * (1.0 + lax.erf(x * (2.0 ** -0.5)))


def _nt_dot(a, b):
    return lax.dot_general(a, b, (((1,), (1,)), ((), ())), preferred_element_type=F32)


def _proj_kernel(x_ref, wqk_ref, wv_ref, wg_ref, lng_ref, lnb_ref,
                 qq_ref, kk_ref, v_ref, u_ref, vn_ref):
    xb = x_ref[...].astype(BF16)
    qk = jnp.dot(xb, wqk_ref[...], preferred_element_type=F32)
    qq_ref[...] = qk[:, :DIFF_WIDTH].astype(BF16)
    kk_ref[...] = qk[:, DIFF_WIDTH:].astype(BF16)
    v_ref[...] = jnp.dot(xb, wv_ref[...], preferred_element_type=F32).astype(BF16)
    g = jnp.dot(xb, wg_ref[...], preferred_element_type=F32)
    u_ref[...] = _gelu(g[:, :GMLP_WIDTH]).astype(BF16)
    gv = _gelu(g[:, GMLP_WIDTH:])
    vn_ref[...] = _layer_norm(gv, lng_ref[...], lnb_ref[...]).astype(BF16)


def _proj(x2, wqk, wv, wg, lng, lnb):
    n, d = x2.shape
    tm = PROJ_TM
    row = lambda i: (i, 0)
    const = lambda i: (0, 0)
    out = jax.ShapeDtypeStruct((n, DIFF_WIDTH), BF16)
    return pl.pallas_call(
        _proj_kernel,
        out_shape=(out, out, out, out, out),
        grid=(n // tm,),
        in_specs=[
            pl.BlockSpec((tm, d), row),
            pl.BlockSpec(wqk.shape, const),
            pl.BlockSpec(wv.shape, const),
            pl.BlockSpec(wg.shape, const),
            pl.BlockSpec(lng.shape, const),
            pl.BlockSpec(lnb.shape, const),
        ],
        out_specs=[pl.BlockSpec((tm, DIFF_WIDTH), row)] * 5,
        compiler_params=pltpu.CompilerParams(
            dimension_semantics=("arbitrary",), vmem_limit_bytes=VMEM_LIMIT),
        name="proj",
    )(x2, wqk, wv, wg, lng, lnb)


def _attn_kernel(lamv_ref, g_ref, qq_ref, kk_ref, v_ref, o_ref):
    t = ATTN_T
    qi = pl.program_id(2)
    q = qq_ref[0]
    lane = lax.broadcasted_iota(jnp.int32, q.shape, 1)
    first = lane < DIFF_HEAD_DIM
    zero = jnp.zeros_like(q)
    q1 = jnp.where(first, q, zero)
    q2 = jnp.where(first, zero, q)

    def step(j, carry, masked):
        start = pl.multiple_of(j * t, t)
        k = kk_ref[0, pl.ds(start, t), :]
        vt = v_ref[0, pl.ds(start, t), :]
        new = []
        for qh, (m, l, acc) in zip((q1, q2), carry):
            s = _nt_dot(qh, k)
            if masked:
                rowi = lax.broadcasted_iota(jnp.int32, s.shape, 0)
                coli = lax.broadcasted_iota(jnp.int32, s.shape, 1)
                s = jnp.where(coli <= rowi, s, -jnp.inf)
            m_new = jnp.maximum(m, jnp.max(s, axis=-1, keepdims=True))
            alpha = jnp.exp(m - m_new)
            p = jnp.exp(s - m_new)
            l_new = alpha * l + jnp.sum(p, axis=-1, keepdims=True)
            acc_new = alpha * acc + jnp.dot(p.astype(BF16), vt, preferred_element_type=F32)
            new.append((m_new, l_new, acc_new))
        return tuple(new)

    def init():
        return (jnp.full((t, 1), -jnp.inf, F32), jnp.zeros((t, 1), F32),
                jnp.zeros((t, V_HEAD_DIM), F32))

    carry = lax.fori_loop(0, qi, lambda j, c: step(j, c, False), (init(), init()))
    (_, l1, acc1), (_, l2, acc2) = step(qi, carry, True)

    lv = lamv_ref[...]
    lam = (jnp.exp(jnp.sum(lv[0:1] * lv[1:2], axis=-1, keepdims=True))
           - jnp.exp(jnp.sum(lv[2:3] * lv[3:4], axis=-1, keepdims=True)) + LAMBDA_INIT)
    o = acc1 / l1 - lam * (acc2 / l2)
    ms = jnp.mean(o * o, axis=-1, keepdims=True)
    o = o * lax.rsqrt(ms + LN_EPS) * g_ref[...] * (1.0 - LAMBDA_INIT)
    o_ref[0] = o.astype(BF16)


def _attention(qq, kk, v, lamv, subln_g):
    b, s, _ = qq.shape
    t = ATTN_T
    qspec = pl.BlockSpec((1, t, V_HEAD_DIM), lambda bi, h, qi: (bi, qi, h))
    kvspec = pl.BlockSpec((1, s, V_HEAD_DIM), lambda bi, h, qi: (bi, 0, h))
    const = lambda bi, h, qi: (0, 0)
    return pl.pallas_call(
        _attn_kernel,
        out_shape=jax.ShapeDtypeStruct((b, s, DIFF_WIDTH), BF16),
        grid=(b, DIFF_HEADS, s // t),
        in_specs=[pl.BlockSpec(lamv.shape, const), pl.BlockSpec(subln_g.shape, const),
                  qspec, kvspec, kvspec],
        out_specs=qspec,
        compiler_params=pltpu.CompilerParams(
            dimension_semantics=("arbitrary",) * 3, vmem_limit_bytes=VMEM_LIMIT),
        name="attn",
    )(lamv, subln_g, qq, kk, v)


def _mix_kernel(attn_ref, u_ref, vn_ref, x_ref, wsp_ref, bsp_ref, wo_ref, g1_ref, b1_ref,
                wrt_ref, br_ref,
                x1_ref, eid_ref, gate_ref, rank_ref, cnt_ref,
                cat_ref, carry_ref):
    tm = MIX_TM

    @pl.when(pl.program_id(0) == 0)
    def _():
        carry_ref[...] = jnp.zeros_like(carry_ref)

    ri = lax.broadcasted_iota(jnp.int32, (CHUNK, CHUNK), 0)
    ci = lax.broadcasted_iota(jnp.int32, (CHUNK, CHUNK), 1)
    tril = ci <= ri
    first = ci < GMLP_GROUP_DIM
    w = [jnp.where(tril, wsp_ref[g], 0.0).astype(BF16) for g in range(GMLP_GROUPS)]
    cat_ref[:, :DIFF_WIDTH] = attn_ref[...]
    for c in range(tm // CHUNK):
        rows = slice(c * CHUNK, (c + 1) * CHUNK)
        for jb in range(GMLP_WIDTH // LANES):
            cols = slice(jb * LANES, (jb + 1) * LANES)
            vb = vn_ref[rows, cols]
            zero = jnp.zeros_like(vb)
            z = (jnp.dot(w[2 * jb], jnp.where(first, vb, zero), preferred_element_type=F32)
                 + jnp.dot(w[2 * jb + 1], jnp.where(first, zero, vb), preferred_element_type=F32))
            gated = u_ref[rows, cols].astype(F32) * (z + bsp_ref[:, cols])
            cat_ref[rows, DIFF_WIDTH + jb * LANES:DIFF_WIDTH + (jb + 1) * LANES] = gated.astype(BF16)

    mixed = jnp.dot(cat_ref[...], wo_ref[...], preferred_element_type=F32)
    x1 = _layer_norm(DEEPNORM_ALPHA * x_ref[...] + mixed, g1_ref[...], b1_ref[...])
    x1_ref[...] = x1

    xh = x1.astype(BF16)
    xl = (x1 - xh.astype(F32)).astype(BF16)
    wr = wrt_ref[...]
    wh = wr.astype(BF16)
    wl = (wr - wh.astype(F32)).astype(BF16)
    logits = _nt_dot(wh, xh) + _nt_dot(wl, xh) + _nt_dot(wh, xl) + br_ref[...]

    eio = lax.broadcasted_iota(jnp.int32, logits.shape, 0).astype(F32)
    vals, idxs, sels = [], [], []
    cur = logits
    for _ in range(TOP_K):
        mx = jnp.max(cur, axis=0, keepdims=True)
        idx = jnp.min(jnp.where(cur == mx, eio, float(N_EXPERTS)), axis=0, keepdims=True)
        sel = eio == idx
        vals.append(mx)
        idxs.append(idx)
        sels.append(sel)
        cur = jnp.where(sel, -jnp.inf, cur)
    ex = [jnp.exp(vk - vals[0]) for vk in vals]
    denom = ex[0] + ex[1] + ex[2] + ex[3]
    gate_ref[...] = jnp.concatenate([e / denom for e in ex], axis=0)
    eid_ref[...] = jnp.concatenate(idxs, axis=0).astype(jnp.int32)

    chosen = (sels[0] | sels[1] | sels[2] | sels[3])
    onehot = jnp.where(chosen, 1.0, 0.0)
    ti = lax.broadcasted_iota(jnp.int32, (tm, tm), 0)
    tj = lax.broadcasted_iota(jnp.int32, (tm, tm), 1)
    before = jnp.where(ti < tj, 1.0, 0.0).astype(BF16)
    carry = carry_ref[:, 0:1]
    cnt_before = jnp.dot(onehot.astype(BF16), before, preferred_element_type=F32) + carry
    ranks = [jnp.sum(jnp.where(s, cnt_before, 0.0), axis=0, keepdims=True) for s in sels]
    rank_ref[...] = jnp.concatenate(ranks, axis=0).astype(jnp.int32)
    total = carry + jnp.sum(onehot, axis=1, keepdims=True)
    carry_ref[...] = jnp.broadcast_to(total, carry_ref.shape)
    cnt_ref[...] = jnp.broadcast_to(total, cnt_ref.shape)


def _mix(attn, u, vn, x2, wsp, bsp, wo, g1, b1, wrt, br):
    n, d = x2.shape
    tm = MIX_TM
    row = lambda i: (i, 0)
    col = lambda i: (0, i)
    const2 = lambda i: (0, 0)
    const3 = lambda i: (0, 0, 0)
    tok = lambda dt: jax.ShapeDtypeStruct((TOP_K, n), dt)
    return pl.pallas_call(
        _mix_kernel,
        out_shape=(jax.ShapeDtypeStruct((n, d), F32), tok(jnp.int32), tok(F32), tok(jnp.int32),
                   jax.ShapeDtypeStruct((N_EXPERTS, LANES), F32)),
        grid=(n // tm,),
        in_specs=[
            pl.BlockSpec((tm, DIFF_WIDTH), row),
            pl.BlockSpec((tm, GMLP_WIDTH), row),
            pl.BlockSpec((tm, GMLP_WIDTH), row),
            pl.BlockSpec((tm, d), row),
            pl.BlockSpec(wsp.shape, const3),
            pl.BlockSpec(bsp.shape, const2),
            pl.BlockSpec(wo.shape, const2),
            pl.BlockSpec(g1.shape, const2),
            pl.BlockSpec(b1.shape, const2),
            pl.BlockSpec(wrt.shape, const2),
            pl.BlockSpec(br.shape, const2),
        ],
        out_specs=[
            pl.BlockSpec((tm, d), row),
            pl.BlockSpec((TOP_K, tm), col),
            pl.BlockSpec((TOP_K, tm), col),
            pl.BlockSpec((TOP_K, tm), col),
            pl.BlockSpec((N_EXPERTS, LANES), const2),
        ],
        scratch_shapes=[pltpu.VMEM((tm, DIFF_WIDTH + GMLP_WIDTH), BF16),
                        pltpu.VMEM((N_EXPERTS, LANES), F32)],
        compiler_params=pltpu.CompilerParams(
            dimension_semantics=("arbitrary",), vmem_limit_bytes=VMEM_LIMIT),
        name="mix",
    )(attn, u, vn, x2, wsp, bsp, wo, g1, b1, wrt, br)


def _dispatch_kernel(pad_start_ref, pad_len_ref, nused_ref, dest_ref, x1_ref, xs_hbm, zblk_ref,
                     sem, zsem):
    tm = DISPATCH_TM
    nblk = xs_hbm.shape[0] // FFN_TM
    i = pl.program_id(0)

    def row_copy(t, k):
        return pltpu.make_async_copy(x1_ref.at[pl.ds(t, 1)],
                                     xs_hbm.at[pl.ds(dest_ref[t * TOP_K + k], 1)], sem)

    def issue(t, _):
        for k in range(TOP_K):
            row_copy(t, k).start()
        return 0

    lax.fori_loop(0, tm, issue, 0)

    @pl.when(i == 0)
    def _():
        zblk_ref[...] = jnp.zeros_like(zblk_ref)

        def zero_copy(r):
            return pltpu.make_async_copy(zblk_ref.at[pl.ds(0, 1)], xs_hbm.at[pl.ds(r, 1)], zsem)

        def tail_copy(blk):
            return pltpu.make_async_copy(
                zblk_ref, xs_hbm.at[pl.ds(pl.multiple_of(blk * FFN_TM, FFN_TM), FFN_TM)], zsem)

        def tail_start(blk, c):
            tail_copy(blk).start()
            return c

        def tail_wait(blk, c):
            tail_copy(blk).wait()
            return c

        lax.fori_loop(nused_ref[0], nblk, tail_start, 0)
        lax.fori_loop(nused_ref[0], nblk, tail_wait, 0)

        def per_expert(e, _):
            s = pad_start_ref[e]
            cnt = pad_len_ref[e]

            def start(r, c):
                zero_copy(s + r).start()
                return c

            def wait(r, c):
                zero_copy(s + r).wait()
                return c

            lax.fori_loop(0, cnt, start, 0)
            lax.fori_loop(0, cnt, wait, 0)
            return 0

        lax.fori_loop(0, N_EXPERTS, per_expert, 0)

    for _ in range(TOP_K):
        pltpu.make_async_copy(x1_ref, xs_hbm.at[pl.ds(0, tm)], sem).wait()


def _dispatch(x1, dest_flat, pad_start, pad_len, nused, p_rows):
    n, d = x1.shape
    tm = DISPATCH_TM
    return pl.pallas_call(
        _dispatch_kernel,
        out_shape=jax.ShapeDtypeStruct((p_rows, d), F32),
        grid_spec=pltpu.PrefetchScalarGridSpec(
            num_scalar_prefetch=3,
            grid=(n // tm,),
            in_specs=[
                pl.BlockSpec((tm * TOP_K,), lambda i, *_: (i,), memory_space=pltpu.SMEM),
                pl.BlockSpec((tm, d), lambda i, *_: (i, 0)),
            ],
            out_specs=pl.BlockSpec(memory_space=pl.ANY),
            scratch_shapes=[pltpu.VMEM((FFN_TM, d), F32), pltpu.SemaphoreType.DMA(()),
                            pltpu.SemaphoreType.DMA(())],
        ),
        compiler_params=pltpu.CompilerParams(
            dimension_semantics=("arbitrary",), vmem_limit_bytes=VMEM_LIMIT),
        name="dispatch",
    )(pad_start, pad_len, nused, dest_flat, x1)


def _ffn_kernel(blk_e_ref, nused_ref, xs_ref, wup_ref, bup_ref, wdn_ref, bdn_ref, y_ref):
    dff = wdn_ref.shape[1]
    used = pl.program_id(0) < nused_ref[0]

    @pl.when(jnp.logical_not(used))
    def _():
        y_ref[...] = jnp.zeros_like(y_ref)

    @pl.when(used)
    def _():
        xb = xs_ref[...].astype(BF16)
        h = jnp.dot(xb, wup_ref[0], preferred_element_type=F32) + bup_ref[0]
        gate = jnp.minimum(h[:, :dff], SWIGLU_LIMIT)
        lin = jnp.clip(h[:, dff:], -SWIGLU_LIMIT, SWIGLU_LIMIT)
        act = (lin + 1.0) * gate * jax.nn.sigmoid(SWIGLU_ALPHA * gate)
        y_ref[...] = jnp.dot(act.astype(BF16), wdn_ref[0], preferred_element_type=F32) + bdn_ref[0]


def _ffn(xs, blk_e, nused, wup, bup, wdn, bdn):
    p_rows, d = xs.shape
    tm = FFN_TM
    e, _, dff2 = wup.shape
    dff = dff2 // 2
    rowblk = lambda i, be, nu: (i, 0)
    wsel = lambda i, be, nu: (be[i], 0, 0)
    return pl.pallas_call(
        _ffn_kernel,
        out_shape=jax.ShapeDtypeStruct((p_rows, d), F32),
        grid_spec=pltpu.PrefetchScalarGridSpec(
            num_scalar_prefetch=2,
            grid=(p_rows // tm,),
            in_specs=[
                pl.BlockSpec((tm, d), rowblk),
                pl.BlockSpec((1, d, dff2), wsel),
                pl.BlockSpec((1, 1, dff2), wsel),
                pl.BlockSpec((1, dff, d), wsel),
                pl.BlockSpec((1, 1, d), wsel),
            ],
            out_specs=pl.BlockSpec((tm, d), rowblk),
        ),
        compiler_params=pltpu.CompilerParams(
            dimension_semantics=("arbitrary",), vmem_limit_bytes=VMEM_LIMIT),
        name="ffn",
    )(blk_e, nused, xs, wup, bup, wdn, bdn)


def _combine_kernel(dcur_ref, dnxt_ref, gate_ref, x1_ref, g2_ref, b2_ref, y_hbm, o_ref, ybuf, sem):
    tm = COMBINE_TM
    i = pl.program_id(0)
    nsteps = pl.num_programs(0)
    slot = lax.rem(i, 2)

    def row_copy(dref, t, k, sl):
        return pltpu.make_async_copy(y_hbm.at[pl.ds(dref[t * TOP_K + k], 1)],
                                     ybuf.at[sl, k, pl.ds(t, 1)], sem.at[sl])

    def issue(dref, sl):
        def body(t, _):
            for k in range(TOP_K):
                row_copy(dref, t, k, sl).start()
            return 0
        lax.fori_loop(0, tm, body, 0)

    @pl.when(i == 0)
    def _():
        issue(dcur_ref, 0)

    @pl.when(i + 1 < nsteps)
    def _():
        issue(dnxt_ref, 1 - slot)

    for k in range(TOP_K):
        pltpu.make_async_copy(y_hbm.at[pl.ds(0, tm)], ybuf.at[slot, k], sem.at[slot]).wait()

    gates = gate_ref[...]
    ffn = gates[:, 0:1] * ybuf[slot, 0]
    for k in range(1, TOP_K):
        ffn = ffn + gates[:, k:k + 1] * ybuf[slot, k]
    o_ref[...] = _layer_norm(DEEPNORM_ALPHA * x1_ref[...] + ffn, g2_ref[...], b2_ref[...])


def _combine(y, dest_flat, gates_tok, x1, g2, b2):
    n, d = x1.shape
    tm = COMBINE_TM
    nsteps = n // tm
    row = lambda i: (i, 0)
    const = lambda i: (0, 0)
    return pl.pallas_call(
        _combine_kernel,
        out_shape=jax.ShapeDtypeStruct((n, d), F32),
        grid=(nsteps,),
        in_specs=[
            pl.BlockSpec((tm * TOP_K,), lambda i: (i,), memory_space=pltpu.SMEM),
            pl.BlockSpec((tm * TOP_K,), lambda i: (jnp.minimum(i + 1, nsteps - 1),),
                         memory_space=pltpu.SMEM),
            pl.BlockSpec((tm, TOP_K), row),
            pl.BlockSpec((tm, d), row),
            pl.BlockSpec(g2.shape, const),
            pl.BlockSpec(b2.shape, const),
            pl.BlockSpec(memory_space=pl.ANY),
        ],
        out_specs=pl.BlockSpec((tm, d), row),
        scratch_shapes=[pltpu.VMEM((2, TOP_K, tm, d), F32), pltpu.SemaphoreType.DMA((2,))],
        compiler_params=pltpu.CompilerParams(
            dimension_semantics=("arbitrary",), vmem_limit_bytes=VMEM_LIMIT),
        name="combine",
    )(dest_flat, dest_flat, gates_tok, x1, g2, b2, y)


def _head_pairs(w_a, w_b):
    d = w_a.shape[0]
    a = w_a.reshape(d, DIFF_HEADS, DIFF_HEAD_DIM)
    b = w_b.reshape(d, DIFF_HEADS, DIFF_HEAD_DIM)
    return jnp.concatenate([a, b], axis=-1).reshape(d, DIFF_HEADS * V_HEAD_DIM)


def kernel(x, w_in, lambda_q1, lambda_k1, lambda_q2, lambda_k2, subln_g, gmlp_ln_g, gmlp_ln_b,
           w_spatial, b_spatial, w_o, ln1_g, ln1_b, w_router, b_router, w_up, b_up,
           w_down, b_down, ln2_g, ln2_b):
    b, s, d = x.shape
    n = b * s
    x2 = x.reshape(n, d)

    w = w_in[0]
    c = QK_WIDTH
    scale = DIFF_HEAD_DIM ** -0.5
    wq = _head_pairs(w[:, 0:c], w[:, c:2 * c]) * scale
    wk = _head_pairs(w[:, 2 * c:3 * c], w[:, 3 * c:4 * c])
    wqk = jnp.concatenate([wq, wk], axis=1).astype(BF16)
    wv = w[:, 4 * c:4 * c + DIFF_WIDTH].astype(BF16)
    wg = w[:, 4 * c + DIFF_WIDTH:].astype(BF16)
    lamv = jnp.concatenate([lambda_q1, lambda_k1, lambda_q2, lambda_k2], axis=0)
    bsp = jnp.repeat(b_spatial[0].T, GMLP_GROUP_DIM, axis=1)

    qq, kk, v, u, vn = _proj(x2, wqk, wv, wg, gmlp_ln_g, gmlp_ln_b)
    attn = _attention(qq.reshape(b, s, -1), kk.reshape(b, s, -1), v.reshape(b, s, -1),
                      lamv, subln_g).reshape(n, DIFF_WIDTH)
    x1, eid, gates, rank, cnt = _mix(attn, u, vn, x2, w_spatial[0], bsp, w_o[0].astype(BF16),
                                     ln1_g, ln1_b, w_router[0].T, b_router[0][:, None])

    counts = cnt[:, 0].astype(jnp.int32)
    padded = ((counts + FFN_TM - 1) // FFN_TM) * FFN_TM
    end_padded = jnp.cumsum(padded)
    start_padded = end_padded - padded
    eids = jnp.arange(N_EXPERTS, dtype=jnp.int32)[:, None, None]
    start_of = jnp.sum(jnp.where(eid[None] == eids, start_padded[:, None, None], 0), axis=0)
    dest_flat = (start_of + rank).T.reshape(n * TOP_K)
    p_rows = n * TOP_K + N_EXPERTS * FFN_TM
    nblk = p_rows // FFN_TM
    nused = (end_padded[-1:] // FFN_TM).astype(jnp.int32)
    blk_start = jnp.arange(nblk, dtype=jnp.int32) * FFN_TM
    blk_e = jnp.minimum(jnp.sum((end_padded[None, :] <= blk_start[:, None]).astype(jnp.int32), axis=1),
                        N_EXPERTS - 1)

    xs = _dispatch(x1, dest_flat, start_padded + counts, padded - counts, nused, p_rows)
    y = _ffn(xs, blk_e, nused, w_up[0].astype(BF16), b_up[0][:, None, :],
             w_down[0].astype(BF16), b_down[0][:, None, :])
    out = _combine(y, dest_flat, gates.T, x1, ln2_g, ln2_b)
    return out.reshape(b, s, d)
```

```python
import functools

import jax
import jax.numpy as jnp
from jax import lax
from jax.experimental import pallas as pl
from jax.experimental.pallas import tpu as pltpu

DIFF_HEADS = 4
DIFF_HEAD_DIM = 64
V_HEAD_DIM = 2 * DIFF_HEAD_DIM
QK_WIDTH = DIFF_HEADS * DIFF_HEAD_DIM
DIFF_WIDTH = DIFF_HEADS * V_HEAD_DIM
GMLP_GROUPS = 8
GMLP_GROUP_DIM = 64
GMLP_WIDTH = GMLP_GROUPS * GMLP_GROUP_DIM
CHUNK = 128
N_EXPERTS = 32
TOP_K = 4
SWIGLU_LIMIT = 7.0
SWIGLU_ALPHA = 1.702
LN_EPS = 1e-5
DEPTH = 1
DEEPNORM_ALPHA = (2.0 * DEPTH) ** 0.25
LAMBDA_INIT = 0.8 - 0.6 * 1.0

LANES = 128

PROJ_TM = 512
ATTN_T = 512
MIX_TM = 256
DISPATCH_TM = 512
FFN_TM = 256
COMBINE_TM = 256

VMEM_LIMIT = 48 * 1024 * 1024

BF16 = jnp.bfloat16
F32 = jnp.float32


def _layer_norm(y, g, b):
    mu = jnp.mean(y, axis=-1, keepdims=True)
    yc = y - mu
    var = jnp.mean(yc * yc, axis=-1, keepdims=True)
    return yc * lax.rsqrt(var + LN_EPS) * g + b


def _gelu(x):
    return 0.5 * x * (1.0 + lax.erf(x * (2.0 ** -0.5)))


def _nt_dot(a, b):
    return lax.dot_general(a, b, (((1,), (1,)), ((), ())), preferred_element_type=F32)


def _proj_kernel(x_ref, wqk_ref, wvt_ref, wg_ref, lng_ref, lnb_ref,
                 qq_ref, kk_ref, vt_ref, u_ref, vn_ref):
    xb = x_ref[...].astype(BF16)
    qk = jnp.dot(xb, wqk_ref[...], preferred_element_type=F32)
    qq_ref[...] = qk[:, :DIFF_WIDTH].astype(BF16)
    kk_ref[...] = qk[:, DIFF_WIDTH:].astype(BF16)
    vt_ref[0] = _nt_dot(wvt_ref[...], xb).astype(BF16)
    g = jnp.dot(xb, wg_ref[...], preferred_element_type=F32)
    u_ref[...] = _gelu(g[:, :GMLP_WIDTH]).astype(BF16)
    gv = _gelu(g[:, GMLP_WIDTH:])
    vn_ref[...] = _layer_norm(gv, lng_ref[...], lnb_ref[...]).astype(BF16)


def _proj(x2, wqk, wvt, wg, lng, lnb):
    n, d = x2.shape
    tm = PROJ_TM
    row = lambda i: (i, 0)
    const = lambda i: (0, 0)
    out = jax.ShapeDtypeStruct((n, DIFF_WIDTH), BF16)
    vt_out = jax.ShapeDtypeStruct((n // tm, DIFF_WIDTH, tm), BF16)
    rowspec = pl.BlockSpec((tm, DIFF_WIDTH), row)
    return pl.pallas_call(
        _proj_kernel,
        out_shape=(out, out, vt_out, out, out),
        grid=(n // tm,),
        in_specs=[
            pl.BlockSpec((tm, d), row),
            pl.BlockSpec(wqk.shape, const),
            pl.BlockSpec(wvt.shape, const),
            pl.BlockSpec(wg.shape, const),
            pl.BlockSpec(lng.shape, const),
            pl.BlockSpec(lnb.shape, const),
        ],
        out_specs=[rowspec, rowspec, pl.BlockSpec((1, DIFF_WIDTH, tm), lambda i: (i, 0, 0)),
                   rowspec, rowspec],
        compiler_params=pltpu.CompilerParams(
            dimension_semantics=("arbitrary",), vmem_limit_bytes=VMEM_LIMIT),
        name="proj",
    )(x2, wqk, wvt, wg, lng, lnb)


def _attn_kernel(lamv_ref, g_ref, qq_ref, kk_ref, vt_ref, o_ref):
    t = ATTN_T
    qi = pl.program_id(2)
    q = qq_ref[0]
    lane = lax.broadcasted_iota(jnp.int32, q.shape, 1)
    first = lane < DIFF_HEAD_DIM
    zero = jnp.zeros_like(q)
    q1 = jnp.where(first, q, zero)
    q2 = jnp.where(first, zero, q)

    def scores(j):
        k = kk_ref[0, pl.ds(pl.multiple_of(j * t, t), t), :]
        return _nt_dot(k, q1), _nt_dot(k, q2)

    def consume(j, s_pair, carry, masked):
        vt = vt_ref[0, j]
        new = []
        for s, (m, l, acc) in zip(s_pair, carry):
            if masked:
                key = lax.broadcasted_iota(jnp.int32, s.shape, 0)
                qry = lax.broadcasted_iota(jnp.int32, s.shape, 1)
                s = jnp.where(key <= qry, s, -jnp.inf)
            m_new = jnp.maximum(m, jnp.max(s, axis=0, keepdims=True))
            alpha = jnp.exp(m - m_new)
            p = jnp.exp(s - m_new)
            l_new = alpha * l + jnp.sum(p, axis=0, keepdims=True)
            acc_new = alpha * acc + jnp.dot(vt, p.astype(BF16), preferred_element_type=F32)
            new.append((m_new, l_new, acc_new))
        return tuple(new)

    def init():
        return (jnp.full((1, t), -jnp.inf, F32), jnp.zeros((1, t), F32),
                jnp.zeros((V_HEAD_DIM, t), F32))

    carry = lax.fori_loop(0, qi, lambda j, c: consume(j, scores(j), c, False), (init(), init()))
    (_, l1, acc1), (_, l2, acc2) = consume(qi, scores(qi), carry, True)

    lv = lamv_ref[...]
    lam = (jnp.exp(jnp.sum(lv[0:1] * lv[1:2], axis=-1, keepdims=True))
           - jnp.exp(jnp.sum(lv[2:3] * lv[3:4], axis=-1, keepdims=True)) + LAMBDA_INIT)
    o = acc1 / l1 - lam * (acc2 / l2)
    ms = jnp.mean(o * o, axis=0, keepdims=True)
    o = o * lax.rsqrt(ms + LN_EPS) * g_ref[...] * (1.0 - LAMBDA_INIT)
    o_ref[0] = o.T.astype(BF16)


def _attention(qq, kk, vt, lamv, subln_g_col):
    b, s, _ = qq.shape
    t = ATTN_T
    qspec = pl.BlockSpec((1, t, V_HEAD_DIM), lambda bi, h, qi: (bi, qi, h))
    kspec = pl.BlockSpec((1, s, V_HEAD_DIM), lambda bi, h, qi: (bi, 0, h))
    vtspec = pl.BlockSpec((1, s // t, V_HEAD_DIM, t), lambda bi, h, qi: (bi, 0, h, 0))
    const = lambda bi, h, qi: (0, 0)
    return pl.pallas_call(
        _attn_kernel,
        out_shape=jax.ShapeDtypeStruct((b, s, DIFF_WIDTH), BF16),
        grid=(b, DIFF_HEADS, s // t),
        in_specs=[pl.BlockSpec(lamv.shape, const), pl.BlockSpec(subln_g_col.shape, const),
                  qspec, kspec, vtspec],
        out_specs=qspec,
        compiler_params=pltpu.CompilerParams(
            dimension_semantics=("arbitrary",) * 3, vmem_limit_bytes=VMEM_LIMIT),
        name="attn",
    )(lamv, subln_g_col, qq, kk, vt)


def _mix_kernel(attn_ref, u_ref, vn_ref, x_ref, wsp_ref, bsp_ref, wo_ref, g1_ref, b1_ref,
                wrt_ref, br_ref,
                x1_ref, eid_ref, gate_ref, rank_ref, cnt_ref,
                cat_ref, carry_ref):
    tm = MIX_TM

    @pl.when(pl.program_id(0) == 0)
    def _():
        carry_ref[...] = jnp.zeros_like(carry_ref)

    ri = lax.broadcasted_iota(jnp.int32, (CHUNK, CHUNK), 0)
    ci = lax.broadcasted_iota(jnp.int32, (CHUNK, CHUNK), 1)
    tril = ci <= ri
    first = ci < GMLP_GROUP_DIM
    w = [jnp.where(tril, wsp_ref[g], 0.0).astype(BF16) for g in range(GMLP_GROUPS)]
    cat_ref[:, :DIFF_WIDTH] = attn_ref[...]
    for c in range(tm // CHUNK):
        rows = slice(c * CHUNK, (c + 1) * CHUNK)
        for jb in range(GMLP_WIDTH // LANES):
            cols = slice(jb * LANES, (jb + 1) * LANES)
            vb = vn_ref[rows, cols]
            zero = jnp.zeros_like(vb)
            z = (jnp.dot(w[2 * jb], jnp.where(first, vb, zero), preferred_element_type=F32)
                 + jnp.dot(w[2 * jb + 1], jnp.where(first, zero, vb), preferred_element_type=F32))
            gated = u_ref[rows, cols].astype(F32) * (z + bsp_ref[:, cols])
            cat_ref[rows, DIFF_WIDTH + jb * LANES:DIFF_WIDTH + (jb + 1) * LANES] = gated.astype(BF16)

    mixed = jnp.dot(cat_ref[...], wo_ref[...], preferred_element_type=F32)
    x1 = _layer_norm(DEEPNORM_ALPHA * x_ref[...] + mixed, g1_ref[...], b1_ref[...])
    x1_ref[...] = x1

    xh = x1.astype(BF16)
    xl = (x1 - xh.astype(F32)).astype(BF16)
    wr = wrt_ref[...]
    wh = wr.astype(BF16)
    wl = (wr - wh.astype(F32)).astype(BF16)
    logits = _nt_dot(wh, xh) + _nt_dot(wl, xh) + _nt_dot(wh, xl) + br_ref[...]

    eio = lax.broadcasted_iota(jnp.int32, logits.shape, 0).astype(F32)
    vals, idxs, sels = [], [], []
    cur = logits
    for _ in range(TOP_K):
        mx = jnp.max(cur, axis=0, keepdims=True)
        idx = jnp.min(jnp.where(cur == mx, eio, float(N_EXPERTS)), axis=0, keepdims=True)
        sel = eio == idx
        vals.append(mx)
        idxs.append(idx)
        sels.append(sel)
        cur = jnp.where(sel, -jnp.inf, cur)
    ex = [jnp.exp(vk - vals[0]) for vk in vals]
    denom = ex[0] + ex[1] + ex[2] + ex[3]
    gate_ref[...] = jnp.concatenate([e / denom for e in ex], axis=0)
    eid_ref[...] = jnp.concatenate(idxs, axis=0).astype(jnp.int32)

    chosen = (sels[0] | sels[1] | sels[2] | sels[3])
    onehot = jnp.where(chosen, 1.0, 0.0)
    ti = lax.broadcasted_iota(jnp.int32, (tm, tm), 0)
    tj = lax.broadcasted_iota(jnp.int32, (tm, tm), 1)
    before = jnp.where(ti < tj, 1.0, 0.0).astype(BF16)
    carry = carry_ref[:, 0:1]
    cnt_before = jnp.dot(onehot.astype(BF16), before, preferred_element_type=F32) + carry
    ranks = [jnp.sum(jnp.where(s, cnt_before, 0.0), axis=0, keepdims=True) for s in sels]
    rank_ref[...] = jnp.concatenate(ranks, axis=0).astype(jnp.int32)
    total = carry + jnp.sum(onehot, axis=1, keepdims=True)
    carry_ref[...] = jnp.broadcast_to(total, carry_ref.shape)
    cnt_ref[...] = jnp.broadcast_to(total, cnt_ref.shape)


def _mix(attn, u, vn, x2, wsp, bsp, wo, g1, b1, wrt, br):
    n, d = x2.shape
    tm = MIX_TM
    row = lambda i: (i, 0)
    col = lambda i: (0, i)
    const2 = lambda i: (0, 0)
    const3 = lambda i: (0, 0, 0)
    tok = lambda dt: jax.ShapeDtypeStruct((TOP_K, n), dt)
    return pl.pallas_call(
        _mix_kernel,
        out_shape=(jax.ShapeDtypeStruct((n, d), F32), tok(jnp.int32), tok(F32), tok(jnp.int32),
                   jax.ShapeDtypeStruct((N_EXPERTS, LANES), F32)),
        grid=(n // tm,),
        in_specs=[
            pl.BlockSpec((tm, DIFF_WIDTH), row),
            pl.BlockSpec((tm, GMLP_WIDTH), row),
            pl.BlockSpec((tm, GMLP_WIDTH), row),
            pl.BlockSpec((tm, d), row),
            pl.BlockSpec(wsp.shape, const3),
            pl.BlockSpec(bsp.shape, const2),
            pl.BlockSpec(wo.shape, const2),
            pl.BlockSpec(g1.shape, const2),
            pl.BlockSpec(b1.shape, const2),
            pl.BlockSpec(wrt.shape, const2),
            pl.BlockSpec(br.shape, const2),
        ],
        out_specs=[
            pl.BlockSpec((tm, d), row),
            pl.BlockSpec((TOP_K, tm), col),
            pl.BlockSpec((TOP_K, tm), col),
            pl.BlockSpec((TOP_K, tm), col),
            pl.BlockSpec((N_EXPERTS, LANES), const2),
        ],
        scratch_shapes=[pltpu.VMEM((tm, DIFF_WIDTH + GMLP_WIDTH), BF16),
                        pltpu.VMEM((N_EXPERTS, LANES), F32)],
        compiler_params=pltpu.CompilerParams(
            dimension_semantics=("arbitrary",), vmem_limit_bytes=VMEM_LIMIT),
        name="mix",
    )(attn, u, vn, x2, wsp, bsp, wo, g1, b1, wrt, br)


def _dispatch_kernel(pad_start_ref, pad_len_ref, nused_ref, dest_ref, x1_ref, xs_hbm, zblk_ref,
                     sem, zsem):
    tm = DISPATCH_TM
    nblk = xs_hbm.shape[0] // FFN_TM
    i = pl.program_id(0)

    def row_copy(t, k):
        return pltpu.make_async_copy(x1_ref.at[pl.ds(t, 1)],
                                     xs_hbm.at[pl.ds(dest_ref[t * TOP_K + k], 1)], sem)

    def issue(t, _):
        for k in range(TOP_K):
            row_copy(t, k).start(priority=k % 2)
        return 0

    lax.fori_loop(0, tm, issue, 0)

    @pl.when(i == 0)
    def _():
        zblk_ref[...] = jnp.zeros_like(zblk_ref)

        def zero_copy(r):
            return pltpu.make_async_copy(zblk_ref.at[pl.ds(0, 1)], xs_hbm.at[pl.ds(r, 1)], zsem)

        def tail_copy(blk):
            return pltpu.make_async_copy(
                zblk_ref, xs_hbm.at[pl.ds(pl.multiple_of(blk * FFN_TM, FFN_TM), FFN_TM)], zsem)

        def tail_start(blk, c):
            tail_copy(blk).start()
            return c

        def tail_wait(blk, c):
            tail_copy(blk).wait()
            return c

        lax.fori_loop(nused_ref[0], nblk, tail_start, 0)
        lax.fori_loop(nused_ref[0], nblk, tail_wait, 0)

        def per_expert(e, _):
            s = pad_start_ref[e]
            cnt = pad_len_ref[e]

            def start(r, c):
                zero_copy(s + r).start()
                return c

            def wait(r, c):
                zero_copy(s + r).wait()
                return c

            lax.fori_loop(0, cnt, start, 0)
            lax.fori_loop(0, cnt, wait, 0)
            return 0

        lax.fori_loop(0, N_EXPERTS, per_expert, 0)

    for _ in range(TOP_K):
        pltpu.make_async_copy(x1_ref, xs_hbm.at[pl.ds(0, tm)], sem).wait()


def _dispatch(x1, dest_flat, pad_start, pad_len, nused, p_rows):
    n, d = x1.shape
    tm = DISPATCH_TM
    return pl.pallas_call(
        _dispatch_kernel,
        out_shape=jax.ShapeDtypeStruct((p_rows, d), F32),
        grid_spec=pltpu.PrefetchScalarGridSpec(
            num_scalar_prefetch=3,
            grid=(n // tm,),
            in_specs=[
                pl.BlockSpec((tm * TOP_K,), lambda i, *_: (i,), memory_space=pltpu.SMEM),
                pl.BlockSpec((tm, d), lambda i, *_: (i, 0)),
            ],
            out_specs=pl.BlockSpec(memory_space=pl.ANY),
            scratch_shapes=[pltpu.VMEM((FFN_TM, d), F32), pltpu.SemaphoreType.DMA(()),
                            pltpu.SemaphoreType.DMA(())],
        ),
        compiler_params=pltpu.CompilerParams(
            dimension_semantics=("arbitrary",), vmem_limit_bytes=VMEM_LIMIT),
        name="dispatch",
    )(pad_start, pad_len, nused, dest_flat, x1)


def _ffn_kernel(blk_e_ref, nused_ref, xs_ref, wup_ref, bup_ref, wdn_ref, bdn_ref, y_ref):
    dff = wdn_ref.shape[1]
    used = pl.program_id(0) < nused_ref[0]

    @pl.when(jnp.logical_not(used))
    def _():
        y_ref[...] = jnp.zeros_like(y_ref)

    @pl.when(used)
    def _():
        xb = xs_ref[...].astype(BF16)
        h = jnp.dot(xb, wup_ref[0], preferred_element_type=F32) + bup_ref[0]
        gate = jnp.minimum(h[:, :dff], SWIGLU_LIMIT)
        lin = jnp.clip(h[:, dff:], -SWIGLU_LIMIT, SWIGLU_LIMIT)
        act = (lin + 1.0) * gate * jax.nn.sigmoid(SWIGLU_ALPHA * gate)
        y_ref[...] = jnp.dot(act.astype(BF16), wdn_ref[0], preferred_element_type=F32) + bdn_ref[0]


def _ffn(xs, blk_e, nused, wup, bup, wdn, bdn):
    p_rows, d = xs.shape
    tm = FFN_TM
    e, _, dff2 = wup.shape
    dff = dff2 // 2
    rowblk = lambda i, be, nu: (i, 0)
    wsel = lambda i, be, nu: (be[i], 0, 0)
    return pl.pallas_call(
        _ffn_kernel,
        out_shape=jax.ShapeDtypeStruct((p_rows, d), F32),
        grid_spec=pltpu.PrefetchScalarGridSpec(
            num_scalar_prefetch=2,
            grid=(p_rows // tm,),
            in_specs=[
                pl.BlockSpec((tm, d), rowblk),
                pl.BlockSpec((1, d, dff2), wsel),
                pl.BlockSpec((1, 1, dff2), wsel),
                pl.BlockSpec((1, dff, d), wsel),
                pl.BlockSpec((1, 1, d), wsel),
            ],
            out_specs=pl.BlockSpec((tm, d), rowblk),
        ),
        compiler_params=pltpu.CompilerParams(
            dimension_semantics=("arbitrary",), vmem_limit_bytes=VMEM_LIMIT),
        name="ffn",
    )(blk_e, nused, xs, wup, bup, wdn, bdn)


def _combine_kernel(dcur_ref, dnxt_ref, gate_ref, x1_ref, g2_ref, b2_ref, y_hbm, o_ref, ybuf, sem):
    tm = COMBINE_TM
    i = pl.program_id(0)
    nsteps = pl.num_programs(0)
    slot = lax.rem(i, 2)

    def row_copy(dref, t, k, sl):
        return pltpu.make_async_copy(y_hbm.at[pl.ds(dref[t * TOP_K + k], 1)],
                                     ybuf.at[sl, k, pl.ds(t, 1)], sem.at[sl])

    def issue(dref, sl):
        def body(t, _):
            for k in range(TOP_K):
                row_copy(dref, t, k, sl).start(priority=k % 2)
            return 0
        lax.fori_loop(0, tm, body, 0)

    @pl.when(i == 0)
    def _():
        issue(dcur_ref, 0)

    @pl.when(i + 1 < nsteps)
    def _():
        issue(dnxt_ref, 1 - slot)

    for k in range(TOP_K):
        pltpu.make_async_copy(y_hbm.at[pl.ds(0, tm)], ybuf.at[slot, k], sem.at[slot]).wait()

    gates = gate_ref[...]
    ffn = gates[:, 0:1] * ybuf[slot, 0]
    for k in range(1, TOP_K):
        ffn = ffn + gates[:, k:k + 1] * ybuf[slot, k]
    o_ref[...] = _layer_norm(DEEPNORM_ALPHA * x1_ref[...] + ffn, g2_ref[...], b2_ref[...])


def _combine(y, dest_flat, gates_tok, x1, g2, b2):
    n, d = x1.shape
    tm = COMBINE_TM
    nsteps = n // tm
    row = lambda i: (i, 0)
    const = lambda i: (0, 0)
    return pl.pallas_call(
        _combine_kernel,
        out_shape=jax.ShapeDtypeStruct((n, d), F32),
        grid=(nsteps,),
        in_specs=[
            pl.BlockSpec((tm * TOP_K,), lambda i: (i,), memory_space=pltpu.SMEM),
            pl.BlockSpec((tm * TOP_K,), lambda i: (jnp.minimum(i + 1, nsteps - 1),),
                         memory_space=pltpu.SMEM),
            pl.BlockSpec((tm, TOP_K), row),
            pl.BlockSpec((tm, d), row),
            pl.BlockSpec(g2.shape, const),
            pl.BlockSpec(b2.shape, const),
            pl.BlockSpec(memory_space=pl.ANY),
        ],
        out_specs=pl.BlockSpec((tm, d), row),
        scratch_shapes=[pltpu.VMEM((2, TOP_K, tm, d), F32), pltpu.SemaphoreType.DMA((2,))],
        compiler_params=pltpu.CompilerParams(
            dimension_semantics=("arbitrary",), vmem_limit_bytes=VMEM_LIMIT),
        name="combine",
    )(dest_flat, dest_flat, gates_tok, x1, g2, b2, y)


def _head_pairs(w_a, w_b):
    d = w_a.shape[0]
    a = w_a.reshape(d, DIFF_HEADS, DIFF_HEAD_DIM)
    b = w_b.reshape(d, DIFF_HEADS, DIFF_HEAD_DIM)
    return jnp.concatenate([a, b], axis=-1).reshape(d, DIFF_HEADS * V_HEAD_DIM)


def kernel(x, w_in, lambda_q1, lambda_k1, lambda_q2, lambda_k2, subln_g, gmlp_ln_g, gmlp_ln_b,
           w_spatial, b_spatial, w_o, ln1_g, ln1_b, w_router, b_router, w_up, b_up,
           w_down, b_down, ln2_g, ln2_b):
    b, s, d = x.shape
    n = b * s
    x2 = x.reshape(n, d)

    w = w_in[0]
    c = QK_WIDTH
    scale = DIFF_HEAD_DIM ** -0.5
    wq = _head_pairs(w[:, 0:c], w[:, c:2 * c]) * scale
    wk = _head_pairs(w[:, 2 * c:3 * c], w[:, 3 * c:4 * c])
    wqk = jnp.concatenate([wq, wk], axis=1).astype(BF16)
    wvt = w[:, 4 * c:4 * c + DIFF_WIDTH].T.astype(BF16)
    wg = w[:, 4 * c + DIFF_WIDTH:].astype(BF16)
    lamv = jnp.concatenate([lambda_q1, lambda_k1, lambda_q2, lambda_k2], axis=0)
    bsp = jnp.repeat(b_spatial[0].T, GMLP_GROUP_DIM, axis=1)

    assert PROJ_TM == ATTN_T and s % ATTN_T == 0
    qq, kk, vt, u, vn = _proj(x2, wqk, wvt, wg, gmlp_ln_g, gmlp_ln_b)
    attn = _attention(qq.reshape(b, s, -1), kk.reshape(b, s, -1),
                      vt.reshape(b, s // ATTN_T, DIFF_WIDTH, ATTN_T),
                      lamv, subln_g.reshape(V_HEAD_DIM, 1)).reshape(n, DIFF_WIDTH)
    x1, eid, gates, rank, cnt = _mix(attn, u, vn, x2, w_spatial[0], bsp, w_o[0].astype(BF16),
                                     ln1_g, ln1_b, w_router[0].T, b_router[0][:, None])

    counts = cnt[:, 0].astype(jnp.int32)
    padded = ((counts + FFN_TM - 1) // FFN_TM) * FFN_TM
    end_padded = jnp.cumsum(padded)
    start_padded = end_padded - padded
    eids = jnp.arange(N_EXPERTS, dtype=jnp.int32)[:, None, None]
    start_of = jnp.sum(jnp.where(eid[None] == eids, start_padded[:, None, None], 0), axis=0)
    dest_flat = (start_of + rank).T.reshape(n * TOP_K)
    p_rows = n * TOP_K + N_EXPERTS * FFN_TM
    nblk = p_rows // FFN_TM
    nused = (end_padded[-1:] // FFN_TM).astype(jnp.int32)
    blk_start = jnp.arange(nblk, dtype=jnp.int32) * FFN_TM
    blk_e = jnp.minimum(jnp.sum((end_padded[None, :] <= blk_start[:, None]).astype(jnp.int32), axis=1),
                        N_EXPERTS - 1)

    xs = _dispatch(x1, dest_flat, start_padded + counts, padded - counts, nused, p_rows)
    y = _ffn(xs, blk_e, nused, w_up[0].astype(BF16), b_up[0][:, None, :],
             w_down[0].astype(BF16), b_down[0][:, None, :])
    out = _combine(y, dest_flat, gates.T, x1, ln2_g, ln2_b)
    return out.reshape(b, s, d)
```

```python
import functools

import jax
import jax.numpy as jnp
from jax import lax
from jax.experimental import pallas as pl
from jax.experimental.pallas import tpu as pltpu

DIFF_HEADS = 4
DIFF_HEAD_DIM = 64
V_HEAD_DIM = 2 * DIFF_HEAD_DIM
QK_WIDTH = DIFF_HEADS * DIFF_HEAD_DIM
DIFF_WIDTH = DIFF_HEADS * V_HEAD_DIM
GMLP_GROUPS = 8
GMLP_GROUP_DIM = 64
GMLP_WIDTH = GMLP_GROUPS * GMLP_GROUP_DIM
CHUNK = 128
N_EXPERTS = 32
TOP_K = 4
SWIGLU_LIMIT = 7.0
SWIGLU_ALPHA = 1.702
LN_EPS = 1e-5
DEPTH = 1
DEEPNORM_ALPHA = (2.0 * DEPTH) ** 0.25
LAMBDA_INIT = 0.8 - 0.6 * 1.0

LANES = 128

PROJ_TM = 512
ATTN_T = 512
MIX_TM = 256
DISPATCH_TM = 512
FFN_TM = 256
FFN_CHUNK = 256
COMBINE_TM = 256

VMEM_LIMIT = 48 * 1024 * 1024

BF16 = jnp.bfloat16
F32 = jnp.float32


def _layer_norm(y, g, b):
    mu = jnp.mean(y, axis=-1, keepdims=True)
    yc = y - mu
    var = jnp.mean(yc * yc, axis=-1, keepdims=True)
    return yc * lax.rsqrt(var + LN_EPS) * g + b


def _gelu(x):
    return 0.5 * x * (1.0 + lax.erf(x * (2.0 ** -0.5)))


def _nt_dot(a, b):
    return lax.dot_general(a, b, (((1,), (1,)), ((), ())), preferred_element_type=F32)


def _proj_kernel(x_ref, wqk_ref, wvt_ref, wg_ref, lng_ref, lnb_ref,
                 qq_ref, kk_ref, vt_ref, u_ref, vn_ref):
    xb = x_ref[...].astype(BF16)
    qk = jnp.dot(xb, wqk_ref[...], preferred_element_type=F32)
    qq_ref[...] = qk[:, :DIFF_WIDTH].astype(BF16)
    kk_ref[...] = qk[:, DIFF_WIDTH:].astype(BF16)
    vt_ref[0] = _nt_dot(wvt_ref[...], xb).astype(BF16)
    g = jnp.dot(xb, wg_ref[...], preferred_element_type=F32)
    u_ref[...] = _gelu(g[:, :GMLP_WIDTH]).astype(BF16)
    gv = _gelu(g[:, GMLP_WIDTH:])
    vn_ref[...] = _layer_norm(gv, lng_ref[...], lnb_ref[...]).astype(BF16)


def _proj(x2, wqk, wvt, wg, lng, lnb):
    n, d = x2.shape
    tm = PROJ_TM
    row = lambda i: (i, 0)
    const = lambda i: (0, 0)
    out = jax.ShapeDtypeStruct((n, DIFF_WIDTH), BF16)
    vt_out = jax.ShapeDtypeStruct((n // tm, DIFF_WIDTH, tm), BF16)
    rowspec = pl.BlockSpec((tm, DIFF_WIDTH), row)
    return pl.pallas_call(
        _proj_kernel,
        out_shape=(out, out, vt_out, out, out),
        grid=(n // tm,),
        in_specs=[
            pl.BlockSpec((tm, d), row),
            pl.BlockSpec(wqk.shape, const),
            pl.BlockSpec(wvt.shape, const),
            pl.BlockSpec(wg.shape, const),
            pl.BlockSpec(lng.shape, const),
            pl.BlockSpec(lnb.shape, const),
        ],
        out_specs=[rowspec, rowspec, pl.BlockSpec((1, DIFF_WIDTH, tm), lambda i: (i, 0, 0)),
                   rowspec, rowspec],
        compiler_params=pltpu.CompilerParams(
            dimension_semantics=("arbitrary",), vmem_limit_bytes=VMEM_LIMIT),
        name="proj",
    )(x2, wqk, wvt, wg, lng, lnb)


def _attn_kernel(lamv_ref, g_ref, qq_ref, kk_ref, vt_ref, o_ref):
    t = ATTN_T
    qi = pl.program_id(2)
    q = qq_ref[0]
    lane = lax.broadcasted_iota(jnp.int32, q.shape, 1)
    first = lane < DIFF_HEAD_DIM
    zero = jnp.zeros_like(q)
    q1 = jnp.where(first, q, zero)
    q2 = jnp.where(first, zero, q)

    def scores(j):
        k = kk_ref[0, pl.ds(pl.multiple_of(j * t, t), t), :]
        return _nt_dot(k, q1), _nt_dot(k, q2)

    def consume(j, s_pair, carry, masked):
        vt = vt_ref[0, j]
        new = []
        for s, (m, l, acc) in zip(s_pair, carry):
            if masked:
                key = lax.broadcasted_iota(jnp.int32, s.shape, 0)
                qry = lax.broadcasted_iota(jnp.int32, s.shape, 1)
                s = jnp.where(key <= qry, s, -jnp.inf)
            m_new = jnp.maximum(m, jnp.max(s, axis=0, keepdims=True))
            alpha = jnp.exp(m - m_new)
            p = jnp.exp(s - m_new)
            l_new = alpha * l + jnp.sum(p, axis=0, keepdims=True)
            acc_new = alpha * acc + jnp.dot(vt, p.astype(BF16), preferred_element_type=F32)
            new.append((m_new, l_new, acc_new))
        return tuple(new)

    def init():
        return (jnp.full((1, t), -jnp.inf, F32), jnp.zeros((1, t), F32),
                jnp.zeros((V_HEAD_DIM, t), F32))

    carry = lax.fori_loop(0, qi, lambda j, c: consume(j, scores(j), c, False), (init(), init()))
    (_, l1, acc1), (_, l2, acc2) = consume(qi, scores(qi), carry, True)

    lv = lamv_ref[...]
    lam = (jnp.exp(jnp.sum(lv[0:1] * lv[1:2], axis=-1, keepdims=True))
           - jnp.exp(jnp.sum(lv[2:3] * lv[3:4], axis=-1, keepdims=True)) + LAMBDA_INIT)
    o = acc1 / l1 - lam * (acc2 / l2)
    ms = jnp.mean(o * o, axis=0, keepdims=True)
    o = o * lax.rsqrt(ms + LN_EPS) * g_ref[...] * (1.0 - LAMBDA_INIT)
    o_ref[0] = o.T.astype(BF16)


def _attention(qq, kk, vt, lamv, subln_g_col):
    b, s, _ = qq.shape
    t = ATTN_T
    qspec = pl.BlockSpec((1, t, V_HEAD_DIM), lambda bi, h, qi: (bi, qi, h))
    kspec = pl.BlockSpec((1, s, V_HEAD_DIM), lambda bi, h, qi: (bi, 0, h))
    vtspec = pl.BlockSpec((1, s // t, V_HEAD_DIM, t), lambda bi, h, qi: (bi, 0, h, 0))
    const = lambda bi, h, qi: (0, 0)
    return pl.pallas_call(
        _attn_kernel,
        out_shape=jax.ShapeDtypeStruct((b, s, DIFF_WIDTH), BF16),
        grid=(b, DIFF_HEADS, s // t),
        in_specs=[pl.BlockSpec(lamv.shape, const), pl.BlockSpec(subln_g_col.shape, const),
                  qspec, kspec, vtspec],
        out_specs=qspec,
        compiler_params=pltpu.CompilerParams(
            dimension_semantics=("arbitrary",) * 3, vmem_limit_bytes=VMEM_LIMIT),
        name="attn",
    )(lamv, subln_g_col, qq, kk, vt)


def _mix_kernel(attn_ref, u_ref, vn_ref, x_ref, wsp_ref, bsp_ref, wo_ref, g1_ref, b1_ref,
                wrt_ref, br_ref,
                x1_ref, eid_ref, gate_ref, rank_ref, cnt_ref,
                cat_ref, carry_ref):
    tm = MIX_TM

    @pl.when(pl.program_id(0) == 0)
    def _():
        carry_ref[...] = jnp.zeros_like(carry_ref)

    ri = lax.broadcasted_iota(jnp.int32, (CHUNK, CHUNK), 0)
    ci = lax.broadcasted_iota(jnp.int32, (CHUNK, CHUNK), 1)
    tril = ci <= ri
    first = ci < GMLP_GROUP_DIM
    w = [jnp.where(tril, wsp_ref[g], 0.0).astype(BF16) for g in range(GMLP_GROUPS)]
    cat_ref[:, :DIFF_WIDTH] = attn_ref[...]
    for c in range(tm // CHUNK):
        rows = slice(c * CHUNK, (c + 1) * CHUNK)
        for jb in range(GMLP_WIDTH // LANES):
            cols = slice(jb * LANES, (jb + 1) * LANES)
            vb = vn_ref[rows, cols]
            zero = jnp.zeros_like(vb)
            z = (jnp.dot(w[2 * jb], jnp.where(first, vb, zero), preferred_element_type=F32)
                 + jnp.dot(w[2 * jb + 1], jnp.where(first, zero, vb), preferred_element_type=F32))
            gated = u_ref[rows, cols].astype(F32) * (z + bsp_ref[:, cols])
            cat_ref[rows, DIFF_WIDTH + jb * LANES:DIFF_WIDTH + (jb + 1) * LANES] = gated.astype(BF16)

    mixed = jnp.dot(cat_ref[...], wo_ref[...], preferred_element_type=F32)
    x1 = _layer_norm(DEEPNORM_ALPHA * x_ref[...] + mixed, g1_ref[...], b1_ref[...])
    x1_ref[...] = x1

    xh = x1.astype(BF16)
    xl = (x1 - xh.astype(F32)).astype(BF16)
    wr = wrt_ref[...]
    wh = wr.astype(BF16)
    wl = (wr - wh.astype(F32)).astype(BF16)
    logits = _nt_dot(wh, xh) + _nt_dot(wl, xh) + _nt_dot(wh, xl) + br_ref[...]

    eio = lax.broadcasted_iota(jnp.int32, logits.shape, 0).astype(F32)
    vals, idxs, sels = [], [], []
    cur = logits
    for _ in range(TOP_K):
        mx = jnp.max(cur, axis=0, keepdims=True)
        idx = jnp.min(jnp.where(cur == mx, eio, float(N_EXPERTS)), axis=0, keepdims=True)
        sel = eio == idx
        vals.append(mx)
        idxs.append(idx)
        sels.append(sel)
        cur = jnp.where(sel, -jnp.inf, cur)
    ex = [jnp.exp(vk - vals[0]) for vk in vals]
    denom = ex[0] + ex[1] + ex[2] + ex[3]
    gate_ref[...] = jnp.concatenate([e / denom for e in ex], axis=0)
    eid_ref[...] = jnp.concatenate(idxs, axis=0).astype(jnp.int32)

    chosen = (sels[0] | sels[1] | sels[2] | sels[3])
    onehot = jnp.where(chosen, 1.0, 0.0)
    ti = lax.broadcasted_iota(jnp.int32, (tm, tm), 0)
    tj = lax.broadcasted_iota(jnp.int32, (tm, tm), 1)
    before = jnp.where(ti < tj, 1.0, 0.0).astype(BF16)
    carry = carry_ref[:, 0:1]
    cnt_before = jnp.dot(onehot.astype(BF16), before, preferred_element_type=F32) + carry
    ranks = [jnp.sum(jnp.where(s, cnt_before, 0.0), axis=0, keepdims=True) for s in sels]
    rank_ref[...] = jnp.concatenate(ranks, axis=0).astype(jnp.int32)
    total = carry + jnp.sum(onehot, axis=1, keepdims=True)
    carry_ref[...] = jnp.broadcast_to(total, carry_ref.shape)
    cnt_ref[...] = jnp.broadcast_to(total, cnt_ref.shape)


def _mix(attn, u, vn, x2, wsp, bsp, wo, g1, b1, wrt, br):
    n, d = x2.shape
    tm = MIX_TM
    row = lambda i: (i, 0)
    col = lambda i: (0, i)
    const2 = lambda i: (0, 0)
    const3 = lambda i: (0, 0, 0)
    tok = lambda dt: jax.ShapeDtypeStruct((TOP_K, n), dt)
    return pl.pallas_call(
        _mix_kernel,
        out_shape=(jax.ShapeDtypeStruct((n, d), F32), tok(jnp.int32), tok(F32), tok(jnp.int32),
                   jax.ShapeDtypeStruct((N_EXPERTS, LANES), F32)),
        grid=(n // tm,),
        in_specs=[
            pl.BlockSpec((tm, DIFF_WIDTH), row),
            pl.BlockSpec((tm, GMLP_WIDTH), row),
            pl.BlockSpec((tm, GMLP_WIDTH), row),
            pl.BlockSpec((tm, d), row),
            pl.BlockSpec(wsp.shape, const3),
            pl.BlockSpec(bsp.shape, const2),
            pl.BlockSpec(wo.shape, const2),
            pl.BlockSpec(g1.shape, const2),
            pl.BlockSpec(b1.shape, const2),
            pl.BlockSpec(wrt.shape, const2),
            pl.BlockSpec(br.shape, const2),
        ],
        out_specs=[
            pl.BlockSpec((tm, d), row),
            pl.BlockSpec((TOP_K, tm), col),
            pl.BlockSpec((TOP_K, tm), col),
            pl.BlockSpec((TOP_K, tm), col),
            pl.BlockSpec((N_EXPERTS, LANES), const2),
        ],
        scratch_shapes=[pltpu.VMEM((tm, DIFF_WIDTH + GMLP_WIDTH), BF16),
                        pltpu.VMEM((N_EXPERTS, LANES), F32)],
        compiler_params=pltpu.CompilerParams(
            dimension_semantics=("arbitrary",), vmem_limit_bytes=VMEM_LIMIT),
        name="mix",
    )(attn, u, vn, x2, wsp, bsp, wo, g1, b1, wrt, br)


def _dispatch_kernel(pad_start_ref, pad_len_ref, nused_ref, dest_ref, x1_ref, xs_hbm, aug_ref,
                     zblk_ref, sem, zsem):
    tm = DISPATCH_TM
    d = x1_ref.shape[1]
    n = pl.num_programs(0) * tm
    nblk = xs_hbm.shape[0] // FFN_TM
    i = pl.program_id(0)

    tok = i * tm + lax.broadcasted_iota(jnp.int32, (tm, LANES), 0)
    for k in range(TOP_K):
        aug_ref[k, :, :d] = x1_ref[...]
        aug_ref[k, :, d:] = (tok + k * n).astype(F32)

    def row_copy(t, k):
        return pltpu.make_async_copy(aug_ref.at[k, pl.ds(t, 1)],
                                     xs_hbm.at[pl.ds(dest_ref[t * TOP_K + k], 1)], sem)

    def issue(t, _):
        for k in range(TOP_K):
            row_copy(t, k).start(priority=k % 2)
        return 0

    lax.fori_loop(0, tm, issue, 0)

    @pl.when(i == 0)
    def _():
        zblk_ref[:, :d] = jnp.zeros((FFN_TM, d), F32)
        zblk_ref[:, d:] = jnp.full((FFN_TM, LANES), TOP_K * n, jnp.int32).astype(F32)

        def zero_copy(r):
            return pltpu.make_async_copy(zblk_ref.at[pl.ds(0, 1)], xs_hbm.at[pl.ds(r, 1)], zsem)

        def tail_copy(blk):
            return pltpu.make_async_copy(
                zblk_ref, xs_hbm.at[pl.ds(pl.multiple_of(blk * FFN_TM, FFN_TM), FFN_TM)], zsem)

        def tail_start(blk, c):
            tail_copy(blk).start()
            return c

        def tail_wait(blk, c):
            tail_copy(blk).wait()
            return c

        lax.fori_loop(nused_ref[0], nblk, tail_start, 0)
        lax.fori_loop(nused_ref[0], nblk, tail_wait, 0)

        def per_expert(e, _):
            s = pad_start_ref[e]
            cnt = pad_len_ref[e]

            def start(r, c):
                zero_copy(s + r).start()
                return c

            def wait(r, c):
                zero_copy(s + r).wait()
                return c

            lax.fori_loop(0, cnt, start, 0)
            lax.fori_loop(0, cnt, wait, 0)
            return 0

        lax.fori_loop(0, N_EXPERTS, per_expert, 0)

    for k in range(TOP_K):
        pltpu.make_async_copy(aug_ref.at[k], xs_hbm.at[pl.ds(0, tm)], sem).wait()


def _dispatch(x1, dest_flat, pad_start, pad_len, nused, p_rows):
    n, d = x1.shape
    tm = DISPATCH_TM
    return pl.pallas_call(
        _dispatch_kernel,
        out_shape=jax.ShapeDtypeStruct((p_rows, d + LANES), F32),
        grid_spec=pltpu.PrefetchScalarGridSpec(
            num_scalar_prefetch=3,
            grid=(n // tm,),
            in_specs=[
                pl.BlockSpec((tm * TOP_K,), lambda i, *_: (i,), memory_space=pltpu.SMEM),
                pl.BlockSpec((tm, d), lambda i, *_: (i, 0)),
            ],
            out_specs=pl.BlockSpec(memory_space=pl.ANY),
            scratch_shapes=[pltpu.VMEM((TOP_K, tm, d + LANES), F32),
                            pltpu.VMEM((FFN_TM, d + LANES), F32),
                            pltpu.SemaphoreType.DMA(()), pltpu.SemaphoreType.DMA(())],
        ),
        compiler_params=pltpu.CompilerParams(
            dimension_semantics=("arbitrary",), vmem_limit_bytes=VMEM_LIMIT),
        name="dispatch",
    )(pad_start, pad_len, nused, dest_flat, x1)


def _ffn_kernel(blk_e_ref, nused_ref, xs_ref, wup_ref, bup_ref, wdn_ref, bdn_ref, y4_hbm,
                ybuf, act_ref, ids_v, ids_s, ssem, isem):
    tm = FFN_TM
    d = wdn_ref.shape[2]
    dff = wdn_ref.shape[1]
    sub = d // LANES
    n_assign = y4_hbm.shape[0] // sub - tm
    i = pl.program_id(0)
    nused = nused_ref[0]
    slot = lax.rem(i, 2)
    prev = 1 - slot
    n_chunks = (dff + d) // FFN_CHUNK
    rows_per_chunk = tm // n_chunks

    def ids_copy(s):
        return pltpu.make_async_copy(ids_v.at[s], ids_s.at[s], isem.at[s])

    def ybuf_rows(s, r, nrows):
        return ybuf.at[pl.ds(pl.multiple_of((s * tm + r) * sub, sub), nrows * sub)]

    def wait_scatter(s):
        pltpu.make_async_copy(ybuf_rows(s, 0, tm), y4_hbm.at[pl.ds(0, tm * sub)], ssem.at[s]).wait()

    def scatter_rows(s, lo, hi):
        for r in range(lo, hi):
            a = ids_s[s, r // LANES, r % LANES]
            tgt = jnp.where(a < n_assign, a, n_assign + r)
            pltpu.make_async_copy(ybuf_rows(s, r, 1),
                                  y4_hbm.at[pl.ds(pl.multiple_of(tgt * sub, sub), sub)],
                                  ssem.at[s]).start(priority=r % 2)

    def compute(s, between):
        ri = lax.broadcasted_iota(jnp.int32, (LANES, LANES), 0)
        ci = lax.broadcasted_iota(jnp.int32, (LANES, LANES), 1)
        dense = [jnp.sum(jnp.where(ri == ci, xs_ref[g * LANES:(g + 1) * LANES, d:], 0.0),
                         axis=0, keepdims=True) for g in range(tm // LANES)]
        ids_v[s] = jnp.concatenate(dense, axis=0).astype(jnp.int32)
        ids_copy(s).start()

        xb = xs_ref[:, :d].astype(BF16)
        for c in range(dff // FFN_CHUNK):
            between(c)
            lo = c * FFN_CHUNK
            hg = (jnp.dot(xb, wup_ref[0, :, lo:lo + FFN_CHUNK], preferred_element_type=F32)
                  + bup_ref[0, :, lo:lo + FFN_CHUNK])
            hl = (jnp.dot(xb, wup_ref[0, :, dff + lo:dff + lo + FFN_CHUNK],
                          preferred_element_type=F32) + bup_ref[0, :, dff + lo:dff + lo + FFN_CHUNK])
            gate = jnp.minimum(hg, SWIGLU_LIMIT)
            lin = jnp.clip(hl, -SWIGLU_LIMIT, SWIGLU_LIMIT)
            act = (lin + 1.0) * gate * jax.nn.sigmoid(SWIGLU_ALPHA * gate)
            act_ref[:, lo:lo + FFN_CHUNK] = act.astype(BF16)
        for c in range(d // FFN_CHUNK):
            between(dff // FFN_CHUNK + c)
            lo = c * FFN_CHUNK
            y = (jnp.dot(act_ref[...], wdn_ref[0, :, lo:lo + FFN_CHUNK], preferred_element_type=F32)
                 + bdn_ref[0, :, lo:lo + FFN_CHUNK])
            for j in range(FFN_CHUNK // LANES):
                cb = lo // LANES + j
                ybuf[pl.ds(s * (tm * sub) + cb, tm, stride=sub), :] = y[:, j * LANES:(j + 1) * LANES]

    def scatter_chunk(c):
        scatter_rows(prev, c * rows_per_chunk, (c + 1) * rows_per_chunk)

    @pl.when(i == 0)
    def _():
        ybuf[tm * sub:, :] = jnp.zeros((tm * sub, LANES), F32)
        fill = pltpu.make_async_copy(ybuf_rows(1, 0, tm),
                                     y4_hbm.at[pl.ds(n_assign * sub, tm * sub)], ssem.at[1])
        fill.start()
        fill.wait()

    steady = jnp.logical_and(i >= 2, i < nused)

    @pl.when(steady)
    def _():
        wait_scatter(slot)
        ids_copy(prev).wait()
        compute(slot, scatter_chunk)

    @pl.when(jnp.logical_not(steady))
    def _():
        @pl.when(i < nused)
        def _():
            compute(slot, lambda c: None)

        @pl.when(jnp.logical_and(i >= 1, i <= nused))
        def _():
            ids_copy(prev).wait()
            scatter_rows(prev, 0, tm)

        @pl.when(i == nused)
        def _():
            @pl.when(nused >= 2)
            def _():
                wait_scatter(slot)
            wait_scatter(prev)


def _ffn(xs, blk_e, nused, wup, bup, wdn, bdn, n_assign):
    p_rows, dx = xs.shape
    tm = FFN_TM
    e, d, dff2 = wup.shape
    dff = dff2 // 2
    rowblk = lambda i, be, nu: (jnp.minimum(i, nu[0] - 1), 0)
    wsel = lambda i, be, nu: (be[jnp.minimum(i, nu[0] - 1)], 0, 0)
    return pl.pallas_call(
        _ffn_kernel,
        out_shape=jax.ShapeDtypeStruct(((n_assign + tm) * (d // LANES), LANES), F32),
        grid_spec=pltpu.PrefetchScalarGridSpec(
            num_scalar_prefetch=2,
            grid=(p_rows // tm + 1,),
            in_specs=[
                pl.BlockSpec((tm, dx), rowblk),
                pl.BlockSpec((1, d, dff2), wsel),
                pl.BlockSpec((1, 1, dff2), wsel),
                pl.BlockSpec((1, dff, d), wsel),
                pl.BlockSpec((1, 1, d), wsel),
            ],
            out_specs=pl.BlockSpec(memory_space=pl.ANY),
            scratch_shapes=[
                pltpu.VMEM((2 * tm * (d // LANES), LANES), F32),
                pltpu.VMEM((tm, dff), BF16),
                pltpu.VMEM((2, tm // LANES, LANES), jnp.int32),
                pltpu.SMEM((2, tm // LANES, LANES), jnp.int32),
                pltpu.SemaphoreType.DMA((2,)),
                pltpu.SemaphoreType.DMA((2,)),
            ],
        ),
        compiler_params=pltpu.CompilerParams(
            dimension_semantics=("arbitrary",), vmem_limit_bytes=VMEM_LIMIT),
        name="ffn",
    )(blk_e, nused, xs, wup, bup, wdn, bdn)


def _combine_kernel(y0_ref, y1_ref, y2_ref, y3_ref, gate_ref, x1_ref, g2_ref, b2_ref, o_ref):
    tm, d = x1_ref.shape
    sub = d // LANES
    gates = gate_ref[...]

    def rows(y_ref):
        return jnp.concatenate([y_ref[pl.ds(j, tm, stride=sub), :] for j in range(sub)], axis=1)

    ffn = gates[:, 0:1] * rows(y0_ref)
    for k, y_ref in ((1, y1_ref), (2, y2_ref), (3, y3_ref)):
        ffn = ffn + gates[:, k:k + 1] * rows(y_ref)
    o_ref[...] = _layer_norm(DEEPNORM_ALPHA * x1_ref[...] + ffn, g2_ref[...], b2_ref[...])


def _combine(y4, gates_tok, x1, g2, b2):
    n, d = x1.shape
    tm = COMBINE_TM
    nsteps = n // tm
    row = lambda i: (i, 0)
    const = lambda i: (0, 0)
    yk = [pl.BlockSpec((tm * (d // LANES), LANES),
                       functools.partial(lambda k, i: (k * nsteps + i, 0), k))
          for k in range(TOP_K)]
    return pl.pallas_call(
        _combine_kernel,
        out_shape=jax.ShapeDtypeStruct((n, d), F32),
        grid=(nsteps,),
        in_specs=yk + [
            pl.BlockSpec((tm, TOP_K), row),
            pl.BlockSpec((tm, d), row),
            pl.BlockSpec(g2.shape, const),
            pl.BlockSpec(b2.shape, const),
        ],
        out_specs=pl.BlockSpec((tm, d), row),
        compiler_params=pltpu.CompilerParams(
            dimension_semantics=("arbitrary",), vmem_limit_bytes=VMEM_LIMIT),
        name="combine",
    )(y4, y4, y4, y4, gates_tok, x1, g2, b2)


def _head_pairs(w_a, w_b):
    d = w_a.shape[0]
    a = w_a.reshape(d, DIFF_HEADS, DIFF_HEAD_DIM)
    b = w_b.reshape(d, DIFF_HEADS, DIFF_HEAD_DIM)
    return jnp.concatenate([a, b], axis=-1).reshape(d, DIFF_HEADS * V_HEAD_DIM)


def kernel(x, w_in, lambda_q1, lambda_k1, lambda_q2, lambda_k2, subln_g, gmlp_ln_g, gmlp_ln_b,
           w_spatial, b_spatial, w_o, ln1_g, ln1_b, w_router, b_router, w_up, b_up,
           w_down, b_down, ln2_g, ln2_b):
    b, s, d = x.shape
    n = b * s
    x2 = x.reshape(n, d)

    w = w_in[0]
    c = QK_WIDTH
    scale = DIFF_HEAD_DIM ** -0.5
    wq = _head_pairs(w[:, 0:c], w[:, c:2 * c]) * scale
    wk = _head_pairs(w[:, 2 * c:3 * c], w[:, 3 * c:4 * c])
    wqk = jnp.concatenate([wq, wk], axis=1).astype(BF16)
    wvt = w[:, 4 * c:4 * c + DIFF_WIDTH].T.astype(BF16)
    wg = w[:, 4 * c + DIFF_WIDTH:].astype(BF16)
    lamv = jnp.concatenate([lambda_q1, lambda_k1, lambda_q2, lambda_k2], axis=0)
    bsp = jnp.repeat(b_spatial[0].T, GMLP_GROUP_DIM, axis=1)

    assert PROJ_TM == ATTN_T and s % ATTN_T == 0
    qq, kk, vt, u, vn = _proj(x2, wqk, wvt, wg, gmlp_ln_g, gmlp_ln_b)
    attn = _attention(qq.reshape(b, s, -1), kk.reshape(b, s, -1),
                      vt.reshape(b, s // ATTN_T, DIFF_WIDTH, ATTN_T),
                      lamv, subln_g.reshape(V_HEAD_DIM, 1)).reshape(n, DIFF_WIDTH)
    x1, eid, gates, rank, cnt = _mix(attn, u, vn, x2, w_spatial[0], bsp, w_o[0].astype(BF16),
                                     ln1_g, ln1_b, w_router[0].T, b_router[0][:, None])

    counts = cnt[:, 0].astype(jnp.int32)
    padded = ((counts + FFN_TM - 1) // FFN_TM) * FFN_TM
    end_padded = jnp.cumsum(padded)
    start_padded = end_padded - padded
    eids = jnp.arange(N_EXPERTS, dtype=jnp.int32)[:, None, None]
    start_of = jnp.sum(jnp.where(eid[None] == eids, start_padded[:, None, None], 0), axis=0)
    dest_flat = (start_of + rank).T.reshape(n * TOP_K)
    p_rows = n * TOP_K + N_EXPERTS * FFN_TM
    nblk = p_rows // FFN_TM
    nused = (end_padded[-1:] // FFN_TM).astype(jnp.int32)
    blk_start = jnp.arange(nblk, dtype=jnp.int32) * FFN_TM
    blk_e = jnp.minimum(jnp.sum((end_padded[None, :] <= blk_start[:, None]).astype(jnp.int32), axis=1),
                        N_EXPERTS - 1)

    xs = _dispatch(x1, dest_flat, start_padded + counts, padded - counts, nused, p_rows)
    y4 = _ffn(xs, blk_e, nused, w_up[0].astype(BF16), b_up[0][:, None, :],
              w_down[0].astype(BF16), b_down[0][:, None, :], n * TOP_K)
    out = _combine(y4, gates.T, x1, ln2_g, ln2_b)
    return out.reshape(b, s, d)
```

```python
import jax
import jax.numpy as jnp
from jax import lax
from jax.experimental import pallas as pl
from jax.experimental.pallas import tpu as pltpu

DIFF_HEADS = 4
DIFF_HEAD_DIM = 64
V_HEAD_DIM = 2 * DIFF_HEAD_DIM
QK_WIDTH = DIFF_HEADS * DIFF_HEAD_DIM
DIFF_WIDTH = DIFF_HEADS * V_HEAD_DIM
GMLP_GROUPS = 8
GMLP_GROUP_DIM = 64
GMLP_WIDTH = GMLP_GROUPS * GMLP_GROUP_DIM
CHUNK = 128
N_EXPERTS = 32
TOP_K = 4
SWIGLU_LIMIT = 7.0
SWIGLU_ALPHA = 1.702
LN_EPS = 1e-5
DEPTH = 1
DEEPNORM_ALPHA = (2.0 * DEPTH) ** 0.25
LAMBDA_INIT = 0.8 - 0.6 * 1.0

LANES = 128

PROJ_TM = 512
ATTN_T = 512
MOE_TM = 256
FFN_TM = 256
RUN_BITS = MOE_TM.bit_length()

VMEM_LIMIT = 48 * 1024 * 1024

BF16 = jnp.bfloat16
F32 = jnp.float32


def _layer_norm(y, g, b):
    mu = jnp.mean(y, axis=-1, keepdims=True)
    yc = y - mu
    var = jnp.mean(yc * yc, axis=-1, keepdims=True)
    return yc * lax.rsqrt(var + LN_EPS) * g + b


def _gelu(x):
    return 0.5 * x * (1.0 + lax.erf(x * (2.0 ** -0.5)))


def _nt_dot(a, b):
    return lax.dot_general(a, b, (((1,), (1,)), ((), ())), preferred_element_type=F32)


def _slabs_to_rows(ref, first, nrows, sub):
    return jnp.concatenate([ref[pl.ds(first + j, nrows, stride=sub), :] for j in range(sub)], axis=1)


def _rows_to_slabs(ref, first, rows, sub):
    for j in range(sub):
        ref[pl.ds(first + j, rows.shape[0], stride=sub), :] = rows[:, j * LANES:(j + 1) * LANES]


def _proj_kernel(x_ref, wqk_ref, wvt_ref, wg_ref, lng_ref, lnb_ref,
                 qq_ref, kk_ref, vt_ref, u_ref, vn_ref):
    xb = x_ref[...].astype(BF16)
    qk = jnp.dot(xb, wqk_ref[...], preferred_element_type=F32)
    qq_ref[...] = qk[:, :DIFF_WIDTH].astype(BF16)
    kk_ref[...] = qk[:, DIFF_WIDTH:].astype(BF16)
    vt_ref[0] = _nt_dot(wvt_ref[...], xb).astype(BF16)
    g = jnp.dot(xb, wg_ref[...], preferred_element_type=F32)
    u_ref[...] = _gelu(g[:, :GMLP_WIDTH]).astype(BF16)
    gv = _gelu(g[:, GMLP_WIDTH:])
    vn_ref[...] = _layer_norm(gv, lng_ref[...], lnb_ref[...]).astype(BF16)


def _proj(x2, wqk, wvt, wg, lng, lnb):
    n, d = x2.shape
    tm = PROJ_TM
    row = lambda i: (i, 0)
    const = lambda i: (0, 0)
    out = jax.ShapeDtypeStruct((n, DIFF_WIDTH), BF16)
    vt_out = jax.ShapeDtypeStruct((n // tm, DIFF_WIDTH, tm), BF16)
    rowspec = pl.BlockSpec((tm, DIFF_WIDTH), row)
    return pl.pallas_call(
        _proj_kernel,
        out_shape=(out, out, vt_out, out, out),
        grid=(n // tm,),
        in_specs=[
            pl.BlockSpec((tm, d), row),
            pl.BlockSpec(wqk.shape, const),
            pl.BlockSpec(wvt.shape, const),
            pl.BlockSpec(wg.shape, const),
            pl.BlockSpec(lng.shape, const),
            pl.BlockSpec(lnb.shape, const),
        ],
        out_specs=[rowspec, rowspec, pl.BlockSpec((1, DIFF_WIDTH, tm), lambda i: (i, 0, 0)),
                   rowspec, rowspec],
        compiler_params=pltpu.CompilerParams(
            dimension_semantics=("arbitrary",), vmem_limit_bytes=VMEM_LIMIT),
        name="proj",
    )(x2, wqk, wvt, wg, lng, lnb)


def _attn_kernel(lamv_ref, g_ref, qq_ref, kk_ref, vt_ref, o_ref):
    t = ATTN_T
    qi = pl.program_id(2)
    q = qq_ref[0]
    lane = lax.broadcasted_iota(jnp.int32, q.shape, 1)
    first = lane < DIFF_HEAD_DIM
    zero = jnp.zeros_like(q)
    q1 = jnp.where(first, q, zero)
    q2 = jnp.where(first, zero, q)

    def scores(j):
        k = kk_ref[0, pl.ds(pl.multiple_of(j * t, t), t), :]
        return _nt_dot(k, q1), _nt_dot(k, q2)

    def consume(j, s_pair, carry, masked):
        vt = vt_ref[0, j]
        new = []
        for s, (m, l, acc) in zip(s_pair, carry):
            if masked:
                key = lax.broadcasted_iota(jnp.int32, s.shape, 0)
                qry = lax.broadcasted_iota(jnp.int32, s.shape, 1)
                s = jnp.where(key <= qry, s, -jnp.inf)
            m_new = jnp.maximum(m, jnp.max(s, axis=0, keepdims=True))
            alpha = jnp.exp(m - m_new)
            p = jnp.exp(s - m_new)
            l_new = alpha * l + jnp.sum(p, axis=0, keepdims=True)
            acc_new = alpha * acc + jnp.dot(vt, p.astype(BF16), preferred_element_type=F32)
            new.append((m_new, l_new, acc_new))
        return tuple(new)

    def init():
        return (jnp.full((1, t), -jnp.inf, F32), jnp.zeros((1, t), F32),
                jnp.zeros((V_HEAD_DIM, t), F32))

    carry = lax.fori_loop(0, qi, lambda j, c: consume(j, scores(j), c, False), (init(), init()))
    (_, l1, acc1), (_, l2, acc2) = consume(qi, scores(qi), carry, True)

    lv = lamv_ref[...]
    lam = (jnp.exp(jnp.sum(lv[0:1] * lv[1:2], axis=-1, keepdims=True))
           - jnp.exp(jnp.sum(lv[2:3] * lv[3:4], axis=-1, keepdims=True)) + LAMBDA_INIT)
    o = acc1 / l1 - lam * (acc2 / l2)
    ms = jnp.mean(o * o, axis=0, keepdims=True)
    o = o * lax.rsqrt(ms + LN_EPS) * g_ref[...] * (1.0 - LAMBDA_INIT)
    o_ref[0] = o.T.astype(BF16)


def _attention(qq, kk, vt, lamv, subln_g_col):
    b, s, _ = qq.shape
    t = ATTN_T
    qspec = pl.BlockSpec((1, t, V_HEAD_DIM), lambda bi, h, qi: (bi, qi, h))
    kspec = pl.BlockSpec((1, s, V_HEAD_DIM), lambda bi, h, qi: (bi, 0, h))
    vtspec = pl.BlockSpec((1, s // t, V_HEAD_DIM, t), lambda bi, h, qi: (bi, 0, h, 0))
    const = lambda bi, h, qi: (0, 0)
    return pl.pallas_call(
        _attn_kernel,
        out_shape=jax.ShapeDtypeStruct((b, s, DIFF_WIDTH), BF16),
        grid=(b, DIFF_HEADS, s // t),
        in_specs=[pl.BlockSpec(lamv.shape, const), pl.BlockSpec(subln_g_col.shape, const),
                  qspec, kspec, vtspec],
        out_specs=qspec,
        compiler_params=pltpu.CompilerParams(
            dimension_semantics=("arbitrary",) * 3, vmem_limit_bytes=VMEM_LIMIT),
        name="attn",
    )(lamv, subln_g_col, qq, kk, vt)


def _mix_kernel(attn_ref, u_ref, vn_ref, x_ref, wsp_ref, bsp_ref, wo_ref, g1_ref, b1_ref,
                wrt_ref, br_ref,
                x1_ref, eid_ref, gate_ref, rank_ref, cnt_ref, tcnt_ref,
                cat_ref, carry_ref):
    tm = MOE_TM

    @pl.when(pl.program_id(0) == 0)
    def _():
        carry_ref[...] = jnp.zeros_like(carry_ref)

    ri = lax.broadcasted_iota(jnp.int32, (CHUNK, CHUNK), 0)
    ci = lax.broadcasted_iota(jnp.int32, (CHUNK, CHUNK), 1)
    tril = ci <= ri
    first = ci < GMLP_GROUP_DIM
    w = [jnp.where(tril, wsp_ref[g], 0.0).astype(BF16) for g in range(GMLP_GROUPS)]
    cat_ref[:, :DIFF_WIDTH] = attn_ref[...]
    for c in range(tm // CHUNK):
        rows = slice(c * CHUNK, (c + 1) * CHUNK)
        for jb in range(GMLP_WIDTH // LANES):
            cols = slice(jb * LANES, (jb + 1) * LANES)
            vb = vn_ref[rows, cols]
            zero = jnp.zeros_like(vb)
            z = (jnp.dot(w[2 * jb], jnp.where(first, vb, zero), preferred_element_type=F32)
                 + jnp.dot(w[2 * jb + 1], jnp.where(first, zero, vb), preferred_element_type=F32))
            gated = u_ref[rows, cols].astype(F32) * (z + bsp_ref[:, cols])
            cat_ref[rows, DIFF_WIDTH + jb * LANES:DIFF_WIDTH + (jb + 1) * LANES] = gated.astype(BF16)

    mixed = jnp.dot(cat_ref[...], wo_ref[...], preferred_element_type=F32)
    x1 = _layer_norm(DEEPNORM_ALPHA * x_ref[...] + mixed, g1_ref[...], b1_ref[...])
    x1_ref[...] = x1

    xh = x1.astype(BF16)
    xl = (x1 - xh.astype(F32)).astype(BF16)
    wr = wrt_ref[...]
    wh = wr.astype(BF16)
    wl = (wr - wh.astype(F32)).astype(BF16)
    logits = _nt_dot(wh, xh) + _nt_dot(wl, xh) + _nt_dot(wh, xl) + br_ref[...]

    eio = lax.broadcasted_iota(jnp.int32, logits.shape, 0).astype(F32)
    vals, idxs, sels = [], [], []
    cur = logits
    for _ in range(TOP_K):
        mx = jnp.max(cur, axis=0, keepdims=True)
        idx = jnp.min(jnp.where(cur == mx, eio, float(N_EXPERTS)), axis=0, keepdims=True)
        sel = eio == idx
        vals.append(mx)
        idxs.append(idx)
        sels.append(sel)
        cur = jnp.where(sel, -jnp.inf, cur)
    ex = [jnp.exp(vk - vals[0]) for vk in vals]
    denom = ex[0] + ex[1] + ex[2] + ex[3]
    gate_ref[...] = jnp.concatenate([e / denom for e in ex], axis=0)
    eid_ref[...] = jnp.concatenate(idxs, axis=0).astype(jnp.int32)

    chosen = (sels[0] | sels[1] | sels[2] | sels[3])
    onehot = jnp.where(chosen, 1.0, 0.0)
    ti = lax.broadcasted_iota(jnp.int32, (tm, tm), 0)
    tj = lax.broadcasted_iota(jnp.int32, (tm, tm), 1)
    before = jnp.where(ti < tj, 1.0, 0.0).astype(BF16)
    carry = carry_ref[:, 0:1]
    cnt_before = jnp.dot(onehot.astype(BF16), before, preferred_element_type=F32) + carry
    ranks = [jnp.sum(jnp.where(s, cnt_before, 0.0), axis=0, keepdims=True) for s in sels]
    rank_ref[...] = jnp.concatenate(ranks, axis=0).astype(jnp.int32)
    total = carry + jnp.sum(onehot, axis=1, keepdims=True)
    carry_ref[...] = jnp.broadcast_to(total, carry_ref.shape)
    cnt_ref[...] = jnp.broadcast_to(total, cnt_ref.shape)
    tcnt_ref[...] = jnp.broadcast_to(total - carry, tcnt_ref.shape)


def _mix(attn, u, vn, x2, wsp, bsp, wo, g1, b1, wrt, br):
    n, d = x2.shape
    tm = MOE_TM
    row = lambda i: (i, 0)
    col = lambda i: (0, i)
    const2 = lambda i: (0, 0)
    const3 = lambda i: (0, 0, 0)
    tok = lambda dt: jax.ShapeDtypeStruct((TOP_K, n), dt)
    return pl.pallas_call(
        _mix_kernel,
        out_shape=(jax.ShapeDtypeStruct((n, d), F32), tok(jnp.int32), tok(F32), tok(jnp.int32),
                   jax.ShapeDtypeStruct((N_EXPERTS, LANES), F32),
                   jax.ShapeDtypeStruct((n // tm * N_EXPERTS, LANES), F32)),
        grid=(n // tm,),
        in_specs=[
            pl.BlockSpec((tm, DIFF_WIDTH), row),
            pl.BlockSpec((tm, GMLP_WIDTH), row),
            pl.BlockSpec((tm, GMLP_WIDTH), row),
            pl.BlockSpec((tm, d), row),
            pl.BlockSpec(wsp.shape, const3),
            pl.BlockSpec(bsp.shape, const2),
            pl.BlockSpec(wo.shape, const2),
            pl.BlockSpec(g1.shape, const2),
            pl.BlockSpec(b1.shape, const2),
            pl.BlockSpec(wrt.shape, const2),
            pl.BlockSpec(br.shape, const2),
        ],
        out_specs=[
            pl.BlockSpec((tm, d), row),
            pl.BlockSpec((TOP_K, tm), col),
            pl.BlockSpec((TOP_K, tm), col),
            pl.BlockSpec((TOP_K, tm), col),
            pl.BlockSpec((N_EXPERTS, LANES), const2),
            pl.BlockSpec((N_EXPERTS, LANES), row),
        ],
        scratch_shapes=[pltpu.VMEM((tm, DIFF_WIDTH + GMLP_WIDTH), BF16),
                        pltpu.VMEM((N_EXPERTS, LANES), F32)],
        compiler_params=pltpu.CompilerParams(
            dimension_semantics=("arbitrary",), vmem_limit_bytes=VMEM_LIMIT),
        name="mix",
    )(attn, u, vn, x2, wsp, bsp, wo, g1, b1, wrt, br)


def _for_each_run(tile, tcnt_ref, off_ref, dst_ref, make_copy):
    def per_expert(e, carry):
        n = tcnt_ref[tile * N_EXPERTS + e]
        off = off_ref[tile * N_EXPERTS + e]
        dst = dst_ref[tile * N_EXPERTS + e]
        for bit in range(RUN_BITS):
            done = (n >> (bit + 1)) << (bit + 1)

            @pl.when(((n >> bit) & 1) == 1)
            def _():
                make_copy(off + done, dst + done, 1 << bit).start()
        return carry

    lax.fori_loop(0, N_EXPERTS, per_expert, 0)


def _dispatch_kernel(tcnt_ref, off_ref, dst_ref, pad_start_ref, pad_len_ref, nused_ref,
                     pos_ref, x1_ref, xs_hbm, sorted_ref, zero_ref, sem, zsem):
    tm = MOE_TM
    rows = TOP_K * tm
    d = x1_ref.shape[1]
    sub = d // LANES
    nblk = xs_hbm.shape[0] // (FFN_TM * sub)
    i = pl.program_id(0)
    last = pl.num_programs(0) - 1
    slot = lax.rem(i, 2)

    def wait_runs(s):
        pltpu.make_async_copy(sorted_ref.at[pl.ds(s * rows * sub, rows * sub)],
                              xs_hbm.at[pl.ds(0, rows * sub)], sem.at[s]).wait()

    @pl.when(i >= 2)
    def _():
        wait_runs(slot)

    pos = pos_ref[...]
    ri = lax.broadcasted_iota(jnp.int32, (rows, tm), 0)
    hit = (ri == pos[0:1]) | (ri == pos[1:2]) | (ri == pos[2:3]) | (ri == pos[3:4])
    perm = jnp.where(hit, 1.0, 0.0).astype(BF16)
    srt = jnp.dot(perm, x1_ref[...].astype(BF16), preferred_element_type=F32)
    base = pl.multiple_of(slot * (rows * sub), rows * sub)
    _rows_to_slabs(sorted_ref, base, srt, sub)

    def run_copy(local_row, global_row, nrows):
        return pltpu.make_async_copy(
            sorted_ref.at[pl.ds(pl.multiple_of((slot * rows + local_row) * sub, sub), nrows * sub)],
            xs_hbm.at[pl.ds(pl.multiple_of(global_row * sub, sub), nrows * sub)], sem.at[slot])

    _for_each_run(i, tcnt_ref, off_ref, dst_ref, run_copy)

    @pl.when(i == 0)
    def _():
        zero_ref[...] = jnp.zeros_like(zero_ref)

        def zero_copy(r):
            return pltpu.make_async_copy(zero_ref.at[pl.ds(0, sub)],
                                         xs_hbm.at[pl.ds(pl.multiple_of(r * sub, sub), sub)], zsem)

        def tail_copy(blk):
            return pltpu.make_async_copy(
                zero_ref, xs_hbm.at[pl.ds(pl.multiple_of(blk * (FFN_TM * sub), FFN_TM * sub),
                                          FFN_TM * sub)], zsem)

        def tail_start(blk, c):
            tail_copy(blk).start()
            return c

        def tail_wait(blk, c):
            tail_copy(blk).wait()
            return c

        lax.fori_loop(nused_ref[0], nblk, tail_start, 0)
        lax.fori_loop(nused_ref[0], nblk, tail_wait, 0)

        def per_expert(e, _):
            s = pad_start_ref[e]
            cnt = pad_len_ref[e]

            def start(r, c):
                zero_copy(s + r).start()
                return c

            def wait(r, c):
                zero_copy(s + r).wait()
                return c

            lax.fori_loop(0, cnt, start, 0)
            lax.fori_loop(0, cnt, wait, 0)
            return 0

        lax.fori_loop(0, N_EXPERTS, per_expert, 0)

    @pl.when(i == last)
    def _():
        wait_runs(slot)

        @pl.when(last >= 1)
        def _():
            wait_runs(1 - slot)


def _dispatch(x1, pos, tcnt, off, dst, pad_start, pad_len, nused, p_rows):
    n, d = x1.shape
    tm = MOE_TM
    sub = d // LANES
    return pl.pallas_call(
        _dispatch_kernel,
        out_shape=jax.ShapeDtypeStruct((p_rows * sub, LANES), F32),
        grid_spec=pltpu.PrefetchScalarGridSpec(
            num_scalar_prefetch=6,
            grid=(n // tm,),
            in_specs=[
                pl.BlockSpec((TOP_K, tm), lambda i, *_: (0, i)),
                pl.BlockSpec((tm, d), lambda i, *_: (i, 0)),
            ],
            out_specs=pl.BlockSpec(memory_space=pl.ANY),
            scratch_shapes=[pltpu.VMEM((2 * TOP_K * tm * sub, LANES), F32),
                            pltpu.VMEM((FFN_TM * sub, LANES), F32),
                            pltpu.SemaphoreType.DMA((2,)), pltpu.SemaphoreType.DMA(())],
        ),
        compiler_params=pltpu.CompilerParams(
            dimension_semantics=("arbitrary",), vmem_limit_bytes=VMEM_LIMIT),
        name="dispatch",
    )(tcnt, off, dst, pad_start, pad_len, nused, pos, x1)


def _ffn_kernel(blk_e_ref, nused_ref, xs_ref, wup_ref, bup_ref, wdn_ref, bdn_ref, y_ref):
    tm = FFN_TM
    dff = wdn_ref.shape[1]
    sub = wdn_ref.shape[2] // LANES
    used = pl.program_id(0) < nused_ref[0]

    @pl.when(jnp.logical_not(used))
    def _():
        y_ref[...] = jnp.zeros_like(y_ref)

    @pl.when(used)
    def _():
        xb = _slabs_to_rows(xs_ref, 0, tm, sub).astype(BF16)
        h = jnp.dot(xb, wup_ref[0], preferred_element_type=F32) + bup_ref[0]
        gate = jnp.minimum(h[:, :dff], SWIGLU_LIMIT)
        lin = jnp.clip(h[:, dff:], -SWIGLU_LIMIT, SWIGLU_LIMIT)
        act = (lin + 1.0) * gate * jax.nn.sigmoid(SWIGLU_ALPHA * gate)
        y = jnp.dot(act.astype(BF16), wdn_ref[0], preferred_element_type=F32) + bdn_ref[0]
        _rows_to_slabs(y_ref, 0, y, sub)


def _ffn(xs, blk_e, nused, wup, bup, wdn, bdn):
    tm = FFN_TM
    e, d, dff2 = wup.shape
    dff = dff2 // 2
    sub = d // LANES
    rowblk = lambda i, be, nu: (i, 0)
    wsel = lambda i, be, nu: (be[i], 0, 0)
    return pl.pallas_call(
        _ffn_kernel,
        out_shape=jax.ShapeDtypeStruct(xs.shape, F32),
        grid_spec=pltpu.PrefetchScalarGridSpec(
            num_scalar_prefetch=2,
            grid=(xs.shape[0] // (tm * sub),),
            in_specs=[
                pl.BlockSpec((tm * sub, LANES), rowblk),
                pl.BlockSpec((1, d, dff2), wsel),
                pl.BlockSpec((1, 1, dff2), wsel),
                pl.BlockSpec((1, dff, d), wsel),
                pl.BlockSpec((1, 1, d), wsel),
            ],
            out_specs=pl.BlockSpec((tm * sub, LANES), rowblk),
        ),
        compiler_params=pltpu.CompilerParams(
            dimension_semantics=("arbitrary",), vmem_limit_bytes=VMEM_LIMIT),
        name="ffn",
    )(blk_e, nused, xs, wup, bup, wdn, bdn)


def _combine_kernel(tcnt_ref, off_ref, dst_ref, pos_ref, gate_ref, x1_ref, g2_ref, b2_ref, y_hbm,
                    o_ref, stage_ref, sem):
    tm = MOE_TM
    rows = TOP_K * tm
    sub = x1_ref.shape[1] // LANES
    i = pl.program_id(0)
    nsteps = pl.num_programs(0)
    slot = lax.rem(i, 2)

    def fetch_runs(tile, s):
        def run_copy(local_row, global_row, nrows):
            return pltpu.make_async_copy(
                y_hbm.at[pl.ds(pl.multiple_of(global_row * sub, sub), nrows * sub)],
                stage_ref.at[pl.ds(pl.multiple_of((s * rows + local_row) * sub, sub), nrows * sub)],
                sem.at[s])
        _for_each_run(tile, tcnt_ref, off_ref, dst_ref, run_copy)

    @pl.when(i == 0)
    def _():
        fetch_runs(0, 0)

    @pl.when(i + 1 < nsteps)
    def _():
        fetch_runs(i + 1, 1 - slot)

    pltpu.make_async_copy(y_hbm.at[pl.ds(0, rows * sub)],
                          stage_ref.at[pl.ds(slot * rows * sub, rows * sub)], sem.at[slot]).wait()

    ys = _slabs_to_rows(stage_ref, pl.multiple_of(slot * (rows * sub), rows * sub), rows, sub)
    yh = ys.astype(BF16)
    yl = (ys - yh.astype(F32)).astype(BF16)
    pos = pos_ref[...]
    gates = gate_ref[...]
    ci = lax.broadcasted_iota(jnp.int32, (tm, rows), 1)
    w = jnp.where(ci == pos[:, 0:1], gates[:, 0:1], 0.0)
    for k in range(1, TOP_K):
        w = w + jnp.where(ci == pos[:, k:k + 1], gates[:, k:k + 1], 0.0)
    wh = w.astype(BF16)
    wl = (w - wh.astype(F32)).astype(BF16)
    ffn = (jnp.dot(wh, yh, preferred_element_type=F32) + jnp.dot(wl, yh, preferred_element_type=F32)
           + jnp.dot(wh, yl, preferred_element_type=F32))
    o_ref[...] = _layer_norm(DEEPNORM_ALPHA * x1_ref[...] + ffn, g2_ref[...], b2_ref[...])


def _combine(y, pos_tok, gates_tok, tcnt, off, dst, x1, g2, b2):
    n, d = x1.shape
    tm = MOE_TM
    sub = d // LANES
    row = lambda i, *_: (i, 0)
    const = lambda i, *_: (0, 0)
    return pl.pallas_call(
        _combine_kernel,
        out_shape=jax.ShapeDtypeStruct((n, d), F32),
        grid_spec=pltpu.PrefetchScalarGridSpec(
            num_scalar_prefetch=3,
            grid=(n // tm,),
            in_specs=[
                pl.BlockSpec((tm, TOP_K), row),
                pl.BlockSpec((tm, TOP_K), row),
                pl.BlockSpec((tm, d), row),
                pl.BlockSpec(g2.shape, const),
                pl.BlockSpec(b2.shape, const),
                pl.BlockSpec(memory_space=pl.ANY),
            ],
            out_specs=pl.BlockSpec((tm, d), row),
            scratch_shapes=[pltpu.VMEM((2 * TOP_K * tm * sub, LANES), F32),
                            pltpu.SemaphoreType.DMA((2,))],
        ),
        compiler_params=pltpu.CompilerParams(
            dimension_semantics=("arbitrary",), vmem_limit_bytes=VMEM_LIMIT),
        name="combine",
    )(tcnt, off, dst, pos_tok, gates_tok, x1, g2, b2, y)


def _head_pairs(w_a, w_b):
    d = w_a.shape[0]
    a = w_a.reshape(d, DIFF_HEADS, DIFF_HEAD_DIM)
    b = w_b.reshape(d, DIFF_HEADS, DIFF_HEAD_DIM)
    return jnp.concatenate([a, b], axis=-1).reshape(d, DIFF_HEADS * V_HEAD_DIM)


def _lookup(table, eid):
    eids = jnp.arange(N_EXPERTS, dtype=jnp.int32)[:, None, None]
    return jnp.sum(jnp.where(eid[None] == eids, table[:, None, :], 0), axis=0)


def kernel(x, w_in, lambda_q1, lambda_k1, lambda_q2, lambda_k2, subln_g, gmlp_ln_g, gmlp_ln_b,
           w_spatial, b_spatial, w_o, ln1_g, ln1_b, w_router, b_router, w_up, b_up,
           w_down, b_down, ln2_g, ln2_b):
    b, s, d = x.shape
    n = b * s
    x2 = x.reshape(n, d)

    w = w_in[0]
    c = QK_WIDTH
    scale = DIFF_HEAD_DIM ** -0.5
    wq = _head_pairs(w[:, 0:c], w[:, c:2 * c]) * scale
    wk = _head_pairs(w[:, 2 * c:3 * c], w[:, 3 * c:4 * c])
    wqk = jnp.concatenate([wq, wk], axis=1).astype(BF16)
    wvt = w[:, 4 * c:4 * c + DIFF_WIDTH].T.astype(BF16)
    wg = w[:, 4 * c + DIFF_WIDTH:].astype(BF16)
    lamv = jnp.concatenate([lambda_q1, lambda_k1, lambda_q2, lambda_k2], axis=0)
    bsp = jnp.repeat(b_spatial[0].T, GMLP_GROUP_DIM, axis=1)

    assert PROJ_TM == ATTN_T and s % ATTN_T == 0 and n % MOE_TM == 0
    qq, kk, vt, u, vn = _proj(x2, wqk, wvt, wg, gmlp_ln_g, gmlp_ln_b)
    attn = _attention(qq.reshape(b, s, -1), kk.reshape(b, s, -1),
                      vt.reshape(b, s // ATTN_T, DIFF_WIDTH, ATTN_T),
                      lamv, subln_g.reshape(V_HEAD_DIM, 1)).reshape(n, DIFF_WIDTH)
    x1, eid, gates, rank, cnt, tcnt_l = _mix(attn, u, vn, x2, w_spatial[0], bsp,
                                             w_o[0].astype(BF16), ln1_g, ln1_b,
                                             w_router[0].T, b_router[0][:, None])

    ntiles = n // MOE_TM
    counts = cnt[:, 0].astype(jnp.int32)
    padded = ((counts + FFN_TM - 1) // FFN_TM) * FFN_TM
    end_padded = jnp.cumsum(padded)
    start_padded = end_padded - padded
    tcnt = tcnt_l.reshape(ntiles, N_EXPERTS, LANES)[:, :, 0].astype(jnp.int32)
    before = jnp.cumsum(tcnt, axis=0) - tcnt
    off = jnp.cumsum(tcnt, axis=1) - tcnt
    dst = start_padded[None, :] + before
    tile_base = jnp.repeat((off - before).T, MOE_TM, axis=1)
    pos = _lookup(tile_base, eid) + rank

    p_rows = n * TOP_K + N_EXPERTS * FFN_TM
    nblk = p_rows // FFN_TM
    nused = (end_padded[-1:] // FFN_TM).astype(jnp.int32)
    blk_start = jnp.arange(nblk, dtype=jnp.int32) * FFN_TM
    blk_e = jnp.minimum(jnp.sum((end_padded[None, :] <= blk_start[:, None]).astype(jnp.int32), axis=1),
                        N_EXPERTS - 1)
    tcnt_f, off_f, dst_f = tcnt.reshape(-1), off.reshape(-1), dst.reshape(-1)

    xs = _dispatch(x1, pos, tcnt_f, off_f, dst_f, start_padded + counts, padded - counts, nused,
                   p_rows)
    y = _ffn(xs, blk_e, nused, w_up[0].astype(BF16), b_up[0][:, None, :],
             w_down[0].astype(BF16), b_down[0][:, None, :])
    out = _combine(y, pos.T, gates.T, tcnt_f, off_f, dst_f, x1, ln2_g, ln2_b)
    return out.reshape(b, s, d)
```

```python
import jax
import jax.numpy as jnp
from jax import lax
from jax.experimental import pallas as pl
from jax.experimental.pallas import tpu as pltpu

DIFF_HEADS = 4
DIFF_HEAD_DIM = 64
V_HEAD_DIM = 2 * DIFF_HEAD_DIM
QK_WIDTH = DIFF_HEADS * DIFF_HEAD_DIM
DIFF_WIDTH = DIFF_HEADS * V_HEAD_DIM
GMLP_GROUPS = 8
GMLP_GROUP_DIM = 64
GMLP_WIDTH = GMLP_GROUPS * GMLP_GROUP_DIM
CHUNK = 128
N_EXPERTS = 32
TOP_K = 4
SWIGLU_LIMIT = 7.0
SWIGLU_ALPHA = 1.702
LN_EPS = 1e-5
DEPTH = 1
DEEPNORM_ALPHA = (2.0 * DEPTH) ** 0.25
LAMBDA_INIT = 0.8 - 0.6 * 1.0

LANES = 128

PROJ_TM = 512
ATTN_T = 512
MOE_TM = 256
MIX_SUB = 2
FFN_TM = 512
WEIGHT_CAST_COLS = 256
RUN_BITS = MOE_TM.bit_length()

VMEM_LIMIT = 48 * 1024 * 1024
FFN_VMEM_LIMIT = 58 * 1024 * 1024

BF16 = jnp.bfloat16
F32 = jnp.float32


def _layer_norm(y, g, b):
    mu = jnp.mean(y, axis=-1, keepdims=True)
    yc = y - mu
    var = jnp.mean(yc * yc, axis=-1, keepdims=True)
    return yc * lax.rsqrt(var + LN_EPS) * g + b


def _gelu(x):
    return 0.5 * x * (1.0 + lax.erf(x * (2.0 ** -0.5)))


def _nt_dot(a, b):
    return lax.dot_general(a, b, (((1,), (1,)), ((), ())), preferred_element_type=F32)


def _slabs_to_rows(ref, first, nrows, sub):
    return jnp.concatenate([ref[pl.ds(first + j, nrows, stride=sub), :] for j in range(sub)], axis=1)


def _rows_to_slabs(ref, first, rows, sub):
    for j in range(sub):
        ref[pl.ds(first + j, rows.shape[0], stride=sub), :] = rows[:, j * LANES:(j + 1) * LANES]


def _proj_kernel(x_ref, wqk_ref, wvt_ref, wg_ref, lng_ref, lnb_ref,
                 qq_ref, kk_ref, vt_ref, u_ref, vn_ref):
    xb = x_ref[...].astype(BF16)
    qk = jnp.dot(xb, wqk_ref[...], preferred_element_type=F32)
    qq_ref[...] = qk[:, :DIFF_WIDTH].astype(BF16)
    kk_ref[...] = qk[:, DIFF_WIDTH:].astype(BF16)
    vt_ref[0] = _nt_dot(wvt_ref[...], xb).astype(BF16)
    g = jnp.dot(xb, wg_ref[...], preferred_element_type=F32)
    u_ref[...] = _gelu(g[:, :GMLP_WIDTH]).astype(BF16)
    gv = _gelu(g[:, GMLP_WIDTH:])
    vn_ref[...] = _layer_norm(gv, lng_ref[...], lnb_ref[...]).astype(BF16)


def _proj(x2, wqk, wvt, wg, lng, lnb):
    n, d = x2.shape
    tm = PROJ_TM
    row = lambda i: (i, 0)
    const = lambda i: (0, 0)
    out = jax.ShapeDtypeStruct((n, DIFF_WIDTH), BF16)
    vt_out = jax.ShapeDtypeStruct((n // tm, DIFF_WIDTH, tm), BF16)
    rowspec = pl.BlockSpec((tm, DIFF_WIDTH), row)
    return pl.pallas_call(
        _proj_kernel,
        out_shape=(out, out, vt_out, out, out),
        grid=(n // tm,),
        in_specs=[
            pl.BlockSpec((tm, d), row),
            pl.BlockSpec(wqk.shape, const),
            pl.BlockSpec(wvt.shape, const),
            pl.BlockSpec(wg.shape, const),
            pl.BlockSpec(lng.shape, const),
            pl.BlockSpec(lnb.shape, const),
        ],
        out_specs=[rowspec, rowspec, pl.BlockSpec((1, DIFF_WIDTH, tm), lambda i: (i, 0, 0)),
                   rowspec, rowspec],
        compiler_params=pltpu.CompilerParams(
            dimension_semantics=("arbitrary",), vmem_limit_bytes=VMEM_LIMIT),
        name="proj",
    )(x2, wqk, wvt, wg, lng, lnb)


def _attn_kernel(lamv_ref, g_ref, qq_ref, kk_ref, vt_ref, o_ref):
    t = ATTN_T
    qi = pl.program_id(2)
    q = qq_ref[0]
    lane = lax.broadcasted_iota(jnp.int32, q.shape, 1)
    first = lane < DIFF_HEAD_DIM
    zero = jnp.zeros_like(q)
    q1 = jnp.where(first, q, zero)
    q2 = jnp.where(first, zero, q)

    def scores(j):
        k = kk_ref[0, pl.ds(pl.multiple_of(j * t, t), t), :]
        return _nt_dot(k, q1), _nt_dot(k, q2)

    def consume(j, s_pair, carry, masked):
        vt = vt_ref[0, j]
        new = []
        for s, (m, l, acc) in zip(s_pair, carry):
            if masked:
                key = lax.broadcasted_iota(jnp.int32, s.shape, 0)
                qry = lax.broadcasted_iota(jnp.int32, s.shape, 1)
                s = jnp.where(key <= qry, s, -jnp.inf)
            m_new = jnp.maximum(m, jnp.max(s, axis=0, keepdims=True))
            alpha = jnp.exp(m - m_new)
            p = jnp.exp(s - m_new)
            l_new = alpha * l + jnp.sum(p, axis=0, keepdims=True)
            acc_new = alpha * acc + jnp.dot(vt, p.astype(BF16), preferred_element_type=F32)
            new.append((m_new, l_new, acc_new))
        return tuple(new)

    def init():
        return (jnp.full((1, t), -jnp.inf, F32), jnp.zeros((1, t), F32),
                jnp.zeros((V_HEAD_DIM, t), F32))

    carry = lax.fori_loop(0, qi, lambda j, c: consume(j, scores(j), c, False), (init(), init()))
    (_, l1, acc1), (_, l2, acc2) = consume(qi, scores(qi), carry, True)

    lv = lamv_ref[...]
    lam = (jnp.exp(jnp.sum(lv[0:1] * lv[1:2], axis=-1, keepdims=True))
           - jnp.exp(jnp.sum(lv[2:3] * lv[3:4], axis=-1, keepdims=True)) + LAMBDA_INIT)
    o = acc1 / l1 - lam * (acc2 / l2)
    ms = jnp.mean(o * o, axis=0, keepdims=True)
    o = o * lax.rsqrt(ms + LN_EPS) * g_ref[...] * (1.0 - LAMBDA_INIT)
    o_ref[0] = o.T.astype(BF16)


def _attention(qq, kk, vt, lamv, subln_g_col):
    b, s, _ = qq.shape
    t = ATTN_T
    qspec = pl.BlockSpec((1, t, V_HEAD_DIM), lambda bi, h, qi: (bi, qi, h))
    kspec = pl.BlockSpec((1, s, V_HEAD_DIM), lambda bi, h, qi: (bi, 0, h))
    vtspec = pl.BlockSpec((1, s // t, V_HEAD_DIM, t), lambda bi, h, qi: (bi, 0, h, 0))
    const = lambda bi, h, qi: (0, 0)
    return pl.pallas_call(
        _attn_kernel,
        out_shape=jax.ShapeDtypeStruct((b, s, DIFF_WIDTH), BF16),
        grid=(b, DIFF_HEADS, s // t),
        in_specs=[pl.BlockSpec(lamv.shape, const), pl.BlockSpec(subln_g_col.shape, const),
                  qspec, kspec, vtspec],
        out_specs=qspec,
        compiler_params=pltpu.CompilerParams(
            dimension_semantics=("arbitrary",) * 3, vmem_limit_bytes=VMEM_LIMIT),
        name="attn",
    )(lamv, subln_g_col, qq, kk, vt)


def _mix_kernel(attn_ref, u_ref, vn_ref, x_ref, wsp_ref, bsp_ref, wo_ref, g1_ref, b1_ref,
                wrt_ref, br_ref,
                x1_ref, eid_ref, gate_ref, rank_ref, cnt_ref, tcnt_ref,
                cat_ref, carry_ref):
    tm = MOE_TM

    @pl.when(pl.program_id(0) == 0)
    def _():
        carry_ref[...] = jnp.zeros_like(carry_ref)

    ri = lax.broadcasted_iota(jnp.int32, (CHUNK, CHUNK), 0)
    ci = lax.broadcasted_iota(jnp.int32, (CHUNK, CHUNK), 1)
    tril = ci <= ri
    first = ci < GMLP_GROUP_DIM
    w = [jnp.where(tril, wsp_ref[g], 0.0).astype(BF16) for g in range(GMLP_GROUPS)]
    wr = wrt_ref[...]
    wh = wr.astype(BF16)
    wl = (wr - wh.astype(F32)).astype(BF16)
    eio = lax.broadcasted_iota(jnp.int32, (N_EXPERTS, tm), 0).astype(F32)
    ti = lax.broadcasted_iota(jnp.int32, (tm, tm), 0)
    tj = lax.broadcasted_iota(jnp.int32, (tm, tm), 1)
    before = jnp.where(ti < tj, 1.0, 0.0).astype(BF16)

    carry = carry_ref[:, 0:1]
    for h in range(MIX_SUB):
        r0 = h * tm
        cat_ref[r0:r0 + tm, :DIFF_WIDTH] = attn_ref[r0:r0 + tm, :]
        for c in range(tm // CHUNK):
            rows = slice(r0 + c * CHUNK, r0 + (c + 1) * CHUNK)
            for jb in range(GMLP_WIDTH // LANES):
                cols = slice(jb * LANES, (jb + 1) * LANES)
                vb = vn_ref[rows, cols]
                zero = jnp.zeros_like(vb)
                z = (jnp.dot(w[2 * jb], jnp.where(first, vb, zero), preferred_element_type=F32)
                     + jnp.dot(w[2 * jb + 1], jnp.where(first, zero, vb), preferred_element_type=F32))
                gated = u_ref[rows, cols].astype(F32) * (z + bsp_ref[:, cols])
                cat_ref[rows, DIFF_WIDTH + jb * LANES:DIFF_WIDTH + (jb + 1) * LANES] = gated.astype(BF16)

        mixed = jnp.dot(cat_ref[r0:r0 + tm, :], wo_ref[...], preferred_element_type=F32)
        x1 = _layer_norm(DEEPNORM_ALPHA * x_ref[r0:r0 + tm, :] + mixed, g1_ref[...], b1_ref[...])
        x1_ref[r0:r0 + tm, :] = x1

        xh = x1.astype(BF16)
        xl = (x1 - xh.astype(F32)).astype(BF16)
        logits = _nt_dot(wh, xh) + _nt_dot(wl, xh) + _nt_dot(wh, xl) + br_ref[...]

        vals, idxs, sels = [], [], []
        cur = logits
        for _ in range(TOP_K):
            mx = jnp.max(cur, axis=0, keepdims=True)
            idx = jnp.min(jnp.where(cur == mx, eio, float(N_EXPERTS)), axis=0, keepdims=True)
            sel = eio == idx
            vals.append(mx)
            idxs.append(idx)
            sels.append(sel)
            cur = jnp.where(sel, -jnp.inf, cur)
        ex = [jnp.exp(vk - vals[0]) for vk in vals]
        denom = ex[0] + ex[1] + ex[2] + ex[3]
        gate_ref[:, r0:r0 + tm] = jnp.concatenate([e / denom for e in ex], axis=0)
        eid_ref[:, r0:r0 + tm] = jnp.concatenate(idxs, axis=0).astype(jnp.int32)

        chosen = (sels[0] | sels[1] | sels[2] | sels[3])
        onehot = jnp.where(chosen, 1.0, 0.0)
        cnt_before = jnp.dot(onehot.astype(BF16), before, preferred_element_type=F32) + carry
        ranks = [jnp.sum(jnp.where(s, cnt_before, 0.0), axis=0, keepdims=True) for s in sels]
        rank_ref[:, r0:r0 + tm] = jnp.concatenate(ranks, axis=0).astype(jnp.int32)
        tile_cnt = jnp.sum(onehot, axis=1, keepdims=True)
        tcnt_ref[h * N_EXPERTS:(h + 1) * N_EXPERTS, :] = jnp.broadcast_to(tile_cnt, (N_EXPERTS, LANES))
        carry = carry + tile_cnt

    carry_ref[...] = jnp.broadcast_to(carry, carry_ref.shape)
    cnt_ref[...] = jnp.broadcast_to(carry, cnt_ref.shape)


def _mix(attn, u, vn, x2, wsp, bsp, wo, g1, b1, wrt, br):
    n, d = x2.shape
    tm = MOE_TM * MIX_SUB
    row = lambda i: (i, 0)
    col = lambda i: (0, i)
    const2 = lambda i: (0, 0)
    const3 = lambda i: (0, 0, 0)
    tok = lambda dt: jax.ShapeDtypeStruct((TOP_K, n), dt)
    return pl.pallas_call(
        _mix_kernel,
        out_shape=(jax.ShapeDtypeStruct((n, d), F32), tok(jnp.int32), tok(F32), tok(jnp.int32),
                   jax.ShapeDtypeStruct((N_EXPERTS, LANES), F32),
                   jax.ShapeDtypeStruct((n // MOE_TM * N_EXPERTS, LANES), F32)),
        grid=(n // tm,),
        in_specs=[
            pl.BlockSpec((tm, DIFF_WIDTH), row),
            pl.BlockSpec((tm, GMLP_WIDTH), row),
            pl.BlockSpec((tm, GMLP_WIDTH), row),
            pl.BlockSpec((tm, d), row),
            pl.BlockSpec(wsp.shape, const3),
            pl.BlockSpec(bsp.shape, const2),
            pl.BlockSpec(wo.shape, const2),
            pl.BlockSpec(g1.shape, const2),
            pl.BlockSpec(b1.shape, const2),
            pl.BlockSpec(wrt.shape, const2),
            pl.BlockSpec(br.shape, const2),
        ],
        out_specs=[
            pl.BlockSpec((tm, d), row),
            pl.BlockSpec((TOP_K, tm), col),
            pl.BlockSpec((TOP_K, tm), col),
            pl.BlockSpec((TOP_K, tm), col),
            pl.BlockSpec((N_EXPERTS, LANES), const2),
            pl.BlockSpec((MIX_SUB * N_EXPERTS, LANES), row),
        ],
        scratch_shapes=[pltpu.VMEM((tm, DIFF_WIDTH + GMLP_WIDTH), BF16),
                        pltpu.VMEM((N_EXPERTS, LANES), F32)],
        compiler_params=pltpu.CompilerParams(
            dimension_semantics=("arbitrary",), vmem_limit_bytes=VMEM_LIMIT),
        name="mix",
    )(attn, u, vn, x2, wsp, bsp, wo, g1, b1, wrt, br)


def _for_each_run(tile, tcnt_ref, off_ref, dst_ref, make_copy):
    def per_expert(e, carry):
        n = tcnt_ref[tile * N_EXPERTS + e]
        off = off_ref[tile * N_EXPERTS + e]
        dst = dst_ref[tile * N_EXPERTS + e]
        for bit in range(RUN_BITS):
            done = (n >> (bit + 1)) << (bit + 1)

            @pl.when(((n >> bit) & 1) == 1)
            def _():
                make_copy(off + done, dst + done, 1 << bit).start()
        return carry

    lax.fori_loop(0, N_EXPERTS, per_expert, 0)


def _dispatch_kernel(tcnt_ref, off_ref, dst_ref, pad_start_ref, pad_len_ref, nused_ref,
                     pos_ref, x1_ref, xs_hbm, sorted_ref, zero_ref, sem, zsem):
    tm = MOE_TM
    rows = TOP_K * tm
    d = x1_ref.shape[1]
    sub = d // LANES
    nblk = xs_hbm.shape[0] // (FFN_TM * sub)
    i = pl.program_id(0)
    last = pl.num_programs(0) - 1
    slot = lax.rem(i, 2)

    def wait_runs(s):
        pltpu.make_async_copy(sorted_ref.at[pl.ds(s * rows * sub, rows * sub)],
                              xs_hbm.at[pl.ds(0, rows * sub)], sem.at[s]).wait()

    @pl.when(i >= 2)
    def _():
        wait_runs(slot)

    pos = pos_ref[...]
    ri = lax.broadcasted_iota(jnp.int32, (rows, tm), 0)
    hit = (ri == pos[0:1]) | (ri == pos[1:2]) | (ri == pos[2:3]) | (ri == pos[3:4])
    perm = jnp.where(hit, 1.0, 0.0).astype(BF16)
    srt = jnp.dot(perm, x1_ref[...].astype(BF16), preferred_element_type=F32)
    base = pl.multiple_of(slot * (rows * sub), rows * sub)
    _rows_to_slabs(sorted_ref, base, srt, sub)

    def run_copy(local_row, global_row, nrows):
        return pltpu.make_async_copy(
            sorted_ref.at[pl.ds(pl.multiple_of((slot * rows + local_row) * sub, sub), nrows * sub)],
            xs_hbm.at[pl.ds(pl.multiple_of(global_row * sub, sub), nrows * sub)], sem.at[slot])

    _for_each_run(i, tcnt_ref, off_ref, dst_ref, run_copy)

    @pl.when(i == 0)
    def _():
        zero_ref[...] = jnp.zeros_like(zero_ref)

        def zero_copy(r):
            return pltpu.make_async_copy(zero_ref.at[pl.ds(0, sub)],
                                         xs_hbm.at[pl.ds(pl.multiple_of(r * sub, sub), sub)], zsem)

        def tail_copy(blk):
            return pltpu.make_async_copy(
                zero_ref, xs_hbm.at[pl.ds(pl.multiple_of(blk * (FFN_TM * sub), FFN_TM * sub),
                                          FFN_TM * sub)], zsem)

        def tail_start(blk, c):
            tail_copy(blk).start()
            return c

        def tail_wait(blk, c):
            tail_copy(blk).wait()
            return c

        lax.fori_loop(nused_ref[0], nblk, tail_start, 0)
        lax.fori_loop(nused_ref[0], nblk, tail_wait, 0)

        def per_expert(e, _):
            s = pad_start_ref[e]
            cnt = pad_len_ref[e]

            def start(r, c):
                zero_copy(s + r).start()
                return c

            def wait(r, c):
                zero_copy(s + r).wait()
                return c

            lax.fori_loop(0, cnt, start, 0)
            lax.fori_loop(0, cnt, wait, 0)
            return 0

        lax.fori_loop(0, N_EXPERTS, per_expert, 0)

    @pl.when(i == last)
    def _():
        wait_runs(slot)

        @pl.when(last >= 1)
        def _():
            wait_runs(1 - slot)


def _dispatch(x1, pos, tcnt, off, dst, pad_start, pad_len, nused, p_rows):
    n, d = x1.shape
    tm = MOE_TM
    sub = d // LANES
    return pl.pallas_call(
        _dispatch_kernel,
        out_shape=jax.ShapeDtypeStruct((p_rows * sub, LANES), F32),
        grid_spec=pltpu.PrefetchScalarGridSpec(
            num_scalar_prefetch=6,
            grid=(n // tm,),
            in_specs=[
                pl.BlockSpec((TOP_K, tm), lambda i, *_: (0, i)),
                pl.BlockSpec((tm, d), lambda i, *_: (i, 0)),
            ],
            out_specs=pl.BlockSpec(memory_space=pl.ANY),
            scratch_shapes=[pltpu.VMEM((2 * TOP_K * tm * sub, LANES), F32),
                            pltpu.VMEM((FFN_TM * sub, LANES), F32),
                            pltpu.SemaphoreType.DMA((2,)), pltpu.SemaphoreType.DMA(())],
        ),
        compiler_params=pltpu.CompilerParams(
            dimension_semantics=("arbitrary",), vmem_limit_bytes=VMEM_LIMIT),
        name="dispatch",
    )(tcnt, off, dst, pad_start, pad_len, nused, pos, x1)


def _ffn_kernel(blk_e_ref, nused_ref, xs_ref, wup_ref, bup_ref, wdn_ref, bdn_ref, y_ref,
                wup_b, wdn_b):
    tm = FFN_TM
    dff = wdn_ref.shape[1]
    sub = wdn_ref.shape[2] // LANES
    i = pl.program_id(0)
    used = i < nused_ref[0]

    @pl.when(jnp.logical_not(used))
    def _():
        y_ref[...] = jnp.zeros_like(y_ref)

    new_expert = jnp.logical_or(i == 0, blk_e_ref[i] != blk_e_ref[jnp.maximum(i - 1, 0)])

    @pl.when(jnp.logical_and(used, new_expert))
    def _():
        for c in range(0, 2 * dff, WEIGHT_CAST_COLS):
            wup_b[:, c:c + WEIGHT_CAST_COLS] = wup_ref[0, :, c:c + WEIGHT_CAST_COLS].astype(BF16)
        for c in range(0, wdn_ref.shape[2], WEIGHT_CAST_COLS):
            wdn_b[:, c:c + WEIGHT_CAST_COLS] = wdn_ref[0, :, c:c + WEIGHT_CAST_COLS].astype(BF16)

    @pl.when(used)
    def _():
        xb = _slabs_to_rows(xs_ref, 0, tm, sub).astype(BF16)
        h = jnp.dot(xb, wup_b[...], preferred_element_type=F32) + bup_ref[0]
        gate = jnp.minimum(h[:, :dff], SWIGLU_LIMIT)
        lin = jnp.clip(h[:, dff:], -SWIGLU_LIMIT, SWIGLU_LIMIT)
        act = (lin + 1.0) * gate * jax.nn.sigmoid(SWIGLU_ALPHA * gate)
        y = jnp.dot(act.astype(BF16), wdn_b[...], preferred_element_type=F32) + bdn_ref[0]
        _rows_to_slabs(y_ref, 0, y, sub)


def _ffn(xs, blk_e, nused, wup, bup, wdn, bdn):
    tm = FFN_TM
    e, d, dff2 = wup.shape
    dff = dff2 // 2
    sub = d // LANES
    rowblk = lambda i, be, nu: (i, 0)
    wsel = lambda i, be, nu: (be[i], 0, 0)
    return pl.pallas_call(
        _ffn_kernel,
        out_shape=jax.ShapeDtypeStruct(xs.shape, F32),
        grid_spec=pltpu.PrefetchScalarGridSpec(
            num_scalar_prefetch=2,
            grid=(xs.shape[0] // (tm * sub),),
            in_specs=[
                pl.BlockSpec((tm * sub, LANES), rowblk),
                pl.BlockSpec((1, d, dff2), wsel),
                pl.BlockSpec((1, 1, dff2), wsel),
                pl.BlockSpec((1, dff, d), wsel),
                pl.BlockSpec((1, 1, d), wsel),
            ],
            out_specs=pl.BlockSpec((tm * sub, LANES), rowblk),
            scratch_shapes=[pltpu.VMEM((d, dff2), BF16), pltpu.VMEM((dff, d), BF16)],
        ),
        compiler_params=pltpu.CompilerParams(
            dimension_semantics=("arbitrary",), vmem_limit_bytes=FFN_VMEM_LIMIT),
        name="ffn",
    )(blk_e, nused, xs, wup, bup, wdn, bdn)


def _combine_kernel(tcnt_ref, off_ref, dst_ref, pos_ref, gate_ref, x1_ref, g2_ref, b2_ref, y_hbm,
                    o_ref, stage_ref, sem):
    tm = MOE_TM
    rows = TOP_K * tm
    sub = x1_ref.shape[1] // LANES
    i = pl.program_id(0)
    nsteps = pl.num_programs(0)
    slot = lax.rem(i, 2)

    def fetch_runs(tile, s):
        def run_copy(local_row, global_row, nrows):
            return pltpu.make_async_copy(
                y_hbm.at[pl.ds(pl.multiple_of(global_row * sub, sub), nrows * sub)],
                stage_ref.at[pl.ds(pl.multiple_of((s * rows + local_row) * sub, sub), nrows * sub)],
                sem.at[s])
        _for_each_run(tile, tcnt_ref, off_ref, dst_ref, run_copy)

    @pl.when(i == 0)
    def _():
        fetch_runs(0, 0)

    @pl.when(i + 1 < nsteps)
    def _():
        fetch_runs(i + 1, 1 - slot)

    pltpu.make_async_copy(y_hbm.at[pl.ds(0, rows * sub)],
                          stage_ref.at[pl.ds(slot * rows * sub, rows * sub)], sem.at[slot]).wait()

    ys = _slabs_to_rows(stage_ref, pl.multiple_of(slot * (rows * sub), rows * sub), rows, sub)
    yh = ys.astype(BF16)
    yl = (ys - yh.astype(F32)).astype(BF16)
    pos = pos_ref[...]
    gates = gate_ref[...]
    ci = lax.broadcasted_iota(jnp.int32, (tm, rows), 1)
    w = jnp.where(ci == pos[:, 0:1], gates[:, 0:1], 0.0)
    for k in range(1, TOP_K):
        w = w + jnp.where(ci == pos[:, k:k + 1], gates[:, k:k + 1], 0.0)
    wh = w.astype(BF16)
    wl = (w - wh.astype(F32)).astype(BF16)
    ffn = (jnp.dot(wh, yh, preferred_element_type=F32) + jnp.dot(wl, yh, preferred_element_type=F32)
           + jnp.dot(wh, yl, preferred_element_type=F32))
    o_ref[...] = _layer_norm(DEEPNORM_ALPHA * x1_ref[...] + ffn, g2_ref[...], b2_ref[...])


def _combine(y, pos_tok, gates_tok, tcnt, off, dst, x1, g2, b2):
    n, d = x1.shape
    tm = MOE_TM
    sub = d // LANES
    row = lambda i, *_: (i, 0)
    const = lambda i, *_: (0, 0)
    return pl.pallas_call(
        _combine_kernel,
        out_shape=jax.ShapeDtypeStruct((n, d), F32),
        grid_spec=pltpu.PrefetchScalarGridSpec(
            num_scalar_prefetch=3,
            grid=(n // tm,),
            in_specs=[
                pl.BlockSpec((tm, TOP_K), row),
                pl.BlockSpec((tm, TOP_K), row),
                pl.BlockSpec((tm, d), row),
                pl.BlockSpec(g2.shape, const),
                pl.BlockSpec(b2.shape, const),
                pl.BlockSpec(memory_space=pl.ANY),
            ],
            out_specs=pl.BlockSpec((tm, d), row),
            scratch_shapes=[pltpu.VMEM((2 * TOP_K * tm * sub, LANES), F32),
                            pltpu.SemaphoreType.DMA((2,))],
        ),
        compiler_params=pltpu.CompilerParams(
            dimension_semantics=("arbitrary",), vmem_limit_bytes=VMEM_LIMIT),
        name="combine",
    )(tcnt, off, dst, pos_tok, gates_tok, x1, g2, b2, y)


def _head_pairs(w_a, w_b):
    d = w_a.shape[0]
    a = w_a.reshape(d, DIFF_HEADS, DIFF_HEAD_DIM)
    b = w_b.reshape(d, DIFF_HEADS, DIFF_HEAD_DIM)
    return jnp.concatenate([a, b], axis=-1).reshape(d, DIFF_HEADS * V_HEAD_DIM)


def _lookup(table, eid):
    eids = jnp.arange(N_EXPERTS, dtype=jnp.int32)[:, None, None]
    return jnp.sum(jnp.where(eid[None] == eids, table[:, None, :], 0), axis=0)


def kernel(x, w_in, lambda_q1, lambda_k1, lambda_q2, lambda_k2, subln_g, gmlp_ln_g, gmlp_ln_b,
           w_spatial, b_spatial, w_o, ln1_g, ln1_b, w_router, b_router, w_up, b_up,
           w_down, b_down, ln2_g, ln2_b):
    b, s, d = x.shape
    n = b * s
    x2 = x.reshape(n, d)

    w = w_in[0]
    c = QK_WIDTH
    scale = DIFF_HEAD_DIM ** -0.5
    wq = _head_pairs(w[:, 0:c], w[:, c:2 * c]) * scale
    wk = _head_pairs(w[:, 2 * c:3 * c], w[:, 3 * c:4 * c])
    wqk = jnp.concatenate([wq, wk], axis=1).astype(BF16)
    wvt = w[:, 4 * c:4 * c + DIFF_WIDTH].T.astype(BF16)
    wg = w[:, 4 * c + DIFF_WIDTH:].astype(BF16)
    lamv = jnp.concatenate([lambda_q1, lambda_k1, lambda_q2, lambda_k2], axis=0)
    bsp = jnp.repeat(b_spatial[0].T, GMLP_GROUP_DIM, axis=1)

    assert PROJ_TM == ATTN_T and s % ATTN_T == 0 and n % (MOE_TM * MIX_SUB) == 0
    qq, kk, vt, u, vn = _proj(x2, wqk, wvt, wg, gmlp_ln_g, gmlp_ln_b)
    attn = _attention(qq.reshape(b, s, -1), kk.reshape(b, s, -1),
                      vt.reshape(b, s // ATTN_T, DIFF_WIDTH, ATTN_T),
                      lamv, subln_g.reshape(V_HEAD_DIM, 1)).reshape(n, DIFF_WIDTH)
    x1, eid, gates, rank, cnt, tcnt_l = _mix(attn, u, vn, x2, w_spatial[0], bsp,
                                             w_o[0].astype(BF16), ln1_g, ln1_b,
                                             w_router[0].T, b_router[0][:, None])

    ntiles = n // MOE_TM
    counts = cnt[:, 0].astype(jnp.int32)
    padded = ((counts + FFN_TM - 1) // FFN_TM) * FFN_TM
    end_padded = jnp.cumsum(padded)
    start_padded = end_padded - padded
    tcnt = tcnt_l.reshape(ntiles, N_EXPERTS, LANES)[:, :, 0].astype(jnp.int32)
    before = jnp.cumsum(tcnt, axis=0) - tcnt
    off = jnp.cumsum(tcnt, axis=1) - tcnt
    dst = start_padded[None, :] + before
    tile_base = jnp.repeat((off - before).T, MOE_TM, axis=1)
    pos = _lookup(tile_base, eid) + rank

    p_rows = n * TOP_K + N_EXPERTS * FFN_TM
    nblk = p_rows // FFN_TM
    nused = (end_padded[-1:] // FFN_TM).astype(jnp.int32)
    blk_start = jnp.arange(nblk, dtype=jnp.int32) * FFN_TM
    blk_e = jnp.minimum(jnp.sum((end_padded[None, :] <= blk_start[:, None]).astype(jnp.int32), axis=1),
                        N_EXPERTS - 1)
    tcnt_f, off_f, dst_f = tcnt.reshape(-1), off.reshape(-1), dst.reshape(-1)

    xs = _dispatch(x1, pos, tcnt_f, off_f, dst_f, start_padded + counts, padded - counts, nused,
                   p_rows)
    y = _ffn(xs, blk_e, nused, w_up[0], b_up[0][:, None, :], w_down[0], b_down[0][:, None, :])
    out = _combine(y, pos.T, gates.T, tcnt_f, off_f, dst_f, x1, ln2_g, ln2_b)
    return out.reshape(b, s, d)
```

```python
import jax
import jax.numpy as jnp
from jax import lax
from jax.experimental import pallas as pl
from jax.experimental.pallas import tpu as pltpu

DIFF_HEADS = 4
DIFF_HEAD_DIM = 64
V_HEAD_DIM = 2 * DIFF_HEAD_DIM
QK_WIDTH = DIFF_HEADS * DIFF_HEAD_DIM
DIFF_WIDTH = DIFF_HEADS * V_HEAD_DIM
GMLP_GROUPS = 8
GMLP_GROUP_DIM = 64
GMLP_WIDTH = GMLP_GROUPS * GMLP_GROUP_DIM
CHUNK = 128
N_EXPERTS = 32
TOP_K = 4
SWIGLU_LIMIT = 7.0
SWIGLU_ALPHA = 1.702
LN_EPS = 1e-5
DEPTH = 1
DEEPNORM_ALPHA = (2.0 * DEPTH) ** 0.25
LAMBDA_INIT = 0.8 - 0.6 * 1.0

LANES = 128

PROJ_TM = 512
ATTN_T = 512
MOE_TM = 256
MIX_SUB = 2
FFN_TM = 512
WEIGHT_CAST_COLS = 256
RUN_BITS = MOE_TM.bit_length()

VMEM_LIMIT = 48 * 1024 * 1024
FFN_VMEM_LIMIT = 58 * 1024 * 1024

BF16 = jnp.bfloat16
F32 = jnp.float32


def _layer_norm(y, g, b):
    mu = jnp.mean(y, axis=-1, keepdims=True)
    yc = y - mu
    var = jnp.mean(yc * yc, axis=-1, keepdims=True)
    return yc * lax.rsqrt(var + LN_EPS) * g + b


def _gelu(x):
    return 0.5 * x * (1.0 + lax.erf(x * (2.0 ** -0.5)))


def _nt_dot(a, b):
    return lax.dot_general(a, b, (((1,), (1,)), ((), ())), preferred_element_type=F32)


def _slabs_to_rows(ref, first, nrows, sub):
    return jnp.concatenate([ref[pl.ds(first + j, nrows, stride=sub), :] for j in range(sub)], axis=1)


def _rows_to_slabs(ref, first, rows, sub):
    for j in range(sub):
        ref[pl.ds(first + j, rows.shape[0], stride=sub), :] = rows[:, j * LANES:(j + 1) * LANES]


def _proj_kernel(x_ref, wqk_ref, wvt_ref, wg_ref, lng_ref, lnb_ref,
                 qq_ref, kk_ref, vt_ref, u_ref, vn_ref):
    xb = x_ref[...].astype(BF16)
    qk = jnp.dot(xb, wqk_ref[...], preferred_element_type=F32)
    qq_ref[...] = qk[:, :DIFF_WIDTH].astype(BF16)
    kk_ref[...] = qk[:, DIFF_WIDTH:].astype(BF16)
    vt_ref[0] = _nt_dot(wvt_ref[...], xb).astype(BF16)
    g = jnp.dot(xb, wg_ref[...], preferred_element_type=F32)
    u_ref[...] = _gelu(g[:, :GMLP_WIDTH]).astype(BF16)
    gv = _gelu(g[:, GMLP_WIDTH:])
    vn_ref[...] = _layer_norm(gv, lng_ref[...], lnb_ref[...]).astype(BF16)


def _proj(x2, wqk, wvt, wg, lng, lnb):
    n, d = x2.shape
    tm = PROJ_TM
    row = lambda i: (i, 0)
    const = lambda i: (0, 0)
    out = jax.ShapeDtypeStruct((n, DIFF_WIDTH), BF16)
    vt_out = jax.ShapeDtypeStruct((n // tm, DIFF_WIDTH, tm), BF16)
    rowspec = pl.BlockSpec((tm, DIFF_WIDTH), row)
    return pl.pallas_call(
        _proj_kernel,
        out_shape=(out, out, vt_out, out, out),
        grid=(n // tm,),
        in_specs=[
            pl.BlockSpec((tm, d), row),
            pl.BlockSpec(wqk.shape, const),
            pl.BlockSpec(wvt.shape, const),
            pl.BlockSpec(wg.shape, const),
            pl.BlockSpec(lng.shape, const),
            pl.BlockSpec(lnb.shape, const),
        ],
        out_specs=[rowspec, rowspec, pl.BlockSpec((1, DIFF_WIDTH, tm), lambda i: (i, 0, 0)),
                   rowspec, rowspec],
        compiler_params=pltpu.CompilerParams(
            dimension_semantics=("arbitrary",), vmem_limit_bytes=VMEM_LIMIT),
        name="proj",
    )(x2, wqk, wvt, wg, lng, lnb)


def _attn_kernel(lamv_ref, g_ref, qq_ref, kk_ref, vt_ref, o_ref, s_ref):
    t = ATTN_T
    qi = pl.program_id(2)
    q = qq_ref[0]
    lane = lax.broadcasted_iota(jnp.int32, q.shape, 1)
    first = lane < DIFF_HEAD_DIM
    zero = jnp.zeros_like(q)
    q1 = jnp.where(first, q, zero)
    q2 = jnp.where(first, zero, q)

    def scores(j, buf):
        k = kk_ref[0, pl.ds(pl.multiple_of(j * t, t), t), :]
        s_ref[buf, 0] = _nt_dot(k, q1)
        s_ref[buf, 1] = _nt_dot(k, q2)

    def consume(j, buf, carry, masked):
        vt = vt_ref[0, j]
        new = []
        for mp, (m, l, acc) in enumerate(carry):
            s = s_ref[buf, mp]
            if masked:
                key = lax.broadcasted_iota(jnp.int32, s.shape, 0)
                qry = lax.broadcasted_iota(jnp.int32, s.shape, 1)
                s = jnp.where(key <= qry, s, -jnp.inf)
            m_new = jnp.maximum(m, jnp.max(s, axis=0, keepdims=True))
            alpha = jnp.exp(m - m_new)
            p = jnp.exp(s - m_new)
            l_new = alpha * l + jnp.sum(p, axis=0, keepdims=True)
            acc_new = alpha * acc + jnp.dot(vt, p.astype(BF16), preferred_element_type=F32)
            new.append((m_new, l_new, acc_new))
        return tuple(new)

    def init():
        return (jnp.full((1, t), -jnp.inf, F32), jnp.zeros((1, t), F32),
                jnp.zeros((V_HEAD_DIM, t), F32))

    def finish(carry):
        (_, l1, acc1), (_, l2, acc2) = carry
        lv = lamv_ref[...]
        lam = (jnp.exp(jnp.sum(lv[0:1] * lv[1:2], axis=-1, keepdims=True))
               - jnp.exp(jnp.sum(lv[2:3] * lv[3:4], axis=-1, keepdims=True)) + LAMBDA_INIT)
        o = acc1 / l1 - lam * (acc2 / l2)
        ms = jnp.mean(o * o, axis=0, keepdims=True)
        o = o * lax.rsqrt(ms + LN_EPS) * g_ref[...] * (1.0 - LAMBDA_INIT)
        o_ref[0] = o.T.astype(BF16)

    def pair(jj, c):
        j = 2 * jj
        scores(j + 1, 1)
        c = consume(j, 0, c, False)
        scores(j + 2, 0)
        return consume(j + 1, 1, c, False)

    scores(0, 0)
    carry = lax.fori_loop(0, qi // 2, pair, (init(), init()))

    @pl.when(qi % 2 == 0)
    def _():
        finish(consume(qi, 0, carry, True))

    @pl.when(qi % 2 == 1)
    def _():
        scores(qi, 1)
        finish(consume(qi, 1, consume(qi - 1, 0, carry, False), True))


def _attention(qq, kk, vt, lamv, subln_g_col):
    b, s, _ = qq.shape
    t = ATTN_T
    qspec = pl.BlockSpec((1, t, V_HEAD_DIM), lambda bi, h, qi: (bi, qi, h))
    kspec = pl.BlockSpec((1, s, V_HEAD_DIM), lambda bi, h, qi: (bi, 0, h))
    vtspec = pl.BlockSpec((1, s // t, V_HEAD_DIM, t), lambda bi, h, qi: (bi, 0, h, 0))
    const = lambda bi, h, qi: (0, 0)
    return pl.pallas_call(
        _attn_kernel,
        out_shape=jax.ShapeDtypeStruct((b, s, DIFF_WIDTH), BF16),
        grid=(b, DIFF_HEADS, s // t),
        in_specs=[pl.BlockSpec(lamv.shape, const), pl.BlockSpec(subln_g_col.shape, const),
                  qspec, kspec, vtspec],
        out_specs=qspec,
        scratch_shapes=[pltpu.VMEM((2, 2, t, t), F32)],
        compiler_params=pltpu.CompilerParams(
            dimension_semantics=("arbitrary",) * 3, vmem_limit_bytes=VMEM_LIMIT),
        name="attn",
    )(lamv, subln_g_col, qq, kk, vt)


def _mix_kernel(attn_ref, u_ref, vn_ref, x_ref, wsp_ref, bsp_ref, wo_ref, g1_ref, b1_ref,
                wrt_ref, br_ref,
                x1_ref, eid_ref, gate_ref, rank_ref, cnt_ref, tcnt_ref,
                cat_ref, carry_ref):
    tm = MOE_TM

    @pl.when(pl.program_id(0) == 0)
    def _():
        carry_ref[...] = jnp.zeros_like(carry_ref)

    ri = lax.broadcasted_iota(jnp.int32, (CHUNK, CHUNK), 0)
    ci = lax.broadcasted_iota(jnp.int32, (CHUNK, CHUNK), 1)
    tril = ci <= ri
    first = ci < GMLP_GROUP_DIM
    w = [jnp.where(tril, wsp_ref[g], 0.0).astype(BF16) for g in range(GMLP_GROUPS)]
    wr = wrt_ref[...]
    wh = wr.astype(BF16)
    wl = (wr - wh.astype(F32)).astype(BF16)
    eio = lax.broadcasted_iota(jnp.int32, (N_EXPERTS, tm), 0).astype(F32)
    ti = lax.broadcasted_iota(jnp.int32, (tm, tm), 0)
    tj = lax.broadcasted_iota(jnp.int32, (tm, tm), 1)
    before = jnp.where(ti < tj, 1.0, 0.0).astype(BF16)

    carry = carry_ref[:, 0:1]
    for h in range(MIX_SUB):
        r0 = h * tm
        cat_ref[r0:r0 + tm, :DIFF_WIDTH] = attn_ref[r0:r0 + tm, :]
        for c in range(tm // CHUNK):
            rows = slice(r0 + c * CHUNK, r0 + (c + 1) * CHUNK)
            for jb in range(GMLP_WIDTH // LANES):
                cols = slice(jb * LANES, (jb + 1) * LANES)
                vb = vn_ref[rows, cols]
                zero = jnp.zeros_like(vb)
                z = (jnp.dot(w[2 * jb], jnp.where(first, vb, zero), preferred_element_type=F32)
                     + jnp.dot(w[2 * jb + 1], jnp.where(first, zero, vb), preferred_element_type=F32))
                gated = u_ref[rows, cols].astype(F32) * (z + bsp_ref[:, cols])
                cat_ref[rows, DIFF_WIDTH + jb * LANES:DIFF_WIDTH + (jb + 1) * LANES] = gated.astype(BF16)

        mixed = jnp.dot(cat_ref[r0:r0 + tm, :], wo_ref[...], preferred_element_type=F32)
        x1 = _layer_norm(DEEPNORM_ALPHA * x_ref[r0:r0 + tm, :] + mixed, g1_ref[...], b1_ref[...])
        x1_ref[r0:r0 + tm, :] = x1

        xh = x1.astype(BF16)
        xl = (x1 - xh.astype(F32)).astype(BF16)
        logits = _nt_dot(wh, xh) + _nt_dot(wl, xh) + _nt_dot(wh, xl) + br_ref[...]

        vals, idxs, sels = [], [], []
        cur = logits
        for _ in range(TOP_K):
            mx = jnp.max(cur, axis=0, keepdims=True)
            idx = jnp.min(jnp.where(cur == mx, eio, float(N_EXPERTS)), axis=0, keepdims=True)
            sel = eio == idx
            vals.append(mx)
            idxs.append(idx)
            sels.append(sel)
            cur = jnp.where(sel, -jnp.inf, cur)
        ex = [jnp.exp(vk - vals[0]) for vk in vals]
        denom = ex[0] + ex[1] + ex[2] + ex[3]
        gate_ref[:, r0:r0 + tm] = jnp.concatenate([e / denom for e in ex], axis=0)
        eid_ref[:, r0:r0 + tm] = jnp.concatenate(idxs, axis=0).astype(jnp.int32)

        chosen = (sels[0] | sels[1] | sels[2] | sels[3])
        onehot = jnp.where(chosen, 1.0, 0.0)
        cnt_before = jnp.dot(onehot.astype(BF16), before, preferred_element_type=F32) + carry
        ranks = [jnp.sum(jnp.where(s, cnt_before, 0.0), axis=0, keepdims=True) for s in sels]
        rank_ref[:, r0:r0 + tm] = jnp.concatenate(ranks, axis=0).astype(jnp.int32)
        tile_cnt = jnp.sum(onehot, axis=1, keepdims=True)
        tcnt_ref[h * N_EXPERTS:(h + 1) * N_EXPERTS, :] = jnp.broadcast_to(tile_cnt, (N_EXPERTS, LANES))
        carry = carry + tile_cnt

    carry_ref[...] = jnp.broadcast_to(carry, carry_ref.shape)
    cnt_ref[...] = jnp.broadcast_to(carry, cnt_ref.shape)


def _mix(attn, u, vn, x2, wsp, bsp, wo, g1, b1, wrt, br):
    n, d = x2.shape
    tm = MOE_TM * MIX_SUB
    row = lambda i: (i, 0)
    col = lambda i: (0, i)
    const2 = lambda i: (0, 0)
    const3 = lambda i: (0, 0, 0)
    tok = lambda dt: jax.ShapeDtypeStruct((TOP_K, n), dt)
    return pl.pallas_call(
        _mix_kernel,
        out_shape=(jax.ShapeDtypeStruct((n, d), F32), tok(jnp.int32), tok(F32), tok(jnp.int32),
                   jax.ShapeDtypeStruct((N_EXPERTS, LANES), F32),
                   jax.ShapeDtypeStruct((n // MOE_TM * N_EXPERTS, LANES), F32)),
        grid=(n // tm,),
        in_specs=[
            pl.BlockSpec((tm, DIFF_WIDTH), row),
            pl.BlockSpec((tm, GMLP_WIDTH), row),
            pl.BlockSpec((tm, GMLP_WIDTH), row),
            pl.BlockSpec((tm, d), row),
            pl.BlockSpec(wsp.shape, const3),
            pl.BlockSpec(bsp.shape, const2),
            pl.BlockSpec(wo.shape, const2),
            pl.BlockSpec(g1.shape, const2),
            pl.BlockSpec(b1.shape, const2),
            pl.BlockSpec(wrt.shape, const2),
            pl.BlockSpec(br.shape, const2),
        ],
        out_specs=[
            pl.BlockSpec((tm, d), row),
            pl.BlockSpec((TOP_K, tm), col),
            pl.BlockSpec((TOP_K, tm), col),
            pl.BlockSpec((TOP_K, tm), col),
            pl.BlockSpec((N_EXPERTS, LANES), const2),
            pl.BlockSpec((MIX_SUB * N_EXPERTS, LANES), row),
        ],
        scratch_shapes=[pltpu.VMEM((tm, DIFF_WIDTH + GMLP_WIDTH), BF16),
                        pltpu.VMEM((N_EXPERTS, LANES), F32)],
        compiler_params=pltpu.CompilerParams(
            dimension_semantics=("arbitrary",), vmem_limit_bytes=VMEM_LIMIT),
        name="mix",
    )(attn, u, vn, x2, wsp, bsp, wo, g1, b1, wrt, br)


def _for_each_run(tile, tcnt_ref, off_ref, dst_ref, make_copy):
    def per_expert(e, carry):
        n = tcnt_ref[tile * N_EXPERTS + e]
        off = off_ref[tile * N_EXPERTS + e]
        dst = dst_ref[tile * N_EXPERTS + e]
        for bit in range(RUN_BITS):
            done = (n >> (bit + 1)) << (bit + 1)

            @pl.when(((n >> bit) & 1) == 1)
            def _():
                make_copy(off + done, dst + done, 1 << bit).start()
        return carry

    lax.fori_loop(0, N_EXPERTS, per_expert, 0)


def _dispatch_kernel(tcnt_ref, off_ref, dst_ref, pad_start_ref, pad_len_ref, nused_ref,
                     pos_ref, x1_ref, xs_hbm, sorted_ref, zero_ref, sem, zsem):
    tm = MOE_TM
    rows = TOP_K * tm
    d = x1_ref.shape[1]
    sub = d // LANES
    nblk = xs_hbm.shape[0] // (FFN_TM * sub)
    i = pl.program_id(0)
    last = pl.num_programs(0) - 1
    slot = lax.rem(i, 2)

    def wait_runs(s):
        pltpu.make_async_copy(sorted_ref.at[pl.ds(s * rows * sub, rows * sub)],
                              xs_hbm.at[pl.ds(0, rows * sub)], sem.at[s]).wait()

    @pl.when(i >= 2)
    def _():
        wait_runs(slot)

    pos = pos_ref[...]
    ri = lax.broadcasted_iota(jnp.int32, (rows, tm), 0)
    hit = (ri == pos[0:1]) | (ri == pos[1:2]) | (ri == pos[2:3]) | (ri == pos[3:4])
    perm = jnp.where(hit, 1.0, 0.0).astype(BF16)
    srt = jnp.dot(perm, x1_ref[...].astype(BF16), preferred_element_type=F32)
    base = pl.multiple_of(slot * (rows * sub), rows * sub)
    _rows_to_slabs(sorted_ref, base, srt, sub)

    def run_copy(local_row, global_row, nrows):
        return pltpu.make_async_copy(
            sorted_ref.at[pl.ds(pl.multiple_of((slot * rows + local_row) * sub, sub), nrows * sub)],
            xs_hbm.at[pl.ds(pl.multiple_of(global_row * sub, sub), nrows * sub)], sem.at[slot])

    _for_each_run(i, tcnt_ref, off_ref, dst_ref, run_copy)

    @pl.when(i == 0)
    def _():
        zero_ref[...] = jnp.zeros_like(zero_ref)

        def zero_copy(r, nrows):
            return pltpu.make_async_copy(
                zero_ref.at[pl.ds(0, nrows * sub)],
                xs_hbm.at[pl.ds(pl.multiple_of(r * sub, sub), nrows * sub)], zsem)

        def tail_copy(blk):
            return pltpu.make_async_copy(
                zero_ref, xs_hbm.at[pl.ds(pl.multiple_of(blk * (FFN_TM * sub), FFN_TM * sub),
                                          FFN_TM * sub)], zsem)

        def tail_start(blk, c):
            tail_copy(blk).start()
            return c

        def tail_wait(blk, c):
            tail_copy(blk).wait()
            return c

        lax.fori_loop(nused_ref[0], nblk, tail_start, 0)
        lax.fori_loop(nused_ref[0], nblk, tail_wait, 0)

        def per_expert(e, _):
            s = pad_start_ref[e]
            cnt = pad_len_ref[e]
            for wait in (False, True):
                for bit in range(FFN_TM.bit_length() - 1):
                    done = (cnt >> (bit + 1)) << (bit + 1)

                    @pl.when(((cnt >> bit) & 1) == 1)
                    def _():
                        piece = zero_copy(s + done, 1 << bit)
                        piece.wait() if wait else piece.start()
            return 0

        lax.fori_loop(0, N_EXPERTS, per_expert, 0)

    @pl.when(i == last)
    def _():
        wait_runs(slot)

        @pl.when(last >= 1)
        def _():
            wait_runs(1 - slot)


def _dispatch(x1, pos, tcnt, off, dst, pad_start, pad_len, nused, p_rows):
    n, d = x1.shape
    tm = MOE_TM
    sub = d // LANES
    return pl.pallas_call(
        _dispatch_kernel,
        out_shape=jax.ShapeDtypeStruct((p_rows * sub, LANES), F32),
        grid_spec=pltpu.PrefetchScalarGridSpec(
            num_scalar_prefetch=6,
            grid=(n // tm,),
            in_specs=[
                pl.BlockSpec((TOP_K, tm), lambda i, *_: (0, i)),
                pl.BlockSpec((tm, d), lambda i, *_: (i, 0)),
            ],
            out_specs=pl.BlockSpec(memory_space=pl.ANY),
            scratch_shapes=[pltpu.VMEM((2 * TOP_K * tm * sub, LANES), F32),
                            pltpu.VMEM((FFN_TM * sub, LANES), F32),
                            pltpu.SemaphoreType.DMA((2,)), pltpu.SemaphoreType.DMA(())],
        ),
        compiler_params=pltpu.CompilerParams(
            dimension_semantics=("arbitrary",), vmem_limit_bytes=VMEM_LIMIT),
        name="dispatch",
    )(tcnt, off, dst, pad_start, pad_len, nused, pos, x1)


def _ffn_kernel(blk_e_ref, nused_ref, xs_ref, wup_ref, bup_ref, wdn_ref, bdn_ref, y_ref,
                wup_b, wdn_b):
    tm = FFN_TM
    dff = wdn_ref.shape[1]
    sub = wdn_ref.shape[2] // LANES
    i = pl.program_id(0)
    used = i < nused_ref[0]

    @pl.when(jnp.logical_not(used))
    def _():
        y_ref[...] = jnp.zeros_like(y_ref)

    new_expert = jnp.logical_or(i == 0, blk_e_ref[i] != blk_e_ref[jnp.maximum(i - 1, 0)])

    @pl.when(jnp.logical_and(used, new_expert))
    def _():
        for c in range(0, 2 * dff, WEIGHT_CAST_COLS):
            wup_b[:, c:c + WEIGHT_CAST_COLS] = wup_ref[0, :, c:c + WEIGHT_CAST_COLS].astype(BF16)
        for c in range(0, wdn_ref.shape[2], WEIGHT_CAST_COLS):
            wdn_b[:, c:c + WEIGHT_CAST_COLS] = wdn_ref[0, :, c:c + WEIGHT_CAST_COLS].astype(BF16)

    @pl.when(used)
    def _():
        xb = _slabs_to_rows(xs_ref, 0, tm, sub).astype(BF16)
        h = jnp.dot(xb, wup_b[...], preferred_element_type=F32) + bup_ref[0]
        gate = jnp.minimum(h[:, :dff], SWIGLU_LIMIT)
        lin = jnp.clip(h[:, dff:], -SWIGLU_LIMIT, SWIGLU_LIMIT)
        act = (lin + 1.0) * gate * jax.nn.sigmoid(SWIGLU_ALPHA * gate)
        y = jnp.dot(act.astype(BF16), wdn_b[...], preferred_element_type=F32) + bdn_ref[0]
        _rows_to_slabs(y_ref, 0, y, sub)


def _ffn(xs, blk_e, nused, wup, bup, wdn, bdn):
    tm = FFN_TM
    e, d, dff2 = wup.shape
    dff = dff2 // 2
    sub = d // LANES
    rowblk = lambda i, be, nu: (i, 0)
    wsel = lambda i, be, nu: (be[i], 0, 0)
    return pl.pallas_call(
        _ffn_kernel,
        out_shape=jax.ShapeDtypeStruct(xs.shape, F32),
        grid_spec=pltpu.PrefetchScalarGridSpec(
            num_scalar_prefetch=2,
            grid=(xs.shape[0] // (tm * sub),),
            in_specs=[
                pl.BlockSpec((tm * sub, LANES), rowblk),
                pl.BlockSpec((1, d, dff2), wsel),
                pl.BlockSpec((1, 1, dff2), wsel),
                pl.BlockSpec((1, dff, d), wsel),
                pl.BlockSpec((1, 1, d), wsel),
            ],
            out_specs=pl.BlockSpec((tm * sub, LANES), rowblk),
            scratch_shapes=[pltpu.VMEM((d, dff2), BF16), pltpu.VMEM((dff, d), BF16)],
        ),
        compiler_params=pltpu.CompilerParams(
            dimension_semantics=("arbitrary",), vmem_limit_bytes=FFN_VMEM_LIMIT),
        name="ffn",
    )(blk_e, nused, xs, wup, bup, wdn, bdn)


def _combine_kernel(tcnt_ref, off_ref, dst_ref, pos_ref, gate_ref, x1_ref, g2_ref, b2_ref, y_hbm,
                    o_ref, stage_ref, sem):
    tm = MOE_TM
    rows = TOP_K * tm
    sub = x1_ref.shape[1] // LANES
    i = pl.program_id(0)
    nsteps = pl.num_programs(0)
    slot = lax.rem(i, 2)

    def fetch_runs(tile, s):
        def run_copy(local_row, global_row, nrows):
            return pltpu.make_async_copy(
                y_hbm.at[pl.ds(pl.multiple_of(global_row * sub, sub), nrows * sub)],
                stage_ref.at[pl.ds(pl.multiple_of((s * rows + local_row) * sub, sub), nrows * sub)],
                sem.at[s])
        _for_each_run(tile, tcnt_ref, off_ref, dst_ref, run_copy)

    @pl.when(i == 0)
    def _():
        fetch_runs(0, 0)

    @pl.when(i + 1 < nsteps)
    def _():
        fetch_runs(i + 1, 1 - slot)

    pltpu.make_async_copy(y_hbm.at[pl.ds(0, rows * sub)],
                          stage_ref.at[pl.ds(slot * rows * sub, rows * sub)], sem.at[slot]).wait()

    ys = _slabs_to_rows(stage_ref, pl.multiple_of(slot * (rows * sub), rows * sub), rows, sub)
    yh = ys.astype(BF16)
    yl = (ys - yh.astype(F32)).astype(BF16)
    pos = pos_ref[...]
    gates = gate_ref[...]
    ci = lax.broadcasted_iota(jnp.int32, (tm, rows), 1)
    w = jnp.where(ci == pos[:, 0:1], gates[:, 0:1], 0.0)
    for k in range(1, TOP_K):
        w = w + jnp.where(ci == pos[:, k:k + 1], gates[:, k:k + 1], 0.0)
    wh = w.astype(BF16)
    wl = (w - wh.astype(F32)).astype(BF16)
    ffn = (jnp.dot(wh, yh, preferred_element_type=F32) + jnp.dot(wl, yh, preferred_element_type=F32)
           + jnp.dot(wh, yl, preferred_element_type=F32))
    o_ref[...] = _layer_norm(DEEPNORM_ALPHA * x1_ref[...] + ffn, g2_ref[...], b2_ref[...])


def _combine(y, pos_tok, gates_tok, tcnt, off, dst, x1, g2, b2):
    n, d = x1.shape
    tm = MOE_TM
    sub = d // LANES
    row = lambda i, *_: (i, 0)
    const = lambda i, *_: (0, 0)
    return pl.pallas_call(
        _combine_kernel,
        out_shape=jax.ShapeDtypeStruct((n, d), F32),
        grid_spec=pltpu.PrefetchScalarGridSpec(
            num_scalar_prefetch=3,
            grid=(n // tm,),
            in_specs=[
                pl.BlockSpec((tm, TOP_K), row),
                pl.BlockSpec((tm, TOP_K), row),
                pl.BlockSpec((tm, d), row),
                pl.BlockSpec(g2.shape, const),
                pl.BlockSpec(b2.shape, const),
                pl.BlockSpec(memory_space=pl.ANY),
            ],
            out_specs=pl.BlockSpec((tm, d), row),
            scratch_shapes=[pltpu.VMEM((2 * TOP_K * tm * sub, LANES), F32),
                            pltpu.SemaphoreType.DMA((2,))],
        ),
        compiler_params=pltpu.CompilerParams(
            dimension_semantics=("arbitrary",), vmem_limit_bytes=VMEM_LIMIT),
        name="combine",
    )(tcnt, off, dst, pos_tok, gates_tok, x1, g2, b2, y)


def _head_pairs(w_a, w_b):
    d = w_a.shape[0]
    a = w_a.reshape(d, DIFF_HEADS, DIFF_HEAD_DIM)
    b = w_b.reshape(d, DIFF_HEADS, DIFF_HEAD_DIM)
    return jnp.concatenate([a, b], axis=-1).reshape(d, DIFF_HEADS * V_HEAD_DIM)


def _lookup(table, eid):
    eids = jnp.arange(N_EXPERTS, dtype=jnp.int32)[:, None, None]
    return jnp.sum(jnp.where(eid[None] == eids, table[:, None, :], 0), axis=0)


def kernel(x, w_in, lambda_q1, lambda_k1, lambda_q2, lambda_k2, subln_g, gmlp_ln_g, gmlp_ln_b,
           w_spatial, b_spatial, w_o, ln1_g, ln1_b, w_router, b_router, w_up, b_up,
           w_down, b_down, ln2_g, ln2_b):
    b, s, d = x.shape
    n = b * s
    x2 = x.reshape(n, d)

    w = w_in[0]
    c = QK_WIDTH
    scale = DIFF_HEAD_DIM ** -0.5
    wq = _head_pairs(w[:, 0:c], w[:, c:2 * c]) * scale
    wk = _head_pairs(w[:, 2 * c:3 * c], w[:, 3 * c:4 * c])
    wqk = jnp.concatenate([wq, wk], axis=1).astype(BF16)
    wvt = w[:, 4 * c:4 * c + DIFF_WIDTH].T.astype(BF16)
    wg = w[:, 4 * c + DIFF_WIDTH:].astype(BF16)
    lamv = jnp.concatenate([lambda_q1, lambda_k1, lambda_q2, lambda_k2], axis=0)
    bsp = jnp.repeat(b_spatial[0].T, GMLP_GROUP_DIM, axis=1)

    assert PROJ_TM == ATTN_T and s % ATTN_T == 0 and n % (MOE_TM * MIX_SUB) == 0
    qq, kk, vt, u, vn = _proj(x2, wqk, wvt, wg, gmlp_ln_g, gmlp_ln_b)
    attn = _attention(qq.reshape(b, s, -1), kk.reshape(b, s, -1),
                      vt.reshape(b, s // ATTN_T, DIFF_WIDTH, ATTN_T),
                      lamv, subln_g.reshape(V_HEAD_DIM, 1)).reshape(n, DIFF_WIDTH)
    x1, eid, gates, rank, cnt, tcnt_l = _mix(attn, u, vn, x2, w_spatial[0], bsp,
                                             w_o[0].astype(BF16), ln1_g, ln1_b,
                                             w_router[0].T, b_router[0][:, None])

    ntiles = n // MOE_TM
    counts = cnt[:, 0].astype(jnp.int32)
    padded = ((counts + FFN_TM - 1) // FFN_TM) * FFN_TM
    end_padded = jnp.cumsum(padded)
    start_padded = end_padded - padded
    tcnt = tcnt_l.reshape(ntiles, N_EXPERTS, LANES)[:, :, 0].astype(jnp.int32)
    before = jnp.cumsum(tcnt, axis=0) - tcnt
    off = jnp.cumsum(tcnt, axis=1) - tcnt
    dst = start_padded[None, :] + before
    tile_base = jnp.repeat((off - before).T, MOE_TM, axis=1)
    pos = _lookup(tile_base, eid) + rank

    p_rows = n * TOP_K + N_EXPERTS * FFN_TM
    nblk = p_rows // FFN_TM
    nused = (end_padded[-1:] // FFN_TM).astype(jnp.int32)
    blk_start = jnp.arange(nblk, dtype=jnp.int32) * FFN_TM
    blk_e = jnp.minimum(jnp.sum((end_padded[None, :] <= blk_start[:, None]).astype(jnp.int32), axis=1),
                        N_EXPERTS - 1)
    tcnt_f, off_f, dst_f = tcnt.reshape(-1), off.reshape(-1), dst.reshape(-1)

    xs = _dispatch(x1, pos, tcnt_f, off_f, dst_f, start_padded + counts, padded - counts, nused,
                   p_rows)
    y = _ffn(xs, blk_e, nused, w_up[0], b_up[0][:, None, :], w_down[0], b_down[0][:, None, :])
    out = _combine(y, pos.T, gates.T, tcnt_f, off_f, dst_f, x1, ln2_g, ln2_b)
    return out.reshape(b, s, d)
```

```python
import jax
import jax.numpy as jnp
from jax import lax
from jax.experimental import pallas as pl
from jax.experimental.pallas import tpu as pltpu

DIFF_HEADS = 4
DIFF_HEAD_DIM = 64
V_HEAD_DIM = 2 * DIFF_HEAD_DIM
QK_WIDTH = DIFF_HEADS * DIFF_HEAD_DIM
DIFF_WIDTH = DIFF_HEADS * V_HEAD_DIM
GMLP_GROUPS = 8
GMLP_GROUP_DIM = 64
GMLP_WIDTH = GMLP_GROUPS * GMLP_GROUP_DIM
CHUNK = 128
N_EXPERTS = 32
TOP_K = 4
SWIGLU_LIMIT = 7.0
SWIGLU_ALPHA = 1.702
LN_EPS = 1e-5
DEPTH = 1
DEEPNORM_ALPHA = (2.0 * DEPTH) ** 0.25
LAMBDA_INIT = 0.8 - 0.6 * 1.0
LOG2_E = 1.4426950408889634

LANES = 128

PROJ_TM = 512
ATTN_T = 512
MOE_TM = 256
MIX_SUB = 2
FFN_TM = 512
WEIGHT_CAST_COLS = 256
RUN_BITS = MOE_TM.bit_length()

VMEM_LIMIT = 48 * 1024 * 1024

BF16 = jnp.bfloat16
F32 = jnp.float32


def _layer_norm(y, g, b):
    mu = jnp.mean(y, axis=-1, keepdims=True)
    yc = y - mu
    var = jnp.mean(yc * yc, axis=-1, keepdims=True)
    return yc * lax.rsqrt(var + LN_EPS) * g + b


def _gelu(x):
    return 0.5 * x * (1.0 + lax.erf(x * (2.0 ** -0.5)))


def _nt_dot(a, b):
    return lax.dot_general(a, b, (((1,), (1,)), ((), ())), preferred_element_type=F32)


def _slabs_to_rows(ref, first, nrows, sub):
    return jnp.concatenate([ref[pl.ds(first + j, nrows, stride=sub), :] for j in range(sub)], axis=1)


def _rows_to_slabs(ref, first, rows, sub):
    for j in range(sub):
        ref[pl.ds(first + j, rows.shape[0], stride=sub), :] = rows[:, j * LANES:(j + 1) * LANES]


def _proj_kernel(x_ref, wqk_ref, wvt_ref, wg_ref, lng_ref, lnb_ref,
                 qq_ref, kk_ref, vt_ref, u_ref, vn_ref):
    xb = x_ref[...].astype(BF16)
    qk = jnp.dot(xb, wqk_ref[...], preferred_element_type=F32)
    qq_ref[...] = qk[:, :DIFF_WIDTH].astype(BF16)
    kk_ref[...] = qk[:, DIFF_WIDTH:].astype(BF16)
    vt_ref[0] = _nt_dot(wvt_ref[...], xb).astype(BF16)
    g = jnp.dot(xb, wg_ref[...], preferred_element_type=F32)
    u_ref[...] = _gelu(g[:, :GMLP_WIDTH]).astype(BF16)
    gv = _gelu(g[:, GMLP_WIDTH:])
    vn_ref[...] = _layer_norm(gv, lng_ref[...], lnb_ref[...]).astype(BF16)


def _proj(x2, wqk, wvt, wg, lng, lnb):
    n, d = x2.shape
    tm = PROJ_TM
    row = lambda i: (i, 0)
    const = lambda i: (0, 0)
    out = jax.ShapeDtypeStruct((n, DIFF_WIDTH), BF16)
    vt_out = jax.ShapeDtypeStruct((n // tm, DIFF_WIDTH, tm), BF16)
    rowspec = pl.BlockSpec((tm, DIFF_WIDTH), row)
    return pl.pallas_call(
        _proj_kernel,
        out_shape=(out, out, vt_out, out, out),
        grid=(n // tm,),
        in_specs=[
            pl.BlockSpec((tm, d), row),
            pl.BlockSpec(wqk.shape, const),
            pl.BlockSpec(wvt.shape, const),
            pl.BlockSpec(wg.shape, const),
            pl.BlockSpec(lng.shape, const),
            pl.BlockSpec(lnb.shape, const),
        ],
        out_specs=[rowspec, rowspec, pl.BlockSpec((1, DIFF_WIDTH, tm), lambda i: (i, 0, 0)),
                   rowspec, rowspec],
        compiler_params=pltpu.CompilerParams(
            dimension_semantics=("arbitrary",), vmem_limit_bytes=VMEM_LIMIT),
        name="proj",
    )(x2, wqk, wvt, wg, lng, lnb)


def _attn_kernel(lamv_ref, g_ref, qq_ref, kk_ref, vt_ref, o_ref, s_ref):
    t = ATTN_T
    qi = pl.program_id(2)
    q = qq_ref[0]
    lane = lax.broadcasted_iota(jnp.int32, q.shape, 1)
    first = lane < DIFF_HEAD_DIM
    zero = jnp.zeros_like(q)
    q1 = jnp.where(first, q, zero)
    q2 = jnp.where(first, zero, q)

    def scores(j, buf):
        k = kk_ref[0, pl.ds(pl.multiple_of(j * t, t), t), :]
        s_ref[buf, 0] = _nt_dot(k, q1)
        s_ref[buf, 1] = _nt_dot(k, q2)

    def consume(j, buf, carry, masked):
        vt = vt_ref[0, j]
        new = []
        for mp, (m, l, acc) in enumerate(carry):
            s = s_ref[buf, mp]
            if masked:
                key = lax.broadcasted_iota(jnp.int32, s.shape, 0)
                qry = lax.broadcasted_iota(jnp.int32, s.shape, 1)
                s = jnp.where(key <= qry, s, -jnp.inf)
            m_new = jnp.maximum(m, jnp.max(s, axis=0, keepdims=True))
            alpha = jnp.exp2(m - m_new)
            p = jnp.exp2(s - m_new)
            l_new = alpha * l + jnp.sum(p, axis=0, keepdims=True)
            acc_new = alpha * acc + jnp.dot(vt, p.astype(BF16), preferred_element_type=F32)
            new.append((m_new, l_new, acc_new))
        return tuple(new)

    def init():
        return (jnp.full((1, t), -jnp.inf, F32), jnp.zeros((1, t), F32),
                jnp.zeros((V_HEAD_DIM, t), F32))

    def finish(carry):
        (_, l1, acc1), (_, l2, acc2) = carry
        lv = lamv_ref[...]
        lam = (jnp.exp(jnp.sum(lv[0:1] * lv[1:2], axis=-1, keepdims=True))
               - jnp.exp(jnp.sum(lv[2:3] * lv[3:4], axis=-1, keepdims=True)) + LAMBDA_INIT)
        o = acc1 / l1 - lam * (acc2 / l2)
        ms = jnp.mean(o * o, axis=0, keepdims=True)
        o = o * lax.rsqrt(ms + LN_EPS) * g_ref[...] * (1.0 - LAMBDA_INIT)
        o_ref[0] = o.T.astype(BF16)

    def pair(jj, c):
        j = 2 * jj
        scores(j + 1, 1)
        c = consume(j, 0, c, False)
        scores(j + 2, 0)
        return consume(j + 1, 1, c, False)

    scores(0, 0)
    carry = lax.fori_loop(0, qi // 2, pair, (init(), init()))

    @pl.when(qi % 2 == 0)
    def _():
        finish(consume(qi, 0, carry, True))

    @pl.when(qi % 2 == 1)
    def _():
        scores(qi, 1)
        finish(consume(qi, 1, consume(qi - 1, 0, carry, False), True))


def _attention(qq, kk, vt, lamv, subln_g_col):
    b, s, _ = qq.shape
    t = ATTN_T
    qspec = pl.BlockSpec((1, t, V_HEAD_DIM), lambda bi, h, qi: (bi, qi, h))
    kspec = pl.BlockSpec((1, s, V_HEAD_DIM), lambda bi, h, qi: (bi, 0, h))
    vtspec = pl.BlockSpec((1, s // t, V_HEAD_DIM, t), lambda bi, h, qi: (bi, 0, h, 0))
    const = lambda bi, h, qi: (0, 0)
    return pl.pallas_call(
        _attn_kernel,
        out_shape=jax.ShapeDtypeStruct((b, s, DIFF_WIDTH), BF16),
        grid=(b, DIFF_HEADS, s // t),
        in_specs=[pl.BlockSpec(lamv.shape, const), pl.BlockSpec(subln_g_col.shape, const),
                  qspec, kspec, vtspec],
        out_specs=qspec,
        scratch_shapes=[pltpu.VMEM((2, 2, t, t), F32)],
        compiler_params=pltpu.CompilerParams(
            dimension_semantics=("arbitrary",) * 3, vmem_limit_bytes=VMEM_LIMIT),
        name="attn",
    )(lamv, subln_g_col, qq, kk, vt)


def _mix_kernel(attn_ref, u_ref, vn_ref, x_ref, wsp_ref, bsp_ref, wo_ref, g1_ref, b1_ref,
                wrt_ref, br_ref,
                x1_ref, eid_ref, gate_ref, rank_ref, cnt_ref, tcnt_ref,
                cat_ref, carry_ref):
    tm = MOE_TM

    @pl.when(pl.program_id(0) == 0)
    def _():
        carry_ref[...] = jnp.zeros_like(carry_ref)

    ri = lax.broadcasted_iota(jnp.int32, (CHUNK, CHUNK), 0)
    ci = lax.broadcasted_iota(jnp.int32, (CHUNK, CHUNK), 1)
    tril = ci <= ri
    first = ci < GMLP_GROUP_DIM
    w = [jnp.where(tril, wsp_ref[g], 0.0).astype(BF16) for g in range(GMLP_GROUPS)]
    wr = wrt_ref[...]
    wh = wr.astype(BF16)
    wl = (wr - wh.astype(F32)).astype(BF16)
    eio = lax.broadcasted_iota(jnp.int32, (N_EXPERTS, tm), 0).astype(F32)
    ti = lax.broadcasted_iota(jnp.int32, (tm, tm), 0)
    tj = lax.broadcasted_iota(jnp.int32, (tm, tm), 1)
    before = jnp.where(ti < tj, 1.0, 0.0).astype(BF16)

    carry = carry_ref[:, 0:1]
    for h in range(MIX_SUB):
        r0 = h * tm
        cat_ref[r0:r0 + tm, :DIFF_WIDTH] = attn_ref[r0:r0 + tm, :]
        for c in range(tm // CHUNK):
            rows = slice(r0 + c * CHUNK, r0 + (c + 1) * CHUNK)
            for jb in range(GMLP_WIDTH // LANES):
                cols = slice(jb * LANES, (jb + 1) * LANES)
                vb = vn_ref[rows, cols]
                zero = jnp.zeros_like(vb)
                z = (jnp.dot(w[2 * jb], jnp.where(first, vb, zero), preferred_element_type=F32)
                     + jnp.dot(w[2 * jb + 1], jnp.where(first, zero, vb), preferred_element_type=F32))
                gated = u_ref[rows, cols].astype(F32) * (z + bsp_ref[:, cols])
                cat_ref[rows, DIFF_WIDTH + jb * LANES:DIFF_WIDTH + (jb + 1) * LANES] = gated.astype(BF16)

        mixed = jnp.dot(cat_ref[r0:r0 + tm, :], wo_ref[...], preferred_element_type=F32)
        x1 = _layer_norm(DEEPNORM_ALPHA * x_ref[r0:r0 + tm, :] + mixed, g1_ref[...], b1_ref[...])
        x1_ref[r0:r0 + tm, :] = x1

        xh = x1.astype(BF16)
        xl = (x1 - xh.astype(F32)).astype(BF16)
        logits = _nt_dot(wh, xh) + _nt_dot(wl, xh) + _nt_dot(wh, xl) + br_ref[...]

        vals, idxs, sels = [], [], []
        cur = logits
        for _ in range(TOP_K):
            mx = jnp.max(cur, axis=0, keepdims=True)
            idx = jnp.min(jnp.where(cur == mx, eio, float(N_EXPERTS)), axis=0, keepdims=True)
            sel = eio == idx
            vals.append(mx)
            idxs.append(idx)
            sels.append(sel)
            cur = jnp.where(sel, -jnp.inf, cur)
        ex = [jnp.exp(vk - vals[0]) for vk in vals]
        denom = ex[0] + ex[1] + ex[2] + ex[3]
        gate_ref[:, r0:r0 + tm] = jnp.concatenate([e / denom for e in ex], axis=0)
        eid_ref[:, r0:r0 + tm] = jnp.concatenate(idxs, axis=0).astype(jnp.int32)

        chosen = (sels[0] | sels[1] | sels[2] | sels[3])
        onehot = jnp.where(chosen, 1.0, 0.0)
        cnt_before = jnp.dot(onehot.astype(BF16), before, preferred_element_type=F32) + carry
        ranks = [jnp.sum(jnp.where(s, cnt_before, 0.0), axis=0, keepdims=True) for s in sels]
        rank_ref[:, r0:r0 + tm] = jnp.concatenate(ranks, axis=0).astype(jnp.int32)
        tile_cnt = jnp.sum(onehot, axis=1, keepdims=True)
        tcnt_ref[h * N_EXPERTS:(h + 1) * N_EXPERTS, :] = jnp.broadcast_to(tile_cnt, (N_EXPERTS, LANES))
        carry = carry + tile_cnt

    carry_ref[...] = jnp.broadcast_to(carry, carry_ref.shape)
    cnt_ref[...] = jnp.broadcast_to(carry, cnt_ref.shape)


def _mix(attn, u, vn, x2, wsp, bsp, wo, g1, b1, wrt, br):
    n, d = x2.shape
    tm = MOE_TM * MIX_SUB
    row = lambda i: (i, 0)
    col = lambda i: (0, i)
    const2 = lambda i: (0, 0)
    const3 = lambda i: (0, 0, 0)
    tok = lambda dt: jax.ShapeDtypeStruct((TOP_K, n), dt)
    return pl.pallas_call(
        _mix_kernel,
        out_shape=(jax.ShapeDtypeStruct((n, d), F32), tok(jnp.int32), tok(F32), tok(jnp.int32),
                   jax.ShapeDtypeStruct((N_EXPERTS, LANES), F32),
                   jax.ShapeDtypeStruct((n // MOE_TM * N_EXPERTS, LANES), F32)),
        grid=(n // tm,),
        in_specs=[
            pl.BlockSpec((tm, DIFF_WIDTH), row),
            pl.BlockSpec((tm, GMLP_WIDTH), row),
            pl.BlockSpec((tm, GMLP_WIDTH), row),
            pl.BlockSpec((tm, d), row),
            pl.BlockSpec(wsp.shape, const3),
            pl.BlockSpec(bsp.shape, const2),
            pl.BlockSpec(wo.shape, const2),
            pl.BlockSpec(g1.shape, const2),
            pl.BlockSpec(b1.shape, const2),
            pl.BlockSpec(wrt.shape, const2),
            pl.BlockSpec(br.shape, const2),
        ],
        out_specs=[
            pl.BlockSpec((tm, d), row),
            pl.BlockSpec((TOP_K, tm), col),
            pl.BlockSpec((TOP_K, tm), col),
            pl.BlockSpec((TOP_K, tm), col),
            pl.BlockSpec((N_EXPERTS, LANES), const2),
            pl.BlockSpec((MIX_SUB * N_EXPERTS, LANES), row),
        ],
        scratch_shapes=[pltpu.VMEM((tm, DIFF_WIDTH + GMLP_WIDTH), BF16),
                        pltpu.VMEM((N_EXPERTS, LANES), F32)],
        compiler_params=pltpu.CompilerParams(
            dimension_semantics=("arbitrary",), vmem_limit_bytes=VMEM_LIMIT),
        name="mix",
    )(attn, u, vn, x2, wsp, bsp, wo, g1, b1, wrt, br)


def _for_each_run(tile, tcnt_ref, off_ref, dst_ref, make_copy):
    def per_expert(e, carry):
        n = tcnt_ref[tile * N_EXPERTS + e]
        off = off_ref[tile * N_EXPERTS + e]
        dst = dst_ref[tile * N_EXPERTS + e]
        for bit in range(RUN_BITS):
            done = (n >> (bit + 1)) << (bit + 1)

            @pl.when(((n >> bit) & 1) == 1)
            def _():
                make_copy(off + done, dst + done, 1 << bit).start()
        return carry

    lax.fori_loop(0, N_EXPERTS, per_expert, 0)


def _dispatch_kernel(tcnt_ref, off_ref, dst_ref, pad_start_ref, pad_len_ref, nused_ref,
                     pos_ref, x1_ref, xs_hbm, sorted_ref, zero_ref, sem, zsem):
    tm = MOE_TM
    rows = TOP_K * tm
    d = x1_ref.shape[1]
    sub = d // LANES
    nblk = xs_hbm.shape[0] // (FFN_TM * sub)
    i = pl.program_id(0)
    last = pl.num_programs(0) - 1
    slot = lax.rem(i, 2)

    def wait_runs(s):
        pltpu.make_async_copy(sorted_ref.at[pl.ds(s * rows * sub, rows * sub)],
                              xs_hbm.at[pl.ds(0, rows * sub)], sem.at[s]).wait()

    @pl.when(i >= 2)
    def _():
        wait_runs(slot)

    pos = pos_ref[...]
    ri = lax.broadcasted_iota(jnp.int32, (rows, tm), 0)
    hit = (ri == pos[0:1]) | (ri == pos[1:2]) | (ri == pos[2:3]) | (ri == pos[3:4])
    perm = jnp.where(hit, 1.0, 0.0).astype(BF16)
    srt = jnp.dot(perm, x1_ref[...].astype(BF16), preferred_element_type=F32)
    base = pl.multiple_of(slot * (rows * sub), rows * sub)
    _rows_to_slabs(sorted_ref, base, srt, sub)

    def run_copy(local_row, global_row, nrows):
        return pltpu.make_async_copy(
            sorted_ref.at[pl.ds(pl.multiple_of((slot * rows + local_row) * sub, sub), nrows * sub)],
            xs_hbm.at[pl.ds(pl.multiple_of(global_row * sub, sub), nrows * sub)], sem.at[slot])

    _for_each_run(i, tcnt_ref, off_ref, dst_ref, run_copy)

    @pl.when(i == 0)
    def _():
        zero_ref[...] = jnp.zeros_like(zero_ref)

        def zero_copy(r, nrows):
            return pltpu.make_async_copy(
                zero_ref.at[pl.ds(0, nrows * sub)],
                xs_hbm.at[pl.ds(pl.multiple_of(r * sub, sub), nrows * sub)], zsem)

        def tail_copy(blk):
            return pltpu.make_async_copy(
                zero_ref, xs_hbm.at[pl.ds(pl.multiple_of(blk * (FFN_TM * sub), FFN_TM * sub),
                                          FFN_TM * sub)], zsem)

        def tail_start(blk, c):
            tail_copy(blk).start()
            return c

        def tail_wait(blk, c):
            tail_copy(blk).wait()
            return c

        lax.fori_loop(nused_ref[0], nblk, tail_start, 0)
        lax.fori_loop(nused_ref[0], nblk, tail_wait, 0)

        def per_expert(e, _):
            s = pad_start_ref[e]
            cnt = pad_len_ref[e]
            for wait in (False, True):
                for bit in range(FFN_TM.bit_length() - 1):
                    done = (cnt >> (bit + 1)) << (bit + 1)

                    @pl.when(((cnt >> bit) & 1) == 1)
                    def _():
                        piece = zero_copy(s + done, 1 << bit)
                        piece.wait() if wait else piece.start()
            return 0

        lax.fori_loop(0, N_EXPERTS, per_expert, 0)

    @pl.when(i == last)
    def _():
        wait_runs(slot)

        @pl.when(last >= 1)
        def _():
            wait_runs(1 - slot)


def _dispatch(x1, pos, tcnt, off, dst, pad_start, pad_len, nused, p_rows):
    n, d = x1.shape
    tm = MOE_TM
    sub = d // LANES
    return pl.pallas_call(
        _dispatch_kernel,
        out_shape=jax.ShapeDtypeStruct((p_rows * sub, LANES), F32),
        grid_spec=pltpu.PrefetchScalarGridSpec(
            num_scalar_prefetch=6,
            grid=(n // tm,),
            in_specs=[
                pl.BlockSpec((TOP_K, tm), lambda i, *_: (0, i)),
                pl.BlockSpec((tm, d), lambda i, *_: (i, 0)),
            ],
            out_specs=pl.BlockSpec(memory_space=pl.ANY),
            scratch_shapes=[pltpu.VMEM((2 * TOP_K * tm * sub, LANES), F32),
                            pltpu.VMEM((FFN_TM * sub, LANES), F32),
                            pltpu.SemaphoreType.DMA((2,)), pltpu.SemaphoreType.DMA(())],
        ),
        compiler_params=pltpu.CompilerParams(
            dimension_semantics=("arbitrary",), vmem_limit_bytes=VMEM_LIMIT),
        name="dispatch",
    )(tcnt, off, dst, pad_start, pad_len, nused, pos, x1)


def _ffn_kernel(blk_e_ref, nused_ref, next_e_ref, xs_ref, wup_hbm, bup_ref, wdn_hbm, bdn_ref, y_ref,
                wup_f, wdn_f, wup_b, wdn_b, wsem):
    tm = FFN_TM
    dff = wdn_f.shape[0]
    sub = wdn_f.shape[1] // LANES
    i = pl.program_id(0)
    used = i < nused_ref[0]

    @pl.when(jnp.logical_not(used))
    def _():
        y_ref[...] = jnp.zeros_like(y_ref)

    def weight_copies(e):
        return (pltpu.make_async_copy(wup_hbm.at[e], wup_f, wsem.at[0]),
                pltpu.make_async_copy(wdn_hbm.at[e], wdn_f, wsem.at[1]))

    e = blk_e_ref[i]
    new_expert = jnp.logical_or(i == 0, e != blk_e_ref[jnp.maximum(i - 1, 0)])

    @pl.when(jnp.logical_and(used, new_expert))
    def _():
        @pl.when(i == 0)
        def _():
            for cp in weight_copies(e):
                cp.start()

        for cp in weight_copies(e):
            cp.wait()
        for c in range(0, 2 * dff, WEIGHT_CAST_COLS):
            wup_b[:, c:c + WEIGHT_CAST_COLS] = wup_f[:, c:c + WEIGHT_CAST_COLS].astype(BF16)
        for c in range(0, wdn_f.shape[1], WEIGHT_CAST_COLS):
            wdn_b[:, c:c + WEIGHT_CAST_COLS] = wdn_f[:, c:c + WEIGHT_CAST_COLS].astype(BF16)

        @pl.when(next_e_ref[e] != e)
        def _():
            for cp in weight_copies(next_e_ref[e]):
                cp.start()

    @pl.when(used)
    def _():
        xb = _slabs_to_rows(xs_ref, 0, tm, sub).astype(BF16)
        h = jnp.dot(xb, wup_b[...], preferred_element_type=F32) + bup_ref[0]
        gate = jnp.minimum(h[:, :dff], SWIGLU_LIMIT)
        lin = jnp.clip(h[:, dff:], -SWIGLU_LIMIT, SWIGLU_LIMIT)
        act = (lin + 1.0) * gate * jax.nn.sigmoid(SWIGLU_ALPHA * gate)
        y = jnp.dot(act.astype(BF16), wdn_b[...], preferred_element_type=F32) + bdn_ref[0]
        _rows_to_slabs(y_ref, 0, y, sub)


def _ffn(xs, blk_e, nused, next_e, wup, bup, wdn, bdn):
    tm = FFN_TM
    e, d, dff2 = wup.shape
    dff = dff2 // 2
    sub = d // LANES
    rowblk = lambda i, be, nu, ne: (i, 0)
    bsel = lambda i, be, nu, ne: (be[i], 0, 0)
    return pl.pallas_call(
        _ffn_kernel,
        out_shape=jax.ShapeDtypeStruct(xs.shape, F32),
        grid_spec=pltpu.PrefetchScalarGridSpec(
            num_scalar_prefetch=3,
            grid=(xs.shape[0] // (tm * sub),),
            in_specs=[
                pl.BlockSpec((tm * sub, LANES), rowblk),
                pl.BlockSpec(memory_space=pl.ANY),
                pl.BlockSpec((1, 1, dff2), bsel),
                pl.BlockSpec(memory_space=pl.ANY),
                pl.BlockSpec((1, 1, d), bsel),
            ],
            out_specs=pl.BlockSpec((tm * sub, LANES), rowblk),
            scratch_shapes=[pltpu.VMEM((d, dff2), F32), pltpu.VMEM((dff, d), F32),
                            pltpu.VMEM((d, dff2), BF16), pltpu.VMEM((dff, d), BF16),
                            pltpu.SemaphoreType.DMA((2,))],
        ),
        compiler_params=pltpu.CompilerParams(
            dimension_semantics=("arbitrary",), vmem_limit_bytes=VMEM_LIMIT),
        name="ffn",
    )(blk_e, nused, next_e, xs, wup, bup, wdn, bdn)


def _combine_kernel(tcnt_ref, off_ref, dst_ref, pos_ref, gate_ref, x1_ref, g2_ref, b2_ref, y_hbm,
                    o_ref, stage_ref, sem):
    tm = MOE_TM
    rows = TOP_K * tm
    sub = x1_ref.shape[1] // LANES
    i = pl.program_id(0)
    nsteps = pl.num_programs(0)
    slot = lax.rem(i, 2)

    def fetch_runs(tile, s):
        def run_copy(local_row, global_row, nrows):
            return pltpu.make_async_copy(
                y_hbm.at[pl.ds(pl.multiple_of(global_row * sub, sub), nrows * sub)],
                stage_ref.at[pl.ds(pl.multiple_of((s * rows + local_row) * sub, sub), nrows * sub)],
                sem.at[s])
        _for_each_run(tile, tcnt_ref, off_ref, dst_ref, run_copy)

    @pl.when(i == 0)
    def _():
        fetch_runs(0, 0)

    @pl.when(i + 1 < nsteps)
    def _():
        fetch_runs(i + 1, 1 - slot)

    pltpu.make_async_copy(y_hbm.at[pl.ds(0, rows * sub)],
                          stage_ref.at[pl.ds(slot * rows * sub, rows * sub)], sem.at[slot]).wait()

    ys = _slabs_to_rows(stage_ref, pl.multiple_of(slot * (rows * sub), rows * sub), rows, sub)
    yh = ys.astype(BF16)
    yl = (ys - yh.astype(F32)).astype(BF16)
    pos = pos_ref[...]
    gates = gate_ref[...]
    ci = lax.broadcasted_iota(jnp.int32, (tm, rows), 1)
    w = jnp.where(ci == pos[:, 0:1], gates[:, 0:1], 0.0)
    for k in range(1, TOP_K):
        w = w + jnp.where(ci == pos[:, k:k + 1], gates[:, k:k + 1], 0.0)
    wh = w.astype(BF16)
    wl = (w - wh.astype(F32)).astype(BF16)
    ffn = (jnp.dot(wh, yh, preferred_element_type=F32) + jnp.dot(wl, yh, preferred_element_type=F32)
           + jnp.dot(wh, yl, preferred_element_type=F32))
    o_ref[...] = _layer_norm(DEEPNORM_ALPHA * x1_ref[...] + ffn, g2_ref[...], b2_ref[...])


def _combine(y, pos_tok, gates_tok, tcnt, off, dst, x1, g2, b2):
    n, d = x1.shape
    tm = MOE_TM
    sub = d // LANES
    row = lambda i, *_: (i, 0)
    const = lambda i, *_: (0, 0)
    return pl.pallas_call(
        _combine_kernel,
        out_shape=jax.ShapeDtypeStruct((n, d), F32),
        grid_spec=pltpu.PrefetchScalarGridSpec(
            num_scalar_prefetch=3,
            grid=(n // tm,),
            in_specs=[
                pl.BlockSpec((tm, TOP_K), row),
                pl.BlockSpec((tm, TOP_K), row),
                pl.BlockSpec((tm, d), row),
                pl.BlockSpec(g2.shape, const),
                pl.BlockSpec(b2.shape, const),
                pl.BlockSpec(memory_space=pl.ANY),
            ],
            out_specs=pl.BlockSpec((tm, d), row),
            scratch_shapes=[pltpu.VMEM((2 * TOP_K * tm * sub, LANES), F32),
                            pltpu.SemaphoreType.DMA((2,))],
        ),
        compiler_params=pltpu.CompilerParams(
            dimension_semantics=("arbitrary",), vmem_limit_bytes=VMEM_LIMIT),
        name="combine",
    )(tcnt, off, dst, pos_tok, gates_tok, x1, g2, b2, y)


def _head_pairs(w_a, w_b):
    d = w_a.shape[0]
    a = w_a.reshape(d, DIFF_HEADS, DIFF_HEAD_DIM)
    b = w_b.reshape(d, DIFF_HEADS, DIFF_HEAD_DIM)
    return jnp.concatenate([a, b], axis=-1).reshape(d, DIFF_HEADS * V_HEAD_DIM)


def _lookup(table, eid):
    eids = jnp.arange(N_EXPERTS, dtype=jnp.int32)[:, None, None]
    return jnp.sum(jnp.where(eid[None] == eids, table[:, None, :], 0), axis=0)


def kernel(x, w_in, lambda_q1, lambda_k1, lambda_q2, lambda_k2, subln_g, gmlp_ln_g, gmlp_ln_b,
           w_spatial, b_spatial, w_o, ln1_g, ln1_b, w_router, b_router, w_up, b_up,
           w_down, b_down, ln2_g, ln2_b):
    b, s, d = x.shape
    n = b * s
    x2 = x.reshape(n, d)

    w = w_in[0]
    c = QK_WIDTH
    scale = DIFF_HEAD_DIM ** -0.5 * LOG2_E
    wq = _head_pairs(w[:, 0:c], w[:, c:2 * c]) * scale
    wk = _head_pairs(w[:, 2 * c:3 * c], w[:, 3 * c:4 * c])
    wqk = jnp.concatenate([wq, wk], axis=1).astype(BF16)
    wvt = w[:, 4 * c:4 * c + DIFF_WIDTH].T.astype(BF16)
    wg = w[:, 4 * c + DIFF_WIDTH:].astype(BF16)
    lamv = jnp.concatenate([lambda_q1, lambda_k1, lambda_q2, lambda_k2], axis=0)
    bsp = jnp.repeat(b_spatial[0].T, GMLP_GROUP_DIM, axis=1)

    assert PROJ_TM == ATTN_T and s % ATTN_T == 0 and n % (MOE_TM * MIX_SUB) == 0
    qq, kk, vt, u, vn = _proj(x2, wqk, wvt, wg, gmlp_ln_g, gmlp_ln_b)
    attn = _attention(qq.reshape(b, s, -1), kk.reshape(b, s, -1),
                      vt.reshape(b, s // ATTN_T, DIFF_WIDTH, ATTN_T),
                      lamv, subln_g.reshape(V_HEAD_DIM, 1)).reshape(n, DIFF_WIDTH)
    x1, eid, gates, rank, cnt, tcnt_l = _mix(attn, u, vn, x2, w_spatial[0], bsp,
                                             w_o[0].astype(BF16), ln1_g, ln1_b,
                                             w_router[0].T, b_router[0][:, None])

    ntiles = n // MOE_TM
    counts = cnt[:, 0].astype(jnp.int32)
    padded = ((counts + FFN_TM - 1) // FFN_TM) * FFN_TM
    end_padded = jnp.cumsum(padded)
    start_padded = end_padded - padded
    tcnt = tcnt_l.reshape(ntiles, N_EXPERTS, LANES)[:, :, 0].astype(jnp.int32)
    before = jnp.cumsum(tcnt, axis=0) - tcnt
    off = jnp.cumsum(tcnt, axis=1) - tcnt
    dst = start_padded[None, :] + before
    tile_base = jnp.repeat((off - before).T, MOE_TM, axis=1)
    pos = _lookup(tile_base, eid) + rank

    p_rows = n * TOP_K + N_EXPERTS * FFN_TM
    nblk = p_rows // FFN_TM
    nused = (end_padded[-1:] // FFN_TM).astype(jnp.int32)
    blk_start = jnp.arange(nblk, dtype=jnp.int32) * FFN_TM
    blk_e = jnp.minimum(jnp.sum((end_padded[None, :] <= blk_start[:, None]).astype(jnp.int32), axis=1),
                        N_EXPERTS - 1)
    tcnt_f, off_f, dst_f = tcnt.reshape(-1), off.reshape(-1), dst.reshape(-1)

    xs = _dispatch(x1, pos, tcnt_f, off_f, dst_f, start_padded + counts, padded - counts, nused,
                   p_rows)
    e_ids = jnp.arange(N_EXPERTS, dtype=jnp.int32)
    later_used = jnp.logical_and(e_ids[None, :] > e_ids[:, None], padded[None, :] > 0)
    next_e = jnp.min(jnp.where(later_used, e_ids[None, :], N_EXPERTS), axis=1)
    next_e = jnp.where(next_e == N_EXPERTS, e_ids, next_e)
    y = _ffn(xs, blk_e, nused, next_e, w_up[0], b_up[0][:, None, :], w_down[0],
             b_down[0][:, None, :])
    out = _combine(y, pos.T, gates.T, tcnt_f, off_f, dst_f, x1, ln2_g, ln2_b)
    return out.reshape(b, s, d)
```

```python
import jax
import jax.numpy as jnp
from jax import lax
from jax.experimental import pallas as pl
from jax.experimental.pallas import tpu as pltpu

DIFF_HEADS = 4
DIFF_HEAD_DIM = 64
V_HEAD_DIM = 2 * DIFF_HEAD_DIM
QK_WIDTH = DIFF_HEADS * DIFF_HEAD_DIM
DIFF_WIDTH = DIFF_HEADS * V_HEAD_DIM
GMLP_GROUPS = 8
GMLP_GROUP_DIM = 64
GMLP_WIDTH = GMLP_GROUPS * GMLP_GROUP_DIM
CHUNK = 128
N_EXPERTS = 32
TOP_K = 4
SWIGLU_LIMIT = 7.0
SWIGLU_ALPHA = 1.702
LN_EPS = 1e-5
DEPTH = 1
DEEPNORM_ALPHA = (2.0 * DEPTH) ** 0.25
LAMBDA_INIT = 0.8 - 0.6 * 1.0
LOG2_E = 1.4426950408889634

LANES = 128

PROJ_TM = 512
ATTN_T = 512
MOE_TM = 256
MIX_SUB = 4
FFN_TM = 512
FFN_CHUNK = 256
WEIGHT_CAST_COLS = 256
RUN_BITS = MOE_TM.bit_length()

VMEM_LIMIT = 48 * 1024 * 1024

BF16 = jnp.bfloat16
F32 = jnp.float32


def _layer_norm(y, g, b):
    mu = jnp.mean(y, axis=-1, keepdims=True)
    yc = y - mu
    var = jnp.mean(yc * yc, axis=-1, keepdims=True)
    return yc * lax.rsqrt(var + LN_EPS) * g + b


def _gelu(x):
    return 0.5 * x * (1.0 + lax.erf(x * (2.0 ** -0.5)))


def _nt_dot(a, b):
    return lax.dot_general(a, b, (((1,), (1,)), ((), ())), preferred_element_type=F32)


def _slabs_to_rows(ref, first, nrows, sub):
    return jnp.concatenate([ref[pl.ds(first + j, nrows, stride=sub), :] for j in range(sub)], axis=1)


def _rows_to_slabs(ref, first, rows, sub):
    for j in range(sub):
        ref[pl.ds(first + j, rows.shape[0], stride=sub), :] = rows[:, j * LANES:(j + 1) * LANES]


def _proj_kernel(x_ref, wqk_ref, wvt_ref, wg_ref, lng_ref, lnb_ref,
                 qq_ref, kk_ref, vt_ref, u_ref, vn_ref):
    xb = x_ref[...].astype(BF16)
    qk = jnp.dot(xb, wqk_ref[...], preferred_element_type=F32)
    qq_ref[...] = qk[:, :DIFF_WIDTH].astype(BF16)
    kk_ref[...] = qk[:, DIFF_WIDTH:].astype(BF16)
    vt_ref[0] = _nt_dot(wvt_ref[...], xb).astype(BF16)
    g = jnp.dot(xb, wg_ref[...], preferred_element_type=F32)
    u_ref[...] = _gelu(g[:, :GMLP_WIDTH]).astype(BF16)
    gv = _gelu(g[:, GMLP_WIDTH:])
    vn_ref[...] = _layer_norm(gv, lng_ref[...], lnb_ref[...]).astype(BF16)


def _proj(x2, wqk, wvt, wg, lng, lnb):
    n, d = x2.shape
    tm = PROJ_TM
    row = lambda i: (i, 0)
    const = lambda i: (0, 0)
    out = jax.ShapeDtypeStruct((n, DIFF_WIDTH), BF16)
    vt_out = jax.ShapeDtypeStruct((n // tm, DIFF_WIDTH, tm), BF16)
    rowspec = pl.BlockSpec((tm, DIFF_WIDTH), row)
    return pl.pallas_call(
        _proj_kernel,
        out_shape=(out, out, vt_out, out, out),
        grid=(n // tm,),
        in_specs=[
            pl.BlockSpec((tm, d), row),
            pl.BlockSpec(wqk.shape, const),
            pl.BlockSpec(wvt.shape, const),
            pl.BlockSpec(wg.shape, const),
            pl.BlockSpec(lng.shape, const),
            pl.BlockSpec(lnb.shape, const),
        ],
        out_specs=[rowspec, rowspec, pl.BlockSpec((1, DIFF_WIDTH, tm), lambda i: (i, 0, 0)),
                   rowspec, rowspec],
        compiler_params=pltpu.CompilerParams(
            dimension_semantics=("arbitrary",), vmem_limit_bytes=VMEM_LIMIT),
        name="proj",
    )(x2, wqk, wvt, wg, lng, lnb)


def _attn_kernel(lamv_ref, g_ref, qq_ref, kk_ref, vt_ref, o_ref, s_ref):
    t = ATTN_T
    qi = pl.program_id(2)
    q = qq_ref[0]
    lane = lax.broadcasted_iota(jnp.int32, q.shape, 1)
    first = lane < DIFF_HEAD_DIM
    zero = jnp.zeros_like(q)
    q1 = jnp.where(first, q, zero)
    q2 = jnp.where(first, zero, q)

    def scores(j, buf):
        k = kk_ref[0, pl.ds(pl.multiple_of(j * t, t), t), :]
        s_ref[buf, 0] = _nt_dot(k, q1)
        s_ref[buf, 1] = _nt_dot(k, q2)

    def consume(j, buf, carry, masked):
        vt = vt_ref[0, j]
        new = []
        for mp, (m, l, acc) in enumerate(carry):
            s = s_ref[buf, mp]
            if masked:
                key = lax.broadcasted_iota(jnp.int32, s.shape, 0)
                qry = lax.broadcasted_iota(jnp.int32, s.shape, 1)
                s = jnp.where(key <= qry, s, -jnp.inf)
            m_new = jnp.maximum(m, jnp.max(s, axis=0, keepdims=True))
            alpha = jnp.exp2(m - m_new)
            p = jnp.exp2(s - m_new)
            l_new = alpha * l + jnp.sum(p, axis=0, keepdims=True)
            acc_new = alpha * acc + jnp.dot(vt, p.astype(BF16), preferred_element_type=F32)
            new.append((m_new, l_new, acc_new))
        return tuple(new)

    def init():
        return (jnp.full((1, t), -jnp.inf, F32), jnp.zeros((1, t), F32),
                jnp.zeros((V_HEAD_DIM, t), F32))

    def finish(carry):
        (_, l1, acc1), (_, l2, acc2) = carry
        lv = lamv_ref[...]
        lam = (jnp.exp(jnp.sum(lv[0:1] * lv[1:2], axis=-1, keepdims=True))
               - jnp.exp(jnp.sum(lv[2:3] * lv[3:4], axis=-1, keepdims=True)) + LAMBDA_INIT)
        o = acc1 / l1 - lam * (acc2 / l2)
        ms = jnp.mean(o * o, axis=0, keepdims=True)
        o = o * lax.rsqrt(ms + LN_EPS) * g_ref[...] * (1.0 - LAMBDA_INIT)
        o_ref[0] = o.T.astype(BF16)

    def pair(jj, c):
        j = 2 * jj
        scores(j + 1, 1)
        c = consume(j, 0, c, False)
        scores(j + 2, 0)
        return consume(j + 1, 1, c, False)

    scores(0, 0)
    carry = lax.fori_loop(0, qi // 2, pair, (init(), init()))

    @pl.when(qi % 2 == 0)
    def _():
        finish(consume(qi, 0, carry, True))

    @pl.when(qi % 2 == 1)
    def _():
        scores(qi, 1)
        finish(consume(qi, 1, consume(qi - 1, 0, carry, False), True))


def _attention(qq, kk, vt, lamv, subln_g_col):
    b, s, _ = qq.shape
    t = ATTN_T
    qspec = pl.BlockSpec((1, t, V_HEAD_DIM), lambda bi, h, qi: (bi, qi, h))
    kspec = pl.BlockSpec((1, s, V_HEAD_DIM), lambda bi, h, qi: (bi, 0, h))
    vtspec = pl.BlockSpec((1, s // t, V_HEAD_DIM, t), lambda bi, h, qi: (bi, 0, h, 0))
    const = lambda bi, h, qi: (0, 0)
    return pl.pallas_call(
        _attn_kernel,
        out_shape=jax.ShapeDtypeStruct((b, s, DIFF_WIDTH), BF16),
        grid=(b, DIFF_HEADS, s // t),
        in_specs=[pl.BlockSpec(lamv.shape, const), pl.BlockSpec(subln_g_col.shape, const),
                  qspec, kspec, vtspec],
        out_specs=qspec,
        scratch_shapes=[pltpu.VMEM((2, 2, t, t), F32)],
        compiler_params=pltpu.CompilerParams(
            dimension_semantics=("arbitrary",) * 3, vmem_limit_bytes=VMEM_LIMIT),
        name="attn",
    )(lamv, subln_g_col, qq, kk, vt)


def _mix_kernel(attn_ref, u_ref, vn_ref, x_ref, wsp_ref, bsp_ref, wo_ref, g1_ref, b1_ref,
                wrt_ref, br_ref,
                x1_ref, eid_ref, gate_ref, rank_ref, cnt_ref, tcnt_ref,
                cat_ref, carry_ref):
    tm = MOE_TM

    @pl.when(pl.program_id(0) == 0)
    def _():
        carry_ref[...] = jnp.zeros_like(carry_ref)

    ri = lax.broadcasted_iota(jnp.int32, (CHUNK, CHUNK), 0)
    ci = lax.broadcasted_iota(jnp.int32, (CHUNK, CHUNK), 1)
    tril = ci <= ri
    first = ci < GMLP_GROUP_DIM
    w = [jnp.where(tril, wsp_ref[g], 0.0).astype(BF16) for g in range(GMLP_GROUPS)]
    wr = wrt_ref[...]
    wh = wr.astype(BF16)
    wl = (wr - wh.astype(F32)).astype(BF16)
    eio = lax.broadcasted_iota(jnp.int32, (N_EXPERTS, tm), 0).astype(F32)
    ti = lax.broadcasted_iota(jnp.int32, (tm, tm), 0)
    tj = lax.broadcasted_iota(jnp.int32, (tm, tm), 1)
    before = jnp.where(ti < tj, 1.0, 0.0).astype(BF16)

    halves = [slice(h * tm, (h + 1) * tm) for h in range(MIX_SUB)]

    for rs in halves:
        cat_ref[rs, :DIFF_WIDTH] = attn_ref[rs, :]
    for c in range(MIX_SUB * tm // CHUNK):
        rows = slice(c * CHUNK, (c + 1) * CHUNK)
        for jb in range(GMLP_WIDTH // LANES):
            cols = slice(jb * LANES, (jb + 1) * LANES)
            vb = vn_ref[rows, cols]
            zero = jnp.zeros_like(vb)
            z = (jnp.dot(w[2 * jb], jnp.where(first, vb, zero), preferred_element_type=F32)
                 + jnp.dot(w[2 * jb + 1], jnp.where(first, zero, vb), preferred_element_type=F32))
            gated = u_ref[rows, cols].astype(F32) * (z + bsp_ref[:, cols])
            cat_ref[rows, DIFF_WIDTH + jb * LANES:DIFF_WIDTH + (jb + 1) * LANES] = gated.astype(BF16)

    x1s = []
    for rs in halves:
        mixed = jnp.dot(cat_ref[rs, :], wo_ref[...], preferred_element_type=F32)
        x1 = _layer_norm(DEEPNORM_ALPHA * x_ref[rs, :] + mixed, g1_ref[...], b1_ref[...])
        x1_ref[rs, :] = x1
        x1s.append(x1)

    curs = []
    for x1 in x1s:
        xh = x1.astype(BF16)
        xl = (x1 - xh.astype(F32)).astype(BF16)
        curs.append(_nt_dot(wh, xh) + _nt_dot(wl, xh) + _nt_dot(wh, xl) + br_ref[...])

    vals = [[] for _ in halves]
    idxs = [[] for _ in halves]
    sels = [[] for _ in halves]
    for _ in range(TOP_K):
        for h in range(MIX_SUB):
            mx = jnp.max(curs[h], axis=0, keepdims=True)
            idx = jnp.min(jnp.where(curs[h] == mx, eio, float(N_EXPERTS)), axis=0, keepdims=True)
            sel = eio == idx
            vals[h].append(mx)
            idxs[h].append(idx)
            sels[h].append(sel)
            curs[h] = jnp.where(sel, -jnp.inf, curs[h])

    carry = carry_ref[:, 0:1]
    for h, rs in enumerate(halves):
        ex = [jnp.exp(vk - vals[h][0]) for vk in vals[h]]
        denom = ex[0] + ex[1] + ex[2] + ex[3]
        gate_ref[:, rs] = jnp.concatenate([e / denom for e in ex], axis=0)
        eid_ref[:, rs] = jnp.concatenate(idxs[h], axis=0).astype(jnp.int32)

        chosen = (sels[h][0] | sels[h][1] | sels[h][2] | sels[h][3])
        onehot = jnp.where(chosen, 1.0, 0.0)
        cnt_before = jnp.dot(onehot.astype(BF16), before, preferred_element_type=F32) + carry
        ranks = [jnp.sum(jnp.where(s, cnt_before, 0.0), axis=0, keepdims=True) for s in sels[h]]
        rank_ref[:, rs] = jnp.concatenate(ranks, axis=0).astype(jnp.int32)
        tile_cnt = jnp.sum(onehot, axis=1, keepdims=True)
        tcnt_ref[h * N_EXPERTS:(h + 1) * N_EXPERTS, :] = jnp.broadcast_to(tile_cnt, (N_EXPERTS, LANES))
        carry = carry + tile_cnt

    carry_ref[...] = jnp.broadcast_to(carry, carry_ref.shape)
    cnt_ref[...] = jnp.broadcast_to(carry, cnt_ref.shape)


def _mix(attn, u, vn, x2, wsp, bsp, wo, g1, b1, wrt, br):
    n, d = x2.shape
    tm = MOE_TM * MIX_SUB
    row = lambda i: (i, 0)
    col = lambda i: (0, i)
    const2 = lambda i: (0, 0)
    const3 = lambda i: (0, 0, 0)
    tok = lambda dt: jax.ShapeDtypeStruct((TOP_K, n), dt)
    return pl.pallas_call(
        _mix_kernel,
        out_shape=(jax.ShapeDtypeStruct((n, d), F32), tok(jnp.int32), tok(F32), tok(jnp.int32),
                   jax.ShapeDtypeStruct((N_EXPERTS, LANES), F32),
                   jax.ShapeDtypeStruct((n // MOE_TM * N_EXPERTS, LANES), F32)),
        grid=(n // tm,),
        in_specs=[
            pl.BlockSpec((tm, DIFF_WIDTH), row),
            pl.BlockSpec((tm, GMLP_WIDTH), row),
            pl.BlockSpec((tm, GMLP_WIDTH), row),
            pl.BlockSpec((tm, d), row),
            pl.BlockSpec(wsp.shape, const3),
            pl.BlockSpec(bsp.shape, const2),
            pl.BlockSpec(wo.shape, const2),
            pl.BlockSpec(g1.shape, const2),
            pl.BlockSpec(b1.shape, const2),
            pl.BlockSpec(wrt.shape, const2),
            pl.BlockSpec(br.shape, const2),
        ],
        out_specs=[
            pl.BlockSpec((tm, d), row),
            pl.BlockSpec((TOP_K, tm), col),
            pl.BlockSpec((TOP_K, tm), col),
            pl.BlockSpec((TOP_K, tm), col),
            pl.BlockSpec((N_EXPERTS, LANES), const2),
            pl.BlockSpec((MIX_SUB * N_EXPERTS, LANES), row),
        ],
        scratch_shapes=[pltpu.VMEM((tm, DIFF_WIDTH + GMLP_WIDTH), BF16),
                        pltpu.VMEM((N_EXPERTS, LANES), F32)],
        compiler_params=pltpu.CompilerParams(
            dimension_semantics=("arbitrary",), vmem_limit_bytes=VMEM_LIMIT),
        name="mix",
    )(attn, u, vn, x2, wsp, bsp, wo, g1, b1, wrt, br)


def _for_each_run(tile, tcnt_ref, off_ref, dst_ref, make_copy):
    def per_expert(e, carry):
        n = tcnt_ref[tile * N_EXPERTS + e]
        off = off_ref[tile * N_EXPERTS + e]
        dst = dst_ref[tile * N_EXPERTS + e]
        for bit in range(RUN_BITS):
            done = (n >> (bit + 1)) << (bit + 1)

            @pl.when(((n >> bit) & 1) == 1)
            def _():
                make_copy(off + done, dst + done, 1 << bit).start()
        return carry

    lax.fori_loop(0, N_EXPERTS, per_expert, 0)


def _dispatch_kernel(tcnt_ref, off_ref, dst_ref, pad_start_ref, pad_len_ref, nused_ref,
                     pos_ref, x1_ref, xs_hbm, sorted_ref, zero_ref, sem, zsem):
    tm = MOE_TM
    rows = TOP_K * tm
    d = x1_ref.shape[1]
    sub = d // LANES
    nblk = xs_hbm.shape[0] // (FFN_TM * sub)
    i = pl.program_id(0)
    last = pl.num_programs(0) - 1
    slot = lax.rem(i, 2)

    def wait_runs(s):
        pltpu.make_async_copy(sorted_ref.at[pl.ds(s * rows * sub, rows * sub)],
                              xs_hbm.at[pl.ds(0, rows * sub)], sem.at[s]).wait()

    @pl.when(i >= 2)
    def _():
        wait_runs(slot)

    pos = pos_ref[...]
    ri = lax.broadcasted_iota(jnp.int32, (rows, tm), 0)
    hit = (ri == pos[0:1]) | (ri == pos[1:2]) | (ri == pos[2:3]) | (ri == pos[3:4])
    perm = jnp.where(hit, 1.0, 0.0).astype(BF16)
    srt = jnp.dot(perm, x1_ref[...].astype(BF16), preferred_element_type=F32)
    base = pl.multiple_of(slot * (rows * sub), rows * sub)
    _rows_to_slabs(sorted_ref, base, srt, sub)

    def run_copy(local_row, global_row, nrows):
        return pltpu.make_async_copy(
            sorted_ref.at[pl.ds(pl.multiple_of((slot * rows + local_row) * sub, sub), nrows * sub)],
            xs_hbm.at[pl.ds(pl.multiple_of(global_row * sub, sub), nrows * sub)], sem.at[slot])

    _for_each_run(i, tcnt_ref, off_ref, dst_ref, run_copy)

    @pl.when(i == 0)
    def _():
        zero_ref[...] = jnp.zeros_like(zero_ref)

        def zero_copy(r, nrows):
            return pltpu.make_async_copy(
                zero_ref.at[pl.ds(0, nrows * sub)],
                xs_hbm.at[pl.ds(pl.multiple_of(r * sub, sub), nrows * sub)], zsem)

        def tail_copy(blk):
            return pltpu.make_async_copy(
                zero_ref, xs_hbm.at[pl.ds(pl.multiple_of(blk * (FFN_TM * sub), FFN_TM * sub),
                                          FFN_TM * sub)], zsem)

        def tail_start(blk, c):
            tail_copy(blk).start()
            return c

        def tail_wait(blk, c):
            tail_copy(blk).wait()
            return c

        lax.fori_loop(nused_ref[0], nblk, tail_start, 0)
        lax.fori_loop(nused_ref[0], nblk, tail_wait, 0)

        def per_expert(e, _):
            s = pad_start_ref[e]
            cnt = pad_len_ref[e]
            for wait in (False, True):
                for bit in range(FFN_TM.bit_length() - 1):
                    done = (cnt >> (bit + 1)) << (bit + 1)

                    @pl.when(((cnt >> bit) & 1) == 1)
                    def _():
                        piece = zero_copy(s + done, 1 << bit)
                        piece.wait() if wait else piece.start()
            return 0

        lax.fori_loop(0, N_EXPERTS, per_expert, 0)

    @pl.when(i == last)
    def _():
        wait_runs(slot)

        @pl.when(last >= 1)
        def _():
            wait_runs(1 - slot)


def _dispatch(x1, pos, tcnt, off, dst, pad_start, pad_len, nused, p_rows):
    n, d = x1.shape
    tm = MOE_TM
    sub = d // LANES
    return pl.pallas_call(
        _dispatch_kernel,
        out_shape=jax.ShapeDtypeStruct((p_rows * sub, LANES), F32),
        grid_spec=pltpu.PrefetchScalarGridSpec(
            num_scalar_prefetch=6,
            grid=(n // tm,),
            in_specs=[
                pl.BlockSpec((TOP_K, tm), lambda i, *_: (0, i)),
                pl.BlockSpec((tm, d), lambda i, *_: (i, 0)),
            ],
            out_specs=pl.BlockSpec(memory_space=pl.ANY),
            scratch_shapes=[pltpu.VMEM((2 * TOP_K * tm * sub, LANES), F32),
                            pltpu.VMEM((FFN_TM * sub, LANES), F32),
                            pltpu.SemaphoreType.DMA((2,)), pltpu.SemaphoreType.DMA(())],
        ),
        compiler_params=pltpu.CompilerParams(
            dimension_semantics=("arbitrary",), vmem_limit_bytes=VMEM_LIMIT),
        name="dispatch",
    )(tcnt, off, dst, pad_start, pad_len, nused, pos, x1)


def _ffn_kernel(blk_e_ref, nused_ref, next_e_ref, xs_ref, wup_hbm, bup_ref, wdn_hbm, bdn_ref, y_ref,
                wup_f, wdn_f, wup_b, wdn_b, act_ref, wsem):
    tm = FFN_TM
    dff = wdn_f.shape[0]
    sub = wdn_f.shape[1] // LANES
    i = pl.program_id(0)
    used = i < nused_ref[0]

    @pl.when(jnp.logical_not(used))
    def _():
        y_ref[...] = jnp.zeros_like(y_ref)

    def weight_copies(e):
        return (pltpu.make_async_copy(wup_hbm.at[e], wup_f, wsem.at[0]),
                pltpu.make_async_copy(wdn_hbm.at[e], wdn_f, wsem.at[1]))

    e = blk_e_ref[i]
    new_expert = jnp.logical_or(i == 0, e != blk_e_ref[jnp.maximum(i - 1, 0)])

    @pl.when(jnp.logical_and(used, new_expert))
    def _():
        @pl.when(i == 0)
        def _():
            for cp in weight_copies(e):
                cp.start()

        for cp in weight_copies(e):
            cp.wait()
        for c in range(0, 2 * dff, WEIGHT_CAST_COLS):
            wup_b[:, c:c + WEIGHT_CAST_COLS] = wup_f[:, c:c + WEIGHT_CAST_COLS].astype(BF16)
        for c in range(0, wdn_f.shape[1], WEIGHT_CAST_COLS):
            wdn_b[:, c:c + WEIGHT_CAST_COLS] = wdn_f[:, c:c + WEIGHT_CAST_COLS].astype(BF16)

        @pl.when(next_e_ref[e] != e)
        def _():
            for cp in weight_copies(next_e_ref[e]):
                cp.start()

    @pl.when(used)
    def _():
        xb = _slabs_to_rows(xs_ref, 0, tm, sub).astype(BF16)
        for c in range(0, dff, FFN_CHUNK):
            hg = (jnp.dot(xb, wup_b[:, c:c + FFN_CHUNK], preferred_element_type=F32)
                  + bup_ref[0, :, c:c + FFN_CHUNK])
            hl = (jnp.dot(xb, wup_b[:, dff + c:dff + c + FFN_CHUNK], preferred_element_type=F32)
                  + bup_ref[0, :, dff + c:dff + c + FFN_CHUNK])
            gate = jnp.minimum(hg, SWIGLU_LIMIT)
            lin = jnp.clip(hl, -SWIGLU_LIMIT, SWIGLU_LIMIT)
            act = (lin + 1.0) * gate * jax.nn.sigmoid(SWIGLU_ALPHA * gate)
            act_ref[:, c:c + FFN_CHUNK] = act.astype(BF16)
        for c in range(0, wdn_f.shape[1], FFN_CHUNK):
            y = (jnp.dot(act_ref[...], wdn_b[:, c:c + FFN_CHUNK], preferred_element_type=F32)
                 + bdn_ref[0, :, c:c + FFN_CHUNK])
            for j in range(FFN_CHUNK // LANES):
                y_ref[pl.ds(c // LANES + j, tm, stride=sub), :] = y[:, j * LANES:(j + 1) * LANES]


def _ffn(xs, blk_e, nused, next_e, wup, bup, wdn, bdn):
    tm = FFN_TM
    e, d, dff2 = wup.shape
    dff = dff2 // 2
    sub = d // LANES
    rowblk = lambda i, be, nu, ne: (i, 0)
    bsel = lambda i, be, nu, ne: (be[i], 0, 0)
    return pl.pallas_call(
        _ffn_kernel,
        out_shape=jax.ShapeDtypeStruct(xs.shape, F32),
        grid_spec=pltpu.PrefetchScalarGridSpec(
            num_scalar_prefetch=3,
            grid=(xs.shape[0] // (tm * sub),),
            in_specs=[
                pl.BlockSpec((tm * sub, LANES), rowblk),
                pl.BlockSpec(memory_space=pl.ANY),
                pl.BlockSpec((1, 1, dff2), bsel),
                pl.BlockSpec(memory_space=pl.ANY),
                pl.BlockSpec((1, 1, d), bsel),
            ],
            out_specs=pl.BlockSpec((tm * sub, LANES), rowblk),
            scratch_shapes=[pltpu.VMEM((d, dff2), F32), pltpu.VMEM((dff, d), F32),
                            pltpu.VMEM((d, dff2), BF16), pltpu.VMEM((dff, d), BF16),
                            pltpu.VMEM((tm, dff), BF16),
                            pltpu.SemaphoreType.DMA((2,))],
        ),
        compiler_params=pltpu.CompilerParams(
            dimension_semantics=("arbitrary",), vmem_limit_bytes=VMEM_LIMIT),
        name="ffn",
    )(blk_e, nused, next_e, xs, wup, bup, wdn, bdn)


def _combine_kernel(tcnt_ref, off_ref, dst_ref, pos_ref, gate_ref, x1_ref, g2_ref, b2_ref, y_hbm,
                    o_ref, stage_ref, sem):
    tm = MOE_TM
    rows = TOP_K * tm
    sub = x1_ref.shape[1] // LANES
    i = pl.program_id(0)
    nsteps = pl.num_programs(0)
    slot = lax.rem(i, 2)

    def fetch_runs(tile, s):
        def run_copy(local_row, global_row, nrows):
            return pltpu.make_async_copy(
                y_hbm.at[pl.ds(pl.multiple_of(global_row * sub, sub), nrows * sub)],
                stage_ref.at[pl.ds(pl.multiple_of((s * rows + local_row) * sub, sub), nrows * sub)],
                sem.at[s])
        _for_each_run(tile, tcnt_ref, off_ref, dst_ref, run_copy)

    @pl.when(i == 0)
    def _():
        fetch_runs(0, 0)

    @pl.when(i + 1 < nsteps)
    def _():
        fetch_runs(i + 1, 1 - slot)

    pltpu.make_async_copy(y_hbm.at[pl.ds(0, rows * sub)],
                          stage_ref.at[pl.ds(slot * rows * sub, rows * sub)], sem.at[slot]).wait()

    ys = _slabs_to_rows(stage_ref, pl.multiple_of(slot * (rows * sub), rows * sub), rows, sub)
    yh = ys.astype(BF16)
    yl = (ys - yh.astype(F32)).astype(BF16)
    pos = pos_ref[...]
    gates = gate_ref[...]
    ci = lax.broadcasted_iota(jnp.int32, (tm, rows), 1)
    w = jnp.where(ci == pos[:, 0:1], gates[:, 0:1], 0.0)
    for k in range(1, TOP_K):
        w = w + jnp.where(ci == pos[:, k:k + 1], gates[:, k:k + 1], 0.0)
    wh = w.astype(BF16)
    wl = (w - wh.astype(F32)).astype(BF16)
    ffn = (jnp.dot(wh, yh, preferred_element_type=F32) + jnp.dot(wl, yh, preferred_element_type=F32)
           + jnp.dot(wh, yl, preferred_element_type=F32))
    o_ref[...] = _layer_norm(DEEPNORM_ALPHA * x1_ref[...] + ffn, g2_ref[...], b2_ref[...])


def _combine(y, pos_tok, gates_tok, tcnt, off, dst, x1, g2, b2):
    n, d = x1.shape
    tm = MOE_TM
    sub = d // LANES
    row = lambda i, *_: (i, 0)
    const = lambda i, *_: (0, 0)
    return pl.pallas_call(
        _combine_kernel,
        out_shape=jax.ShapeDtypeStruct((n, d), F32),
        grid_spec=pltpu.PrefetchScalarGridSpec(
            num_scalar_prefetch=3,
            grid=(n // tm,),
            in_specs=[
                pl.BlockSpec((tm, TOP_K), row),
                pl.BlockSpec((tm, TOP_K), row),
                pl.BlockSpec((tm, d), row),
                pl.BlockSpec(g2.shape, const),
                pl.BlockSpec(b2.shape, const),
                pl.BlockSpec(memory_space=pl.ANY),
            ],
            out_specs=pl.BlockSpec((tm, d), row),
            scratch_shapes=[pltpu.VMEM((2 * TOP_K * tm * sub, LANES), F32),
                            pltpu.SemaphoreType.DMA((2,))],
        ),
        compiler_params=pltpu.CompilerParams(
            dimension_semantics=("arbitrary",), vmem_limit_bytes=VMEM_LIMIT),
        name="combine",
    )(tcnt, off, dst, pos_tok, gates_tok, x1, g2, b2, y)


def _head_pairs(w_a, w_b):
    d = w_a.shape[0]
    a = w_a.reshape(d, DIFF_HEADS, DIFF_HEAD_DIM)
    b = w_b.reshape(d, DIFF_HEADS, DIFF_HEAD_DIM)
    return jnp.concatenate([a, b], axis=-1).reshape(d, DIFF_HEADS * V_HEAD_DIM)


def _lookup(table, eid):
    eids = jnp.arange(N_EXPERTS, dtype=jnp.int32)[:, None, None]
    return jnp.sum(jnp.where(eid[None] == eids, table[:, None, :], 0), axis=0)


def kernel(x, w_in, lambda_q1, lambda_k1, lambda_q2, lambda_k2, subln_g, gmlp_ln_g, gmlp_ln_b,
           w_spatial, b_spatial, w_o, ln1_g, ln1_b, w_router, b_router, w_up, b_up,
           w_down, b_down, ln2_g, ln2_b):
    b, s, d = x.shape
    n = b * s
    x2 = x.reshape(n, d)

    w = w_in[0]
    c = QK_WIDTH
    scale = DIFF_HEAD_DIM ** -0.5 * LOG2_E
    wq = _head_pairs(w[:, 0:c], w[:, c:2 * c]) * scale
    wk = _head_pairs(w[:, 2 * c:3 * c], w[:, 3 * c:4 * c])
    wqk = jnp.concatenate([wq, wk], axis=1).astype(BF16)
    wvt = w[:, 4 * c:4 * c + DIFF_WIDTH].T.astype(BF16)
    wg = w[:, 4 * c + DIFF_WIDTH:].astype(BF16)
    lamv = jnp.concatenate([lambda_q1, lambda_k1, lambda_q2, lambda_k2], axis=0)
    bsp = jnp.repeat(b_spatial[0].T, GMLP_GROUP_DIM, axis=1)

    assert PROJ_TM == ATTN_T and s % ATTN_T == 0 and n % (MOE_TM * MIX_SUB) == 0
    qq, kk, vt, u, vn = _proj(x2, wqk, wvt, wg, gmlp_ln_g, gmlp_ln_b)
    attn = _attention(qq.reshape(b, s, -1), kk.reshape(b, s, -1),
                      vt.reshape(b, s // ATTN_T, DIFF_WIDTH, ATTN_T),
                      lamv, subln_g.reshape(V_HEAD_DIM, 1)).reshape(n, DIFF_WIDTH)
    x1, eid, gates, rank, cnt, tcnt_l = _mix(attn, u, vn, x2, w_spatial[0], bsp,
                                             w_o[0].astype(BF16), ln1_g, ln1_b,
                                             w_router[0].T, b_router[0][:, None])

    ntiles = n // MOE_TM
    counts = cnt[:, 0].astype(jnp.int32)
    padded = ((counts + FFN_TM - 1) // FFN_TM) * FFN_TM
    end_padded = jnp.cumsum(padded)
    start_padded = end_padded - padded
    tcnt = tcnt_l.reshape(ntiles, N_EXPERTS, LANES)[:, :, 0].astype(jnp.int32)
    before = jnp.cumsum(tcnt, axis=0) - tcnt
    off = jnp.cumsum(tcnt, axis=1) - tcnt
    dst = start_padded[None, :] + before
    tile_base = jnp.repeat((off - before).T, MOE_TM, axis=1)
    pos = _lookup(tile_base, eid) + rank

    p_rows = n * TOP_K + N_EXPERTS * FFN_TM
    nblk = p_rows // FFN_TM
    nused = (end_padded[-1:] // FFN_TM).astype(jnp.int32)
    blk_start = jnp.arange(nblk, dtype=jnp.int32) * FFN_TM
    blk_e = jnp.minimum(jnp.sum((end_padded[None, :] <= blk_start[:, None]).astype(jnp.int32), axis=1),
                        N_EXPERTS - 1)
    tcnt_f, off_f, dst_f = tcnt.reshape(-1), off.reshape(-1), dst.reshape(-1)

    xs = _dispatch(x1, pos, tcnt_f, off_f, dst_f, start_padded + counts, padded - counts, nused,
                   p_rows)
    e_ids = jnp.arange(N_EXPERTS, dtype=jnp.int32)
    later_used = jnp.logical_and(e_ids[None, :] > e_ids[:, None], padded[None, :] > 0)
    next_e = jnp.min(jnp.where(later_used, e_ids[None, :], N_EXPERTS), axis=1)
    next_e = jnp.where(next_e == N_EXPERTS, e_ids, next_e)
    y = _ffn(xs, blk_e, nused, next_e, w_up[0], b_up[0][:, None, :], w_down[0],
             b_down[0][:, None, :])
    out = _combine(y, pos.T, gates.T, tcnt_f, off_f, dst_f, x1, ln2_g, ln2_b)
    return out.reshape(b, s, d)
```

```python
import jax
import jax.numpy as jnp
from jax import lax
from jax.experimental import pallas as pl
from jax.experimental.pallas import tpu as pltpu

DIFF_HEADS = 4
DIFF_HEAD_DIM = 64
V_HEAD_DIM = 2 * DIFF_HEAD_DIM
QK_WIDTH = DIFF_HEADS * DIFF_HEAD_DIM
DIFF_WIDTH = DIFF_HEADS * V_HEAD_DIM
GMLP_GROUPS = 8
GMLP_GROUP_DIM = 64
GMLP_WIDTH = GMLP_GROUPS * GMLP_GROUP_DIM
CHUNK = 128
N_EXPERTS = 32
TOP_K = 4
SWIGLU_LIMIT = 7.0
SWIGLU_ALPHA = 1.702
LN_EPS = 1e-5
DEPTH = 1
DEEPNORM_ALPHA = (2.0 * DEPTH) ** 0.25
LAMBDA_INIT = 0.8 - 0.6 * 1.0
LOG2_E = 1.4426950408889634

LANES = 128

PROJ_TM = 512
ATTN_T = 512
MOE_TM = 256
MIX_SUB = 4
COMBINE_SUB = 2
FFN_TM = 512
WEIGHT_CAST_COLS = 256
RUN_BITS = MOE_TM.bit_length()

VMEM_LIMIT = 48 * 1024 * 1024

BF16 = jnp.bfloat16
F32 = jnp.float32


def _layer_norm(y, g, b):
    mu = jnp.mean(y, axis=-1, keepdims=True)
    yc = y - mu
    var = jnp.mean(yc * yc, axis=-1, keepdims=True)
    return yc * lax.rsqrt(var + LN_EPS) * g + b


def _gelu(x):
    return 0.5 * x * (1.0 + lax.erf(x * (2.0 ** -0.5)))


def _nt_dot(a, b):
    return lax.dot_general(a, b, (((1,), (1,)), ((), ())), preferred_element_type=F32)


def _slabs_to_rows(ref, first, nrows, sub):
    return jnp.concatenate([ref[pl.ds(first + j, nrows, stride=sub), :] for j in range(sub)], axis=1)


def _rows_to_slabs(ref, first, rows, sub):
    for j in range(sub):
        ref[pl.ds(first + j, rows.shape[0], stride=sub), :] = rows[:, j * LANES:(j + 1) * LANES]


def _proj_kernel(x_ref, wqk_ref, wvt_ref, wg_ref, lng_ref, lnb_ref,
                 qq_ref, kk_ref, vt_ref, u_ref, vn_ref):
    xb = x_ref[...].astype(BF16)
    qk = jnp.dot(xb, wqk_ref[...], preferred_element_type=F32)
    qq_ref[...] = qk[:, :DIFF_WIDTH].astype(BF16)
    kk_ref[...] = qk[:, DIFF_WIDTH:].astype(BF16)
    vt_ref[0] = _nt_dot(wvt_ref[...], xb).astype(BF16)
    g = jnp.dot(xb, wg_ref[...], preferred_element_type=F32)
    u_ref[...] = _gelu(g[:, :GMLP_WIDTH]).astype(BF16)
    gv = _gelu(g[:, GMLP_WIDTH:])
    vn_ref[...] = _layer_norm(gv, lng_ref[...], lnb_ref[...]).astype(BF16)


def _proj(x2, wqk, wvt, wg, lng, lnb):
    n, d = x2.shape
    tm = PROJ_TM
    row = lambda i: (i, 0)
    const = lambda i: (0, 0)
    out = jax.ShapeDtypeStruct((n, DIFF_WIDTH), BF16)
    vt_out = jax.ShapeDtypeStruct((n // tm, DIFF_WIDTH, tm), BF16)
    rowspec = pl.BlockSpec((tm, DIFF_WIDTH), row)
    return pl.pallas_call(
        _proj_kernel,
        out_shape=(out, out, vt_out, out, out),
        grid=(n // tm,),
        in_specs=[
            pl.BlockSpec((tm, d), row),
            pl.BlockSpec(wqk.shape, const),
            pl.BlockSpec(wvt.shape, const),
            pl.BlockSpec(wg.shape, const),
            pl.BlockSpec(lng.shape, const),
            pl.BlockSpec(lnb.shape, const),
        ],
        out_specs=[rowspec, rowspec, pl.BlockSpec((1, DIFF_WIDTH, tm), lambda i: (i, 0, 0)),
                   rowspec, rowspec],
        compiler_params=pltpu.CompilerParams(
            dimension_semantics=("arbitrary",), vmem_limit_bytes=VMEM_LIMIT),
        name="proj",
    )(x2, wqk, wvt, wg, lng, lnb)


def _attn_kernel(lamv_ref, g_ref, qq_ref, kk_ref, vt_ref, o_ref, s_ref):
    t = ATTN_T
    qi = pl.program_id(2)
    q = qq_ref[0]
    lane = lax.broadcasted_iota(jnp.int32, q.shape, 1)
    first = lane < DIFF_HEAD_DIM
    zero = jnp.zeros_like(q)
    q1 = jnp.where(first, q, zero)
    q2 = jnp.where(first, zero, q)

    def scores(j, buf):
        k = kk_ref[0, pl.ds(pl.multiple_of(j * t, t), t), :]
        s_ref[buf, 0] = _nt_dot(k, q1)
        s_ref[buf, 1] = _nt_dot(k, q2)

    def consume(j, buf, carry, masked):
        vt = vt_ref[0, j]
        new = []
        for mp, (m, l, acc) in enumerate(carry):
            s = s_ref[buf, mp]
            if masked:
                key = lax.broadcasted_iota(jnp.int32, s.shape, 0)
                qry = lax.broadcasted_iota(jnp.int32, s.shape, 1)
                s = jnp.where(key <= qry, s, -jnp.inf)
            m_new = jnp.maximum(m, jnp.max(s, axis=0, keepdims=True))
            alpha = jnp.exp2(m - m_new)
            p = jnp.exp2(s - m_new)
            l_new = alpha * l + jnp.sum(p, axis=0, keepdims=True)
            acc_new = alpha * acc + jnp.dot(vt, p.astype(BF16), preferred_element_type=F32)
            new.append((m_new, l_new, acc_new))
        return tuple(new)

    def init():
        return (jnp.full((1, t), -jnp.inf, F32), jnp.zeros((1, t), F32),
                jnp.zeros((V_HEAD_DIM, t), F32))

    def finish(carry):
        (_, l1, acc1), (_, l2, acc2) = carry
        lv = lamv_ref[...]
        lam = (jnp.exp(jnp.sum(lv[0:1] * lv[1:2], axis=-1, keepdims=True))
               - jnp.exp(jnp.sum(lv[2:3] * lv[3:4], axis=-1, keepdims=True)) + LAMBDA_INIT)
        o = acc1 / l1 - lam * (acc2 / l2)
        ms = jnp.mean(o * o, axis=0, keepdims=True)
        o = o * lax.rsqrt(ms + LN_EPS) * g_ref[...] * (1.0 - LAMBDA_INIT)
        o_ref[0] = o.T.astype(BF16)

    def pair(jj, c):
        j = 2 * jj
        scores(j + 1, 1)
        c = consume(j, 0, c, False)
        scores(j + 2, 0)
        return consume(j + 1, 1, c, False)

    scores(0, 0)
    carry = lax.fori_loop(0, qi // 2, pair, (init(), init()))

    @pl.when(qi % 2 == 0)
    def _():
        finish(consume(qi, 0, carry, True))

    @pl.when(qi % 2 == 1)
    def _():
        scores(qi, 1)
        finish(consume(qi, 1, consume(qi - 1, 0, carry, False), True))


def _attention(qq, kk, vt, lamv, subln_g_col):
    b, s, _ = qq.shape
    t = ATTN_T
    qspec = pl.BlockSpec((1, t, V_HEAD_DIM), lambda bi, h, qi: (bi, qi, h))
    kspec = pl.BlockSpec((1, s, V_HEAD_DIM), lambda bi, h, qi: (bi, 0, h))
    vtspec = pl.BlockSpec((1, s // t, V_HEAD_DIM, t), lambda bi, h, qi: (bi, 0, h, 0))
    const = lambda bi, h, qi: (0, 0)
    return pl.pallas_call(
        _attn_kernel,
        out_shape=jax.ShapeDtypeStruct((b, s, DIFF_WIDTH), BF16),
        grid=(b, DIFF_HEADS, s // t),
        in_specs=[pl.BlockSpec(lamv.shape, const), pl.BlockSpec(subln_g_col.shape, const),
                  qspec, kspec, vtspec],
        out_specs=qspec,
        scratch_shapes=[pltpu.VMEM((2, 2, t, t), F32)],
        compiler_params=pltpu.CompilerParams(
            dimension_semantics=("arbitrary",) * 3, vmem_limit_bytes=VMEM_LIMIT),
        name="attn",
    )(lamv, subln_g_col, qq, kk, vt)


def _mix_kernel(attn_ref, u_ref, vn_ref, x_ref, wsp_ref, bsp_ref, wo_ref, g1_ref, b1_ref,
                wrt_ref, br_ref,
                x1_ref, eid_ref, gate_ref, rank_ref, cnt_ref, tcnt_ref,
                cat_ref, carry_ref):
    tm = MOE_TM

    @pl.when(pl.program_id(0) == 0)
    def _():
        carry_ref[...] = jnp.zeros_like(carry_ref)

    ri = lax.broadcasted_iota(jnp.int32, (CHUNK, CHUNK), 0)
    ci = lax.broadcasted_iota(jnp.int32, (CHUNK, CHUNK), 1)
    tril = ci <= ri
    first = ci < GMLP_GROUP_DIM
    w = [jnp.where(tril, wsp_ref[g], 0.0).astype(BF16) for g in range(GMLP_GROUPS)]
    wr = wrt_ref[...]
    wh = wr.astype(BF16)
    wl = (wr - wh.astype(F32)).astype(BF16)
    eio = lax.broadcasted_iota(jnp.int32, (N_EXPERTS, tm), 0).astype(F32)
    ti = lax.broadcasted_iota(jnp.int32, (tm, tm), 0)
    tj = lax.broadcasted_iota(jnp.int32, (tm, tm), 1)
    before = jnp.where(ti < tj, 1.0, 0.0).astype(BF16)

    halves = [slice(h * tm, (h + 1) * tm) for h in range(MIX_SUB)]

    for rs in halves:
        cat_ref[rs, :DIFF_WIDTH] = attn_ref[rs, :]
    for c in range(MIX_SUB * tm // CHUNK):
        rows = slice(c * CHUNK, (c + 1) * CHUNK)
        for jb in range(GMLP_WIDTH // LANES):
            cols = slice(jb * LANES, (jb + 1) * LANES)
            vb = vn_ref[rows, cols]
            zero = jnp.zeros_like(vb)
            z = (jnp.dot(w[2 * jb], jnp.where(first, vb, zero), preferred_element_type=F32)
                 + jnp.dot(w[2 * jb + 1], jnp.where(first, zero, vb), preferred_element_type=F32))
            gated = u_ref[rows, cols].astype(F32) * (z + bsp_ref[:, cols])
            cat_ref[rows, DIFF_WIDTH + jb * LANES:DIFF_WIDTH + (jb + 1) * LANES] = gated.astype(BF16)

    x1s = []
    for rs in halves:
        mixed = jnp.dot(cat_ref[rs, :], wo_ref[...], preferred_element_type=F32)
        x1 = _layer_norm(DEEPNORM_ALPHA * x_ref[rs, :] + mixed, g1_ref[...], b1_ref[...])
        x1_ref[rs, :] = x1
        x1s.append(x1)

    curs = []
    for x1 in x1s:
        xh = x1.astype(BF16)
        xl = (x1 - xh.astype(F32)).astype(BF16)
        curs.append(_nt_dot(wh, xh) + _nt_dot(wl, xh) + _nt_dot(wh, xl) + br_ref[...])

    vals = [[] for _ in halves]
    idxs = [[] for _ in halves]
    sels = [[] for _ in halves]
    for _ in range(TOP_K):
        for h in range(MIX_SUB):
            mx = jnp.max(curs[h], axis=0, keepdims=True)
            idx = jnp.min(jnp.where(curs[h] == mx, eio, float(N_EXPERTS)), axis=0, keepdims=True)
            sel = eio == idx
            vals[h].append(mx)
            idxs[h].append(idx)
            sels[h].append(sel)
            curs[h] = jnp.where(sel, -jnp.inf, curs[h])

    carry = carry_ref[:, 0:1]
    for h, rs in enumerate(halves):
        ex = [jnp.exp(vk - vals[h][0]) for vk in vals[h]]
        denom = ex[0] + ex[1] + ex[2] + ex[3]
        gate_ref[:, rs] = jnp.concatenate([e / denom for e in ex], axis=0)
        eid_ref[:, rs] = jnp.concatenate(idxs[h], axis=0).astype(jnp.int32)

        chosen = (sels[h][0] | sels[h][1] | sels[h][2] | sels[h][3])
        onehot = jnp.where(chosen, 1.0, 0.0)
        cnt_before = jnp.dot(onehot.astype(BF16), before, preferred_element_type=F32) + carry
        ranks = [jnp.sum(jnp.where(s, cnt_before, 0.0), axis=0, keepdims=True) for s in sels[h]]
        rank_ref[:, rs] = jnp.concatenate(ranks, axis=0).astype(jnp.int32)
        tile_cnt = jnp.sum(onehot, axis=1, keepdims=True)
        tcnt_ref[h * N_EXPERTS:(h + 1) * N_EXPERTS, :] = jnp.broadcast_to(tile_cnt, (N_EXPERTS, LANES))
        carry = carry + tile_cnt

    carry_ref[...] = jnp.broadcast_to(carry, carry_ref.shape)
    cnt_ref[...] = jnp.broadcast_to(carry, cnt_ref.shape)


def _mix(attn, u, vn, x2, wsp, bsp, wo, g1, b1, wrt, br):
    n, d = x2.shape
    tm = MOE_TM * MIX_SUB
    row = lambda i: (i, 0)
    col = lambda i: (0, i)
    const2 = lambda i: (0, 0)
    const3 = lambda i: (0, 0, 0)
    tok = lambda dt: jax.ShapeDtypeStruct((TOP_K, n), dt)
    return pl.pallas_call(
        _mix_kernel,
        out_shape=(jax.ShapeDtypeStruct((n, d), F32), tok(jnp.int32), tok(F32), tok(jnp.int32),
                   jax.ShapeDtypeStruct((N_EXPERTS, LANES), F32),
                   jax.ShapeDtypeStruct((n // MOE_TM * N_EXPERTS, LANES), F32)),
        grid=(n // tm,),
        in_specs=[
            pl.BlockSpec((tm, DIFF_WIDTH), row),
            pl.BlockSpec((tm, GMLP_WIDTH), row),
            pl.BlockSpec((tm, GMLP_WIDTH), row),
            pl.BlockSpec((tm, d), row),
            pl.BlockSpec(wsp.shape, const3),
            pl.BlockSpec(bsp.shape, const2),
            pl.BlockSpec(wo.shape, const2),
            pl.BlockSpec(g1.shape, const2),
            pl.BlockSpec(b1.shape, const2),
            pl.BlockSpec(wrt.shape, const2),
            pl.BlockSpec(br.shape, const2),
        ],
        out_specs=[
            pl.BlockSpec((tm, d), row),
            pl.BlockSpec((TOP_K, tm), col),
            pl.BlockSpec((TOP_K, tm), col),
            pl.BlockSpec((TOP_K, tm), col),
            pl.BlockSpec((N_EXPERTS, LANES), const2),
            pl.BlockSpec((MIX_SUB * N_EXPERTS, LANES), row),
        ],
        scratch_shapes=[pltpu.VMEM((tm, DIFF_WIDTH + GMLP_WIDTH), BF16),
                        pltpu.VMEM((N_EXPERTS, LANES), F32)],
        compiler_params=pltpu.CompilerParams(
            dimension_semantics=("arbitrary",), vmem_limit_bytes=VMEM_LIMIT),
        name="mix",
    )(attn, u, vn, x2, wsp, bsp, wo, g1, b1, wrt, br)


def _for_each_run(tile, tcnt_ref, off_ref, dst_ref, make_copy):
    def per_expert(e, carry):
        n = tcnt_ref[tile * N_EXPERTS + e]
        off = off_ref[tile * N_EXPERTS + e]
        dst = dst_ref[tile * N_EXPERTS + e]
        for bit in range(RUN_BITS):
            done = (n >> (bit + 1)) << (bit + 1)

            @pl.when(((n >> bit) & 1) == 1)
            def _():
                make_copy(off + done, dst + done, 1 << bit).start()
        return carry

    lax.fori_loop(0, N_EXPERTS, per_expert, 0)


def _dispatch_kernel(tcnt_ref, off_ref, dst_ref, pad_start_ref, pad_len_ref, nused_ref,
                     pos_ref, x1_ref, xs_hbm, sorted_ref, zero_ref, sem, zsem):
    tm = MOE_TM
    rows = TOP_K * tm
    d = x1_ref.shape[1]
    sub = d // LANES
    nblk = xs_hbm.shape[0] // (FFN_TM * sub)
    i = pl.program_id(0)
    last = pl.num_programs(0) - 1
    slot = lax.rem(i, 2)

    def wait_runs(s):
        pltpu.make_async_copy(sorted_ref.at[pl.ds(s * rows * sub, rows * sub)],
                              xs_hbm.at[pl.ds(0, rows * sub)], sem.at[s]).wait()

    @pl.when(i >= 2)
    def _():
        wait_runs(slot)

    pos = pos_ref[...]
    ri = lax.broadcasted_iota(jnp.int32, (rows, tm), 0)
    hit = (ri == pos[0:1]) | (ri == pos[1:2]) | (ri == pos[2:3]) | (ri == pos[3:4])
    perm = jnp.where(hit, 1.0, 0.0).astype(BF16)
    srt = jnp.dot(perm, x1_ref[...].astype(BF16), preferred_element_type=F32)
    base = pl.multiple_of(slot * (rows * sub), rows * sub)
    _rows_to_slabs(sorted_ref, base, srt, sub)

    def run_copy(local_row, global_row, nrows):
        return pltpu.make_async_copy(
            sorted_ref.at[pl.ds(pl.multiple_of((slot * rows + local_row) * sub, sub), nrows * sub)],
            xs_hbm.at[pl.ds(pl.multiple_of(global_row * sub, sub), nrows * sub)], sem.at[slot])

    _for_each_run(i, tcnt_ref, off_ref, dst_ref, run_copy)

    @pl.when(i == 0)
    def _():
        zero_ref[...] = jnp.zeros_like(zero_ref)

        def zero_copy(r, nrows):
            return pltpu.make_async_copy(
                zero_ref.at[pl.ds(0, nrows * sub)],
                xs_hbm.at[pl.ds(pl.multiple_of(r * sub, sub), nrows * sub)], zsem)

        def tail_copy(blk):
            return pltpu.make_async_copy(
                zero_ref, xs_hbm.at[pl.ds(pl.multiple_of(blk * (FFN_TM * sub), FFN_TM * sub),
                                          FFN_TM * sub)], zsem)

        def tail_start(blk, c):
            tail_copy(blk).start()
            return c

        def tail_wait(blk, c):
            tail_copy(blk).wait()
            return c

        lax.fori_loop(nused_ref[0], nblk, tail_start, 0)
        lax.fori_loop(nused_ref[0], nblk, tail_wait, 0)

        def per_expert(e, _):
            s = pad_start_ref[e]
            cnt = pad_len_ref[e]
            for wait in (False, True):
                for bit in range(FFN_TM.bit_length() - 1):
                    done = (cnt >> (bit + 1)) << (bit + 1)

                    @pl.when(((cnt >> bit) & 1) == 1)
                    def _():
                        piece = zero_copy(s + done, 1 << bit)
                        piece.wait() if wait else piece.start()
            return 0

        lax.fori_loop(0, N_EXPERTS, per_expert, 0)

    @pl.when(i == last)
    def _():
        wait_runs(slot)

        @pl.when(last >= 1)
        def _():
            wait_runs(1 - slot)


def _dispatch(x1, pos, tcnt, off, dst, pad_start, pad_len, nused, p_rows):
    n, d = x1.shape
    tm = MOE_TM
    sub = d // LANES
    return pl.pallas_call(
        _dispatch_kernel,
        out_shape=jax.ShapeDtypeStruct((p_rows * sub, LANES), F32),
        grid_spec=pltpu.PrefetchScalarGridSpec(
            num_scalar_prefetch=6,
            grid=(n // tm,),
            in_specs=[
                pl.BlockSpec((TOP_K, tm), lambda i, *_: (0, i)),
                pl.BlockSpec((tm, d), lambda i, *_: (i, 0)),
            ],
            out_specs=pl.BlockSpec(memory_space=pl.ANY),
            scratch_shapes=[pltpu.VMEM((2 * TOP_K * tm * sub, LANES), F32),
                            pltpu.VMEM((FFN_TM * sub, LANES), F32),
                            pltpu.SemaphoreType.DMA((2,)), pltpu.SemaphoreType.DMA(())],
        ),
        compiler_params=pltpu.CompilerParams(
            dimension_semantics=("arbitrary",), vmem_limit_bytes=VMEM_LIMIT),
        name="dispatch",
    )(tcnt, off, dst, pad_start, pad_len, nused, pos, x1)


def _ffn_kernel(blk_e_ref, nused_ref, next_e_ref, xs_ref, wup_hbm, bup_ref, wdn_hbm, bdn_ref, y_ref,
                wup_f, wdn_f, wup_b, wdn_b, wsem):
    tm = FFN_TM
    dff = wdn_f.shape[0]
    sub = wdn_f.shape[1] // LANES
    i = pl.program_id(0)
    used = i < nused_ref[0]

    @pl.when(jnp.logical_not(used))
    def _():
        y_ref[...] = jnp.zeros_like(y_ref)

    def weight_copies(e):
        return (pltpu.make_async_copy(wup_hbm.at[e], wup_f, wsem.at[0]),
                pltpu.make_async_copy(wdn_hbm.at[e], wdn_f, wsem.at[1]))

    e = blk_e_ref[i]
    new_expert = jnp.logical_or(i == 0, e != blk_e_ref[jnp.maximum(i - 1, 0)])

    @pl.when(jnp.logical_and(used, new_expert))
    def _():
        @pl.when(i == 0)
        def _():
            for cp in weight_copies(e):
                cp.start()

        for cp in weight_copies(e):
            cp.wait()
        for c in range(0, 2 * dff, WEIGHT_CAST_COLS):
            wup_b[:, c:c + WEIGHT_CAST_COLS] = wup_f[:, c:c + WEIGHT_CAST_COLS].astype(BF16)
        for c in range(0, wdn_f.shape[1], WEIGHT_CAST_COLS):
            wdn_b[:, c:c + WEIGHT_CAST_COLS] = wdn_f[:, c:c + WEIGHT_CAST_COLS].astype(BF16)

        @pl.when(next_e_ref[e] != e)
        def _():
            for cp in weight_copies(next_e_ref[e]):
                cp.start()

    @pl.when(used)
    def _():
        xb = _slabs_to_rows(xs_ref, 0, tm, sub).astype(BF16)
        h = jnp.dot(xb, wup_b[...], preferred_element_type=F32) + bup_ref[0]
        gate = jnp.minimum(h[:, :dff], SWIGLU_LIMIT)
        lin = jnp.clip(h[:, dff:], -SWIGLU_LIMIT, SWIGLU_LIMIT)
        act = (lin + 1.0) * gate * jax.nn.sigmoid(SWIGLU_ALPHA * gate)
        y = jnp.dot(act.astype(BF16), wdn_b[...], preferred_element_type=F32) + bdn_ref[0]
        _rows_to_slabs(y_ref, 0, y, sub)


def _ffn(xs, blk_e, nused, next_e, wup, bup, wdn, bdn):
    tm = FFN_TM
    e, d, dff2 = wup.shape
    dff = dff2 // 2
    sub = d // LANES
    rowblk = lambda i, be, nu, ne: (i, 0)
    bsel = lambda i, be, nu, ne: (be[i], 0, 0)
    return pl.pallas_call(
        _ffn_kernel,
        out_shape=jax.ShapeDtypeStruct(xs.shape, F32),
        grid_spec=pltpu.PrefetchScalarGridSpec(
            num_scalar_prefetch=3,
            grid=(xs.shape[0] // (tm * sub),),
            in_specs=[
                pl.BlockSpec((tm * sub, LANES), rowblk),
                pl.BlockSpec(memory_space=pl.ANY),
                pl.BlockSpec((1, 1, dff2), bsel),
                pl.BlockSpec(memory_space=pl.ANY),
                pl.BlockSpec((1, 1, d), bsel),
            ],
            out_specs=pl.BlockSpec((tm * sub, LANES), rowblk),
            scratch_shapes=[pltpu.VMEM((d, dff2), F32), pltpu.VMEM((dff, d), F32),
                            pltpu.VMEM((d, dff2), BF16), pltpu.VMEM((dff, d), BF16),
                            pltpu.SemaphoreType.DMA((2,))],
        ),
        compiler_params=pltpu.CompilerParams(
            dimension_semantics=("arbitrary",), vmem_limit_bytes=VMEM_LIMIT),
        name="ffn",
    )(blk_e, nused, next_e, xs, wup, bup, wdn, bdn)


def _combine_kernel(tcnt_ref, off_ref, dst_ref, pos_ref, gate_ref, x1_ref, g2_ref, b2_ref, y_hbm,
                    o_ref, stage_ref, sem):
    tm = MOE_TM
    rows = TOP_K * tm
    sub = x1_ref.shape[1] // LANES
    i = pl.program_id(0)
    nsteps = pl.num_programs(0)
    slot = lax.rem(i, 2)

    def stage_row(s, h, local_row):
        return pl.multiple_of(((s * COMBINE_SUB + h) * rows + local_row) * sub, sub)

    def fetch_runs(step, s):
        for h in range(COMBINE_SUB):
            def run_copy(local_row, global_row, nrows, h=h):
                return pltpu.make_async_copy(
                    y_hbm.at[pl.ds(pl.multiple_of(global_row * sub, sub), nrows * sub)],
                    stage_ref.at[pl.ds(stage_row(s, h, local_row), nrows * sub)], sem.at[s])
            _for_each_run(step * COMBINE_SUB + h, tcnt_ref, off_ref, dst_ref, run_copy)

    @pl.when(i == 0)
    def _():
        fetch_runs(0, 0)

    @pl.when(i + 1 < nsteps)
    def _():
        fetch_runs(i + 1, 1 - slot)

    pltpu.make_async_copy(y_hbm.at[pl.ds(0, COMBINE_SUB * rows * sub)],
                          stage_ref.at[pl.ds(stage_row(slot, 0, 0), COMBINE_SUB * rows * sub)],
                          sem.at[slot]).wait()

    tiles = range(COMBINE_SUB)
    ci = lax.broadcasted_iota(jnp.int32, (tm, rows), 1)
    ws = []
    for h in tiles:
        pos = pos_ref[h * tm:(h + 1) * tm, :]
        gates = gate_ref[h * tm:(h + 1) * tm, :]
        w = jnp.where(ci == pos[:, 0:1], gates[:, 0:1], 0.0)
        for k in range(1, TOP_K):
            w = w + jnp.where(ci == pos[:, k:k + 1], gates[:, k:k + 1], 0.0)
        ws.append(w)
    wh = [w.astype(BF16) for w in ws]
    wl = [(w - hi.astype(F32)).astype(BF16) for w, hi in zip(ws, wh)]
    ys = [_slabs_to_rows(stage_ref, stage_row(slot, h, 0), rows, sub) for h in tiles]
    yh = [y.astype(BF16) for y in ys]
    yl = [(y - hi.astype(F32)).astype(BF16) for y, hi in zip(ys, yh)]
    for h in tiles:
        ffn = (jnp.dot(wh[h], yh[h], preferred_element_type=F32)
               + jnp.dot(wl[h], yh[h], preferred_element_type=F32)
               + jnp.dot(wh[h], yl[h], preferred_element_type=F32))
        rs = slice(h * tm, (h + 1) * tm)
        o_ref[rs, :] = _layer_norm(DEEPNORM_ALPHA * x1_ref[rs, :] + ffn, g2_ref[...], b2_ref[...])


def _combine(y, pos_tok, gates_tok, tcnt, off, dst, x1, g2, b2):
    n, d = x1.shape
    tm = MOE_TM * COMBINE_SUB
    sub = d // LANES
    row = lambda i, *_: (i, 0)
    const = lambda i, *_: (0, 0)
    return pl.pallas_call(
        _combine_kernel,
        out_shape=jax.ShapeDtypeStruct((n, d), F32),
        grid_spec=pltpu.PrefetchScalarGridSpec(
            num_scalar_prefetch=3,
            grid=(n // tm,),
            in_specs=[
                pl.BlockSpec((tm, TOP_K), row),
                pl.BlockSpec((tm, TOP_K), row),
                pl.BlockSpec((tm, d), row),
                pl.BlockSpec(g2.shape, const),
                pl.BlockSpec(b2.shape, const),
                pl.BlockSpec(memory_space=pl.ANY),
            ],
            out_specs=pl.BlockSpec((tm, d), row),
            scratch_shapes=[pltpu.VMEM((2 * TOP_K * tm * sub, LANES), F32),
                            pltpu.SemaphoreType.DMA((2,))],
        ),
        compiler_params=pltpu.CompilerParams(
            dimension_semantics=("arbitrary",), vmem_limit_bytes=VMEM_LIMIT),
        name="combine",
    )(tcnt, off, dst, pos_tok, gates_tok, x1, g2, b2, y)


def _head_pairs(w_a, w_b):
    d = w_a.shape[0]
    a = w_a.reshape(d, DIFF_HEADS, DIFF_HEAD_DIM)
    b = w_b.reshape(d, DIFF_HEADS, DIFF_HEAD_DIM)
    return jnp.concatenate([a, b], axis=-1).reshape(d, DIFF_HEADS * V_HEAD_DIM)


def _lookup(table, eid):
    eids = jnp.arange(N_EXPERTS, dtype=jnp.int32)[:, None, None]
    return jnp.sum(jnp.where(eid[None] == eids, table[:, None, :], 0), axis=0)


def kernel(x, w_in, lambda_q1, lambda_k1, lambda_q2, lambda_k2, subln_g, gmlp_ln_g, gmlp_ln_b,
           w_spatial, b_spatial, w_o, ln1_g, ln1_b, w_router, b_router, w_up, b_up,
           w_down, b_down, ln2_g, ln2_b):
    b, s, d = x.shape
    n = b * s
    x2 = x.reshape(n, d)

    w = w_in[0]
    c = QK_WIDTH
    scale = DIFF_HEAD_DIM ** -0.5 * LOG2_E
    wq = _head_pairs(w[:, 0:c], w[:, c:2 * c]) * scale
    wk = _head_pairs(w[:, 2 * c:3 * c], w[:, 3 * c:4 * c])
    wqk = jnp.concatenate([wq, wk], axis=1).astype(BF16)
    wvt = w[:, 4 * c:4 * c + DIFF_WIDTH].T.astype(BF16)
    wg = w[:, 4 * c + DIFF_WIDTH:].astype(BF16)
    lamv = jnp.concatenate([lambda_q1, lambda_k1, lambda_q2, lambda_k2], axis=0)
    bsp = jnp.repeat(b_spatial[0].T, GMLP_GROUP_DIM, axis=1)

    assert PROJ_TM == ATTN_T and s % ATTN_T == 0 and n % (MOE_TM * MIX_SUB) == 0
    qq, kk, vt, u, vn = _proj(x2, wqk, wvt, wg, gmlp_ln_g, gmlp_ln_b)
    attn = _attention(qq.reshape(b, s, -1), kk.reshape(b, s, -1),
                      vt.reshape(b, s // ATTN_T, DIFF_WIDTH, ATTN_T),
                      lamv, subln_g.reshape(V_HEAD_DIM, 1)).reshape(n, DIFF_WIDTH)
    x1, eid, gates, rank, cnt, tcnt_l = _mix(attn, u, vn, x2, w_spatial[0], bsp,
                                             w_o[0].astype(BF16), ln1_g, ln1_b,
                                             w_router[0].T, b_router[0][:, None])

    ntiles = n // MOE_TM
    counts = cnt[:, 0].astype(jnp.int32)
    padded = ((counts + FFN_TM - 1) // FFN_TM) * FFN_TM
    end_padded = jnp.cumsum(padded)
    start_padded = end_padded - padded
    tcnt = tcnt_l.reshape(ntiles, N_EXPERTS, LANES)[:, :, 0].astype(jnp.int32)
    before = jnp.cumsum(tcnt, axis=0) - tcnt
    off = jnp.cumsum(tcnt, axis=1) - tcnt
    dst = start_padded[None, :] + before
    tile_base = jnp.repeat((off - before).T, MOE_TM, axis=1)
    pos = _lookup(tile_base, eid) + rank

    p_rows = n * TOP_K + N_EXPERTS * FFN_TM
    nblk = p_rows // FFN_TM
    nused = (end_padded[-1:] // FFN_TM).astype(jnp.int32)
    blk_start = jnp.arange(nblk, dtype=jnp.int32) * FFN_TM
    blk_e = jnp.minimum(jnp.sum((end_padded[None, :] <= blk_start[:, None]).astype(jnp.int32), axis=1),
                        N_EXPERTS - 1)
    tcnt_f, off_f, dst_f = tcnt.reshape(-1), off.reshape(-1), dst.reshape(-1)

    xs = _dispatch(x1, pos, tcnt_f, off_f, dst_f, start_padded + counts, padded - counts, nused,
                   p_rows)
    e_ids = jnp.arange(N_EXPERTS, dtype=jnp.int32)
    later_used = jnp.logical_and(e_ids[None, :] > e_ids[:, None], padded[None, :] > 0)
    next_e = jnp.min(jnp.where(later_used, e_ids[None, :], N_EXPERTS), axis=1)
    next_e = jnp.where(next_e == N_EXPERTS, e_ids, next_e)
    y = _ffn(xs, blk_e, nused, next_e, w_up[0], b_up[0][:, None, :], w_down[0],
             b_down[0][:, None, :])
    out = _combine(y, pos.T, gates.T, tcnt_f, off_f, dst_f, x1, ln2_g, ln2_b)
    return out.reshape(b, s, d)
```

```python
import jax
import jax.numpy as jnp
from jax import lax
from jax.experimental import pallas as pl
from jax.experimental.pallas import tpu as pltpu

DIFF_HEADS = 4
DIFF_HEAD_DIM = 64
V_HEAD_DIM = 2 * DIFF_HEAD_DIM
QK_WIDTH = DIFF_HEADS * DIFF_HEAD_DIM
DIFF_WIDTH = DIFF_HEADS * V_HEAD_DIM
GMLP_GROUPS = 8
GMLP_GROUP_DIM = 64
GMLP_WIDTH = GMLP_GROUPS * GMLP_GROUP_DIM
CHUNK = 128
N_EXPERTS = 32
TOP_K = 4
SWIGLU_LIMIT = 7.0
SWIGLU_ALPHA = 1.702
LN_EPS = 1e-5
DEPTH = 1
DEEPNORM_ALPHA = (2.0 * DEPTH) ** 0.25
LAMBDA_INIT = 0.8 - 0.6 * 1.0
LOG2_E = 1.4426950408889634

LANES = 128

PROJ_TM = 512
ATTN_T = 512
MOE_TM = 256
MIX_SUB = 4
COMBINE_SUB = 2
FFN_TM = 512
WEIGHT_CAST_COLS = 256
RUN_BITS = MOE_TM.bit_length()

VMEM_LIMIT = 48 * 1024 * 1024

BF16 = jnp.bfloat16
F32 = jnp.float32


def _layer_norm(y, g, b):
    mu = jnp.mean(y, axis=-1, keepdims=True)
    yc = y - mu
    var = jnp.mean(yc * yc, axis=-1, keepdims=True)
    return yc * lax.rsqrt(var + LN_EPS) * g + b


def _gelu(x):
    return 0.5 * x * (1.0 + lax.erf(x * (2.0 ** -0.5)))


def _nt_dot(a, b):
    return lax.dot_general(a, b, (((1,), (1,)), ((), ())), preferred_element_type=F32)


def _slabs_to_rows(ref, first, nrows, sub):
    return jnp.concatenate([ref[pl.ds(first + j, nrows, stride=sub), :] for j in range(sub)], axis=1)


def _rows_to_slabs(ref, first, rows, sub):
    for j in range(sub):
        ref[pl.ds(first + j, rows.shape[0], stride=sub), :] = rows[:, j * LANES:(j + 1) * LANES]


def _proj_kernel(x_ref, wqk_ref, wvt_ref, wg_ref, lng_ref, lnb_ref,
                 qq_ref, kk_ref, vt_ref, u_ref, vn_ref):
    xb = x_ref[...].astype(BF16)
    qk = jnp.dot(xb, wqk_ref[...], preferred_element_type=F32)
    qq_ref[...] = qk[:, :DIFF_WIDTH].astype(BF16)
    kk_ref[...] = qk[:, DIFF_WIDTH:].astype(BF16)
    vt_ref[0] = _nt_dot(wvt_ref[...], xb).astype(BF16)
    g = jnp.dot(xb, wg_ref[...], preferred_element_type=F32)
    u_ref[...] = _gelu(g[:, :GMLP_WIDTH]).astype(BF16)
    gv = _gelu(g[:, GMLP_WIDTH:])
    vn_ref[...] = _layer_norm(gv, lng_ref[...], lnb_ref[...]).astype(BF16)


def _proj(x2, wqk, wvt, wg, lng, lnb):
    n, d = x2.shape
    tm = PROJ_TM
    row = lambda i: (i, 0)
    const = lambda i: (0, 0)
    out = jax.ShapeDtypeStruct((n, DIFF_WIDTH), BF16)
    vt_out = jax.ShapeDtypeStruct((n // tm, DIFF_WIDTH, tm), BF16)
    rowspec = pl.BlockSpec((tm, DIFF_WIDTH), row)
    return pl.pallas_call(
        _proj_kernel,
        out_shape=(out, out, vt_out, out, out),
        grid=(n // tm,),
        in_specs=[
            pl.BlockSpec((tm, d), row),
            pl.BlockSpec(wqk.shape, const),
            pl.BlockSpec(wvt.shape, const),
            pl.BlockSpec(wg.shape, const),
            pl.BlockSpec(lng.shape, const),
            pl.BlockSpec(lnb.shape, const),
        ],
        out_specs=[rowspec, rowspec, pl.BlockSpec((1, DIFF_WIDTH, tm), lambda i: (i, 0, 0)),
                   rowspec, rowspec],
        compiler_params=pltpu.CompilerParams(
            dimension_semantics=("arbitrary",), vmem_limit_bytes=VMEM_LIMIT),
        name="proj",
    )(x2, wqk, wvt, wg, lng, lnb)


def _attn_kernel(lamv_ref, g_ref, qq_ref, kk_ref, vt_ref, o_ref, s_ref):
    def q_tile(qi, carry):
        _attn_q_tile(qi, lamv_ref, g_ref, qq_ref, kk_ref, vt_ref, o_ref, s_ref)
        return carry

    lax.fori_loop(0, qq_ref.shape[1] // ATTN_T, q_tile, 0)


def _attn_q_tile(qi, lamv_ref, g_ref, qq_ref, kk_ref, vt_ref, o_ref, s_ref):
    t = ATTN_T
    q_rows = pl.ds(pl.multiple_of(qi * t, t), t)
    q = qq_ref[0, q_rows, :]
    lane = lax.broadcasted_iota(jnp.int32, q.shape, 1)
    first = lane < DIFF_HEAD_DIM
    zero = jnp.zeros_like(q)
    q1 = jnp.where(first, q, zero)
    q2 = jnp.where(first, zero, q)

    def keys(j):
        return kk_ref[0, pl.ds(pl.multiple_of(j * t, t), t), :]

    def scores(j, buf):
        k = keys(j)
        s_ref[buf, 0] = _nt_dot(k, q1)
        s_ref[buf, 1] = _nt_dot(k, q2)

    def consume(j, buf, carry, masked, nxt=None):
        vt = vt_ref[0, j]
        if nxt is not None:
            k_next = keys(nxt[0])
        new = []
        for mp, ((m, l, acc), qh) in enumerate(zip(carry, (q1, q2))):
            if nxt is not None:
                s_ref[nxt[1], mp] = _nt_dot(k_next, qh)
            s = s_ref[buf, mp]
            if masked:
                key = lax.broadcasted_iota(jnp.int32, s.shape, 0)
                qry = lax.broadcasted_iota(jnp.int32, s.shape, 1)
                s = jnp.where(key <= qry, s, -jnp.inf)
            m_new = jnp.maximum(m, jnp.max(s, axis=0, keepdims=True))
            alpha = jnp.exp2(m - m_new)
            p = jnp.exp2(s - m_new)
            l_new = alpha * l + jnp.sum(p, axis=0, keepdims=True)
            acc_new = alpha * acc + jnp.dot(vt, p.astype(BF16), preferred_element_type=F32)
            new.append((m_new, l_new, acc_new))
        return tuple(new)

    def init():
        return (jnp.full((1, t), -jnp.inf, F32), jnp.zeros((1, t), F32),
                jnp.zeros((V_HEAD_DIM, t), F32))

    def finish(carry):
        (_, l1, acc1), (_, l2, acc2) = carry
        lv = lamv_ref[...]
        lam = (jnp.exp(jnp.sum(lv[0:1] * lv[1:2], axis=-1, keepdims=True))
               - jnp.exp(jnp.sum(lv[2:3] * lv[3:4], axis=-1, keepdims=True)) + LAMBDA_INIT)
        o = acc1 / l1 - lam * (acc2 / l2)
        ms = jnp.mean(o * o, axis=0, keepdims=True)
        o = o * lax.rsqrt(ms + LN_EPS) * g_ref[...] * (1.0 - LAMBDA_INIT)
        o_ref[0, q_rows, :] = o.T.astype(BF16)

    def pair(jj, c):
        j = 2 * jj
        c = consume(j, 0, c, False, nxt=(j + 1, 1))
        return consume(j + 1, 1, c, False, nxt=(j + 2, 0))

    scores(0, 0)
    carry = lax.fori_loop(0, qi // 2, pair, (init(), init()))

    @pl.when(qi % 2 == 0)
    def _():
        finish(consume(qi, 0, carry, True))

    @pl.when(qi % 2 == 1)
    def _():
        finish(consume(qi, 1, consume(qi - 1, 0, carry, False, nxt=(qi, 1)), True))


def _attention(qq, kk, vt, lamv, subln_g_col):
    b, s, _ = qq.shape
    t = ATTN_T
    seqspec = pl.BlockSpec((1, s, V_HEAD_DIM), lambda bi, h: (bi, 0, h))
    vtspec = pl.BlockSpec((1, s // t, V_HEAD_DIM, t), lambda bi, h: (bi, 0, h, 0))
    const = lambda bi, h: (0, 0)
    return pl.pallas_call(
        _attn_kernel,
        out_shape=jax.ShapeDtypeStruct((b, s, DIFF_WIDTH), BF16),
        grid=(b, DIFF_HEADS),
        in_specs=[pl.BlockSpec(lamv.shape, const), pl.BlockSpec(subln_g_col.shape, const),
                  seqspec, seqspec, vtspec],
        out_specs=seqspec,
        scratch_shapes=[pltpu.VMEM((2, 2, t, t), F32)],
        compiler_params=pltpu.CompilerParams(
            dimension_semantics=("arbitrary",) * 2, vmem_limit_bytes=VMEM_LIMIT),
        name="attn",
    )(lamv, subln_g_col, qq, kk, vt)


def _mix_kernel(attn_ref, u_ref, vn_ref, x_ref, wsp_ref, bsp_ref, wo_ref, g1_ref, b1_ref,
                wrt_ref, br_ref,
                x1_ref, eid_ref, gate_ref, rank_ref, cnt_ref, tcnt_ref,
                cat_ref, carry_ref):
    tm = MOE_TM

    @pl.when(pl.program_id(0) == 0)
    def _():
        carry_ref[...] = jnp.zeros_like(carry_ref)

    ri = lax.broadcasted_iota(jnp.int32, (CHUNK, CHUNK), 0)
    ci = lax.broadcasted_iota(jnp.int32, (CHUNK, CHUNK), 1)
    tril = ci <= ri
    first = ci < GMLP_GROUP_DIM
    w = [jnp.where(tril, wsp_ref[g], 0.0).astype(BF16) for g in range(GMLP_GROUPS)]
    wr = wrt_ref[...]
    wh = wr.astype(BF16)
    wl = (wr - wh.astype(F32)).astype(BF16)
    eio = lax.broadcasted_iota(jnp.int32, (N_EXPERTS, tm), 0).astype(F32)
    ti = lax.broadcasted_iota(jnp.int32, (tm, tm), 0)
    tj = lax.broadcasted_iota(jnp.int32, (tm, tm), 1)
    before = jnp.where(ti < tj, 1.0, 0.0).astype(BF16)

    halves = [slice(h * tm, (h + 1) * tm) for h in range(MIX_SUB)]

    for rs in halves:
        cat_ref[rs, :DIFF_WIDTH] = attn_ref[rs, :]
    for c in range(MIX_SUB * tm // CHUNK):
        rows = slice(c * CHUNK, (c + 1) * CHUNK)
        for jb in range(GMLP_WIDTH // LANES):
            cols = slice(jb * LANES, (jb + 1) * LANES)
            vb = vn_ref[rows, cols]
            zero = jnp.zeros_like(vb)
            z = (jnp.dot(w[2 * jb], jnp.where(first, vb, zero), preferred_element_type=F32)
                 + jnp.dot(w[2 * jb + 1], jnp.where(first, zero, vb), preferred_element_type=F32))
            gated = u_ref[rows, cols].astype(F32) * (z + bsp_ref[:, cols])
            cat_ref[rows, DIFF_WIDTH + jb * LANES:DIFF_WIDTH + (jb + 1) * LANES] = gated.astype(BF16)

    x1s = []
    for rs in halves:
        mixed = jnp.dot(cat_ref[rs, :], wo_ref[...], preferred_element_type=F32)
        x1 = _layer_norm(DEEPNORM_ALPHA * x_ref[rs, :] + mixed, g1_ref[...], b1_ref[...])
        x1_ref[rs, :] = x1
        x1s.append(x1)

    curs = []
    for x1 in x1s:
        xh = x1.astype(BF16)
        xl = (x1 - xh.astype(F32)).astype(BF16)
        curs.append(_nt_dot(wh, xh) + _nt_dot(wl, xh) + _nt_dot(wh, xl) + br_ref[...])

    vals = [[] for _ in halves]
    idxs = [[] for _ in halves]
    sels = [[] for _ in halves]
    for _ in range(TOP_K):
        for h in range(MIX_SUB):
            mx = jnp.max(curs[h], axis=0, keepdims=True)
            idx = jnp.min(jnp.where(curs[h] == mx, eio, float(N_EXPERTS)), axis=0, keepdims=True)
            sel = eio == idx
            vals[h].append(mx)
            idxs[h].append(idx)
            sels[h].append(sel)
            curs[h] = jnp.where(sel, -jnp.inf, curs[h])

    carry = carry_ref[:, 0:1]
    for h, rs in enumerate(halves):
        ex = [jnp.exp(vk - vals[h][0]) for vk in vals[h]]
        denom = ex[0] + ex[1] + ex[2] + ex[3]
        gate_ref[:, rs] = jnp.concatenate([e / denom for e in ex], axis=0)
        eid_ref[:, rs] = jnp.concatenate(idxs[h], axis=0).astype(jnp.int32)

        chosen = (sels[h][0] | sels[h][1] | sels[h][2] | sels[h][3])
        onehot = jnp.where(chosen, 1.0, 0.0)
        cnt_before = jnp.dot(onehot.astype(BF16), before, preferred_element_type=F32) + carry
        ranks = [jnp.sum(jnp.where(s, cnt_before, 0.0), axis=0, keepdims=True) for s in sels[h]]
        rank_ref[:, rs] = jnp.concatenate(ranks, axis=0).astype(jnp.int32)
        tile_cnt = jnp.sum(onehot, axis=1, keepdims=True)
        tcnt_ref[h * N_EXPERTS:(h + 1) * N_EXPERTS, :] = jnp.broadcast_to(tile_cnt, (N_EXPERTS, LANES))
        carry = carry + tile_cnt

    carry_ref[...] = jnp.broadcast_to(carry, carry_ref.shape)
    cnt_ref[...] = jnp.broadcast_to(carry, cnt_ref.shape)


def _mix(attn, u, vn, x2, wsp, bsp, wo, g1, b1, wrt, br):
    n, d = x2.shape
    tm = MOE_TM * MIX_SUB
    row = lambda i: (i, 0)
    col = lambda i: (0, i)
    const2 = lambda i: (0, 0)
    const3 = lambda i: (0, 0, 0)
    tok = lambda dt: jax.ShapeDtypeStruct((TOP_K, n), dt)
    return pl.pallas_call(
        _mix_kernel,
        out_shape=(jax.ShapeDtypeStruct((n, d), F32), tok(jnp.int32), tok(F32), tok(jnp.int32),
                   jax.ShapeDtypeStruct((N_EXPERTS, LANES), F32),
                   jax.ShapeDtypeStruct((n // MOE_TM * N_EXPERTS, LANES), F32)),
        grid=(n // tm,),
        in_specs=[
            pl.BlockSpec((tm, DIFF_WIDTH), row),
            pl.BlockSpec((tm, GMLP_WIDTH), row),
            pl.BlockSpec((tm, GMLP_WIDTH), row),
            pl.BlockSpec((tm, d), row),
            pl.BlockSpec(wsp.shape, const3),
            pl.BlockSpec(bsp.shape, const2),
            pl.BlockSpec(wo.shape, const2),
            pl.BlockSpec(g1.shape, const2),
            pl.BlockSpec(b1.shape, const2),
            pl.BlockSpec(wrt.shape, const2),
            pl.BlockSpec(br.shape, const2),
        ],
        out_specs=[
            pl.BlockSpec((tm, d), row),
            pl.BlockSpec((TOP_K, tm), col),
            pl.BlockSpec((TOP_K, tm), col),
            pl.BlockSpec((TOP_K, tm), col),
            pl.BlockSpec((N_EXPERTS, LANES), const2),
            pl.BlockSpec((MIX_SUB * N_EXPERTS, LANES), row),
        ],
        scratch_shapes=[pltpu.VMEM((tm, DIFF_WIDTH + GMLP_WIDTH), BF16),
                        pltpu.VMEM((N_EXPERTS, LANES), F32)],
        compiler_params=pltpu.CompilerParams(
            dimension_semantics=("arbitrary",), vmem_limit_bytes=VMEM_LIMIT),
        name="mix",
    )(attn, u, vn, x2, wsp, bsp, wo, g1, b1, wrt, br)


def _for_each_run(tile, tcnt_ref, off_ref, dst_ref, make_copy):
    def per_expert(e, carry):
        n = tcnt_ref[tile * N_EXPERTS + e]
        off = off_ref[tile * N_EXPERTS + e]
        dst = dst_ref[tile * N_EXPERTS + e]
        for bit in range(RUN_BITS):
            done = (n >> (bit + 1)) << (bit + 1)

            @pl.when(((n >> bit) & 1) == 1)
            def _():
                make_copy(off + done, dst + done, 1 << bit).start()
        return carry

    lax.fori_loop(0, N_EXPERTS, per_expert, 0)


def _dispatch_kernel(tcnt_ref, off_ref, dst_ref, pad_start_ref, pad_len_ref, nused_ref,
                     pos_ref, x1_ref, xs_hbm, sorted_ref, zero_ref, sem, zsem):
    tm = MOE_TM
    rows = TOP_K * tm
    d = x1_ref.shape[1]
    sub = d // LANES
    nblk = xs_hbm.shape[0] // (FFN_TM * sub)
    i = pl.program_id(0)
    last = pl.num_programs(0) - 1
    slot = lax.rem(i, 2)

    def wait_runs(s):
        pltpu.make_async_copy(sorted_ref.at[pl.ds(s * rows * sub, rows * sub)],
                              xs_hbm.at[pl.ds(0, rows * sub)], sem.at[s]).wait()

    @pl.when(i >= 2)
    def _():
        wait_runs(slot)

    pos = pos_ref[...]
    ri = lax.broadcasted_iota(jnp.int32, (rows, tm), 0)
    hit = (ri == pos[0:1]) | (ri == pos[1:2]) | (ri == pos[2:3]) | (ri == pos[3:4])
    perm = jnp.where(hit, 1.0, 0.0).astype(BF16)
    srt = jnp.dot(perm, x1_ref[...].astype(BF16), preferred_element_type=F32)
    base = pl.multiple_of(slot * (rows * sub), rows * sub)
    _rows_to_slabs(sorted_ref, base, srt, sub)

    def run_copy(local_row, global_row, nrows):
        return pltpu.make_async_copy(
            sorted_ref.at[pl.ds(pl.multiple_of((slot * rows + local_row) * sub, sub), nrows * sub)],
            xs_hbm.at[pl.ds(pl.multiple_of(global_row * sub, sub), nrows * sub)], sem.at[slot])

    _for_each_run(i, tcnt_ref, off_ref, dst_ref, run_copy)

    @pl.when(i == 0)
    def _():
        zero_ref[...] = jnp.zeros_like(zero_ref)

        def zero_copy(r, nrows):
            return pltpu.make_async_copy(
                zero_ref.at[pl.ds(0, nrows * sub)],
                xs_hbm.at[pl.ds(pl.multiple_of(r * sub, sub), nrows * sub)], zsem)

        def tail_copy(blk):
            return pltpu.make_async_copy(
                zero_ref, xs_hbm.at[pl.ds(pl.multiple_of(blk * (FFN_TM * sub), FFN_TM * sub),
                                          FFN_TM * sub)], zsem)

        def tail_start(blk, c):
            tail_copy(blk).start()
            return c

        def tail_wait(blk, c):
            tail_copy(blk).wait()
            return c

        lax.fori_loop(nused_ref[0], nblk, tail_start, 0)
        lax.fori_loop(nused_ref[0], nblk, tail_wait, 0)

        def per_expert(e, _):
            s = pad_start_ref[e]
            cnt = pad_len_ref[e]
            for wait in (False, True):
                for bit in range(FFN_TM.bit_length() - 1):
                    done = (cnt >> (bit + 1)) << (bit + 1)

                    @pl.when(((cnt >> bit) & 1) == 1)
                    def _():
                        piece = zero_copy(s + done, 1 << bit)
                        piece.wait() if wait else piece.start()
            return 0

        lax.fori_loop(0, N_EXPERTS, per_expert, 0)

    @pl.when(i == last)
    def _():
        wait_runs(slot)

        @pl.when(last >= 1)
        def _():
            wait_runs(1 - slot)


def _dispatch(x1, pos, tcnt, off, dst, pad_start, pad_len, nused, p_rows):
    n, d = x1.shape
    tm = MOE_TM
    sub = d // LANES
    return pl.pallas_call(
        _dispatch_kernel,
        out_shape=jax.ShapeDtypeStruct((p_rows * sub, LANES), F32),
        grid_spec=pltpu.PrefetchScalarGridSpec(
            num_scalar_prefetch=6,
            grid=(n // tm,),
            in_specs=[
                pl.BlockSpec((TOP_K, tm), lambda i, *_: (0, i)),
                pl.BlockSpec((tm, d), lambda i, *_: (i, 0)),
            ],
            out_specs=pl.BlockSpec(memory_space=pl.ANY),
            scratch_shapes=[pltpu.VMEM((2 * TOP_K * tm * sub, LANES), F32),
                            pltpu.VMEM((FFN_TM * sub, LANES), F32),
                            pltpu.SemaphoreType.DMA((2,)), pltpu.SemaphoreType.DMA(())],
        ),
        compiler_params=pltpu.CompilerParams(
            dimension_semantics=("arbitrary",), vmem_limit_bytes=VMEM_LIMIT),
        name="dispatch",
    )(tcnt, off, dst, pad_start, pad_len, nused, pos, x1)


def _ffn_kernel(blk_e_ref, nused_ref, next_e_ref, xs_ref, wup_hbm, bup_ref, wdn_hbm, bdn_ref, y_ref,
                wup_f, wdn_f, wup_b, wdn_b, wsem):
    tm = FFN_TM
    dff = wdn_f.shape[0]
    sub = wdn_f.shape[1] // LANES
    i = pl.program_id(0)
    used = i < nused_ref[0]

    @pl.when(jnp.logical_not(used))
    def _():
        y_ref[...] = jnp.zeros_like(y_ref)

    def weight_copies(e):
        return (pltpu.make_async_copy(wup_hbm.at[e], wup_f, wsem.at[0]),
                pltpu.make_async_copy(wdn_hbm.at[e], wdn_f, wsem.at[1]))

    e = blk_e_ref[i]
    new_expert = jnp.logical_or(i == 0, e != blk_e_ref[jnp.maximum(i - 1, 0)])

    @pl.when(jnp.logical_and(used, new_expert))
    def _():
        @pl.when(i == 0)
        def _():
            for cp in weight_copies(e):
                cp.start()

        for cp in weight_copies(e):
            cp.wait()
        for c in range(0, 2 * dff, WEIGHT_CAST_COLS):
            wup_b[:, c:c + WEIGHT_CAST_COLS] = wup_f[:, c:c + WEIGHT_CAST_COLS].astype(BF16)
        for c in range(0, wdn_f.shape[1], WEIGHT_CAST_COLS):
            wdn_b[:, c:c + WEIGHT_CAST_COLS] = wdn_f[:, c:c + WEIGHT_CAST_COLS].astype(BF16)

        @pl.when(next_e_ref[e] != e)
        def _():
            for cp in weight_copies(next_e_ref[e]):
                cp.start()

    @pl.when(used)
    def _():
        xb = _slabs_to_rows(xs_ref, 0, tm, sub).astype(BF16)
        h = jnp.dot(xb, wup_b[...], preferred_element_type=F32) + bup_ref[0]
        gate = jnp.minimum(h[:, :dff], SWIGLU_LIMIT)
        lin = jnp.clip(h[:, dff:], -SWIGLU_LIMIT, SWIGLU_LIMIT)
        act = (lin + 1.0) * gate * jax.nn.sigmoid(SWIGLU_ALPHA * gate)
        y = jnp.dot(act.astype(BF16), wdn_b[...], preferred_element_type=F32) + bdn_ref[0]
        _rows_to_slabs(y_ref, 0, y, sub)


def _ffn(xs, blk_e, nused, next_e, wup, bup, wdn, bdn):
    tm = FFN_TM
    e, d, dff2 = wup.shape
    dff = dff2 // 2
    sub = d // LANES
    rowblk = lambda i, be, nu, ne: (i, 0)
    bsel = lambda i, be, nu, ne: (be[i], 0, 0)
    return pl.pallas_call(
        _ffn_kernel,
        out_shape=jax.ShapeDtypeStruct(xs.shape, F32),
        grid_spec=pltpu.PrefetchScalarGridSpec(
            num_scalar_prefetch=3,
            grid=(xs.shape[0] // (tm * sub),),
            in_specs=[
                pl.BlockSpec((tm * sub, LANES), rowblk),
                pl.BlockSpec(memory_space=pl.ANY),
                pl.BlockSpec((1, 1, dff2), bsel),
                pl.BlockSpec(memory_space=pl.ANY),
                pl.BlockSpec((1, 1, d), bsel),
            ],
            out_specs=pl.BlockSpec((tm * sub, LANES), rowblk),
            scratch_shapes=[pltpu.VMEM((d, dff2), F32), pltpu.VMEM((dff, d), F32),
                            pltpu.VMEM((d, dff2), BF16), pltpu.VMEM((dff, d), BF16),
                            pltpu.SemaphoreType.DMA((2,))],
        ),
        compiler_params=pltpu.CompilerParams(
            dimension_semantics=("arbitrary",), vmem_limit_bytes=VMEM_LIMIT),
        name="ffn",
    )(blk_e, nused, next_e, xs, wup, bup, wdn, bdn)


def _combine_kernel(tcnt_ref, off_ref, dst_ref, pos_ref, gate_ref, x1_ref, g2_ref, b2_ref, y_hbm,
                    o_ref, stage_ref, sem):
    tm = MOE_TM
    rows = TOP_K * tm
    sub = x1_ref.shape[1] // LANES
    i = pl.program_id(0)
    nsteps = pl.num_programs(0)
    slot = lax.rem(i, 2)

    def stage_row(s, h, local_row):
        return pl.multiple_of(((s * COMBINE_SUB + h) * rows + local_row) * sub, sub)

    def fetch_runs(step, s):
        for h in range(COMBINE_SUB):
            def run_copy(local_row, global_row, nrows, h=h):
                return pltpu.make_async_copy(
                    y_hbm.at[pl.ds(pl.multiple_of(global_row * sub, sub), nrows * sub)],
                    stage_ref.at[pl.ds(stage_row(s, h, local_row), nrows * sub)], sem.at[s])
            _for_each_run(step * COMBINE_SUB + h, tcnt_ref, off_ref, dst_ref, run_copy)

    @pl.when(i == 0)
    def _():
        fetch_runs(0, 0)

    @pl.when(i + 1 < nsteps)
    def _():
        fetch_runs(i + 1, 1 - slot)

    pltpu.make_async_copy(y_hbm.at[pl.ds(0, COMBINE_SUB * rows * sub)],
                          stage_ref.at[pl.ds(stage_row(slot, 0, 0), COMBINE_SUB * rows * sub)],
                          sem.at[slot]).wait()

    tiles = range(COMBINE_SUB)
    ci = lax.broadcasted_iota(jnp.int32, (tm, rows), 1)
    ws = []
    for h in tiles:
        pos = pos_ref[h * tm:(h + 1) * tm, :]
        gates = gate_ref[h * tm:(h + 1) * tm, :]
        w = jnp.where(ci == pos[:, 0:1], gates[:, 0:1], 0.0)
        for k in range(1, TOP_K):
            w = w + jnp.where(ci == pos[:, k:k + 1], gates[:, k:k + 1], 0.0)
        ws.append(w)
    wh = [w.astype(BF16) for w in ws]
    wl = [(w - hi.astype(F32)).astype(BF16) for w, hi in zip(ws, wh)]
    ys = [_slabs_to_rows(stage_ref, stage_row(slot, h, 0), rows, sub) for h in tiles]
    yh = [y.astype(BF16) for y in ys]
    yl = [(y - hi.astype(F32)).astype(BF16) for y, hi in zip(ys, yh)]
    for h in tiles:
        ffn = (jnp.dot(wh[h], yh[h], preferred_element_type=F32)
               + jnp.dot(wl[h], yh[h], preferred_element_type=F32)
               + jnp.dot(wh[h], yl[h], preferred_element_type=F32))
        rs = slice(h * tm, (h + 1) * tm)
        o_ref[rs, :] = _layer_norm(DEEPNORM_ALPHA * x1_ref[rs, :] + ffn, g2_ref[...], b2_ref[...])


def _combine(y, pos_tok, gates_tok, tcnt, off, dst, x1, g2, b2):
    n, d = x1.shape
    tm = MOE_TM * COMBINE_SUB
    sub = d // LANES
    row = lambda i, *_: (i, 0)
    const = lambda i, *_: (0, 0)
    return pl.pallas_call(
        _combine_kernel,
        out_shape=jax.ShapeDtypeStruct((n, d), F32),
        grid_spec=pltpu.PrefetchScalarGridSpec(
            num_scalar_prefetch=3,
            grid=(n // tm,),
            in_specs=[
                pl.BlockSpec((tm, TOP_K), row),
                pl.BlockSpec((tm, TOP_K), row),
                pl.BlockSpec((tm, d), row),
                pl.BlockSpec(g2.shape, const),
                pl.BlockSpec(b2.shape, const),
                pl.BlockSpec(memory_space=pl.ANY),
            ],
            out_specs=pl.BlockSpec((tm, d), row),
            scratch_shapes=[pltpu.VMEM((2 * TOP_K * tm * sub, LANES), F32),
                            pltpu.SemaphoreType.DMA((2,))],
        ),
        compiler_params=pltpu.CompilerParams(
            dimension_semantics=("arbitrary",), vmem_limit_bytes=VMEM_LIMIT),
        name="combine",
    )(tcnt, off, dst, pos_tok, gates_tok, x1, g2, b2, y)


def _head_pairs(w_a, w_b):
    d = w_a.shape[0]
    a = w_a.reshape(d, DIFF_HEADS, DIFF_HEAD_DIM)
    b = w_b.reshape(d, DIFF_HEADS, DIFF_HEAD_DIM)
    return jnp.concatenate([a, b], axis=-1).reshape(d, DIFF_HEADS * V_HEAD_DIM)


def _lookup(table, eid):
    eids = jnp.arange(N_EXPERTS, dtype=jnp.int32)[:, None, None]
    return jnp.sum(jnp.where(eid[None] == eids, table[:, None, :], 0), axis=0)


def kernel(x, w_in, lambda_q1, lambda_k1, lambda_q2, lambda_k2, subln_g, gmlp_ln_g, gmlp_ln_b,
           w_spatial, b_spatial, w_o, ln1_g, ln1_b, w_router, b_router, w_up, b_up,
           w_down, b_down, ln2_g, ln2_b):
    b, s, d = x.shape
    n = b * s
    x2 = x.reshape(n, d)

    w = w_in[0]
    c = QK_WIDTH
    scale = DIFF_HEAD_DIM ** -0.5 * LOG2_E
    wq = _head_pairs(w[:, 0:c], w[:, c:2 * c]) * scale
    wk = _head_pairs(w[:, 2 * c:3 * c], w[:, 3 * c:4 * c])
    wqk = jnp.concatenate([wq, wk], axis=1).astype(BF16)
    wvt = w[:, 4 * c:4 * c + DIFF_WIDTH].T.astype(BF16)
    wg = w[:, 4 * c + DIFF_WIDTH:].astype(BF16)
    lamv = jnp.concatenate([lambda_q1, lambda_k1, lambda_q2, lambda_k2], axis=0)
    bsp = jnp.repeat(b_spatial[0].T, GMLP_GROUP_DIM, axis=1)

    assert PROJ_TM == ATTN_T and s % ATTN_T == 0 and n % (MOE_TM * MIX_SUB) == 0
    qq, kk, vt, u, vn = _proj(x2, wqk, wvt, wg, gmlp_ln_g, gmlp_ln_b)
    attn = _attention(qq.reshape(b, s, -1), kk.reshape(b, s, -1),
                      vt.reshape(b, s // ATTN_T, DIFF_WIDTH, ATTN_T),
                      lamv, subln_g.reshape(V_HEAD_DIM, 1)).reshape(n, DIFF_WIDTH)
    x1, eid, gates, rank, cnt, tcnt_l = _mix(attn, u, vn, x2, w_spatial[0], bsp,
                                             w_o[0].astype(BF16), ln1_g, ln1_b,
                                             w_router[0].T, b_router[0][:, None])

    ntiles = n // MOE_TM
    counts = cnt[:, 0].astype(jnp.int32)
    padded = ((counts + FFN_TM - 1) // FFN_TM) * FFN_TM
    end_padded = jnp.cumsum(padded)
    start_padded = end_padded - padded
    tcnt = tcnt_l.reshape(ntiles, N_EXPERTS, LANES)[:, :, 0].astype(jnp.int32)
    before = jnp.cumsum(tcnt, axis=0) - tcnt
    off = jnp.cumsum(tcnt, axis=1) - tcnt
    dst = start_padded[None, :] + before
    tile_base = jnp.repeat((off - before).T, MOE_TM, axis=1)
    pos = _lookup(tile_base, eid) + rank

    p_rows = n * TOP_K + N_EXPERTS * FFN_TM
    nblk = p_rows // FFN_TM
    nused = (end_padded[-1:] // FFN_TM).astype(jnp.int32)
    blk_start = jnp.arange(nblk, dtype=jnp.int32) * FFN_TM
    blk_e = jnp.minimum(jnp.sum((end_padded[None, :] <= blk_start[:, None]).astype(jnp.int32), axis=1),
                        N_EXPERTS - 1)
    tcnt_f, off_f, dst_f = tcnt.reshape(-1), off.reshape(-1), dst.reshape(-1)

    xs = _dispatch(x1, pos, tcnt_f, off_f, dst_f, start_padded + counts, padded - counts, nused,
                   p_rows)
    e_ids = jnp.arange(N_EXPERTS, dtype=jnp.int32)
    later_used = jnp.logical_and(e_ids[None, :] > e_ids[:, None], padded[None, :] > 0)
    next_e = jnp.min(jnp.where(later_used, e_ids[None, :], N_EXPERTS), axis=1)
    next_e = jnp.where(next_e == N_EXPERTS, e_ids, next_e)
    y = _ffn(xs, blk_e, nused, next_e, w_up[0], b_up[0][:, None, :], w_down[0],
             b_down[0][:, None, :])
    out = _combine(y, pos.T, gates.T, tcnt_f, off_f, dst_f, x1, ln2_g, ln2_b)
    return out.reshape(b, s, d)
```

```python
import jax
import jax.numpy as jnp
from jax import lax
from jax.experimental import pallas as pl
from jax.experimental.pallas import tpu as pltpu

DIFF_HEADS = 4
DIFF_HEAD_DIM = 64
V_HEAD_DIM = 2 * DIFF_HEAD_DIM
QK_WIDTH = DIFF_HEADS * DIFF_HEAD_DIM
DIFF_WIDTH = DIFF_HEADS * V_HEAD_DIM
GMLP_GROUPS = 8
GMLP_GROUP_DIM = 64
GMLP_WIDTH = GMLP_GROUPS * GMLP_GROUP_DIM
CHUNK = 128
N_EXPERTS = 32
TOP_K = 4
SWIGLU_LIMIT = 7.0
SWIGLU_ALPHA = 1.702
LN_EPS = 1e-5
DEPTH = 1
DEEPNORM_ALPHA = (2.0 * DEPTH) ** 0.25
LAMBDA_INIT = 0.8 - 0.6 * 1.0
LOG2_E = 1.4426950408889634

LANES = 128

PROJ_TM = 512
ATTN_T = 512
MOE_TM = 256
MIX_SUB = 4
COMBINE_SUB = 2
FFN_TM = 512
WEIGHT_CAST_COLS = 256
RUN_BITS = MOE_TM.bit_length()

VMEM_LIMIT = 48 * 1024 * 1024

BF16 = jnp.bfloat16
F32 = jnp.float32


def _layer_norm(y, g, b):
    mu = jnp.mean(y, axis=-1, keepdims=True)
    yc = y - mu
    var = jnp.mean(yc * yc, axis=-1, keepdims=True)
    return yc * lax.rsqrt(var + LN_EPS) * g + b


def _gelu(x):
    return 0.5 * x * (1.0 + lax.erf(x * (2.0 ** -0.5)))


def _nt_dot(a, b):
    return lax.dot_general(a, b, (((1,), (1,)), ((), ())), preferred_element_type=F32)


def _slabs_to_rows(ref, first, nrows, sub):
    return jnp.concatenate([ref[pl.ds(first + j, nrows, stride=sub), :] for j in range(sub)], axis=1)


def _rows_to_slabs(ref, first, rows, sub):
    for j in range(sub):
        ref[pl.ds(first + j, rows.shape[0], stride=sub), :] = rows[:, j * LANES:(j + 1) * LANES]


def _proj_kernel(x_ref, wqk_ref, wvt_ref, wg_ref, lng_ref, lnb_ref,
                 qq_ref, kk_ref, vt_ref, u_ref, vn_ref):
    xb = x_ref[...].astype(BF16)
    qk = jnp.dot(xb, wqk_ref[...], preferred_element_type=F32)
    qq_ref[...] = qk[:, :DIFF_WIDTH].astype(BF16)
    kk_ref[...] = qk[:, DIFF_WIDTH:].astype(BF16)
    vt_ref[0] = _nt_dot(wvt_ref[...], xb).astype(BF16)
    g = jnp.dot(xb, wg_ref[...], preferred_element_type=F32)
    u_ref[...] = _gelu(g[:, :GMLP_WIDTH]).astype(BF16)
    gv = _gelu(g[:, GMLP_WIDTH:])
    vn_ref[...] = _layer_norm(gv, lng_ref[...], lnb_ref[...]).astype(BF16)


def _proj(x2, wqk, wvt, wg, lng, lnb):
    n, d = x2.shape
    tm = PROJ_TM
    row = lambda i: (i, 0)
    const = lambda i: (0, 0)
    out = jax.ShapeDtypeStruct((n, DIFF_WIDTH), BF16)
    vt_out = jax.ShapeDtypeStruct((n // tm, DIFF_WIDTH, tm), BF16)
    rowspec = pl.BlockSpec((tm, DIFF_WIDTH), row)
    return pl.pallas_call(
        _proj_kernel,
        out_shape=(out, out, vt_out, out, out),
        grid=(n // tm,),
        in_specs=[
            pl.BlockSpec((tm, d), row),
            pl.BlockSpec(wqk.shape, const),
            pl.BlockSpec(wvt.shape, const),
            pl.BlockSpec(wg.shape, const),
            pl.BlockSpec(lng.shape, const),
            pl.BlockSpec(lnb.shape, const),
        ],
        out_specs=[rowspec, rowspec, pl.BlockSpec((1, DIFF_WIDTH, tm), lambda i: (i, 0, 0)),
                   rowspec, rowspec],
        compiler_params=pltpu.CompilerParams(
            dimension_semantics=("arbitrary",), vmem_limit_bytes=VMEM_LIMIT),
        name="proj",
    )(x2, wqk, wvt, wg, lng, lnb)


ATTN_GROUP = 4


def _attn_kernel(lamv_ref, g_ref, qq_ref, kk_ref, vt_ref, o_ref, s_ref):
    t = ATTN_T
    nq = qq_ref.shape[1] // t
    starts = (0, 1, 1, 0)

    def group(g, carry):
        for r in range(ATTN_GROUP):
            _attn_q_tile(ATTN_GROUP * g + r, nq, r % 2, starts[r],
                         lamv_ref, g_ref, qq_ref, kk_ref, vt_ref, o_ref, s_ref)
        return carry

    q1, q2 = _masked_queries(qq_ref, 0)
    k0 = kk_ref[0, 0:t, :]
    s_ref[0, 0] = _nt_dot(k0, q1)
    s_ref[0, 1] = _nt_dot(k0, q2)
    lax.fori_loop(0, nq // ATTN_GROUP, group, 0)


def _masked_queries(qq_ref, qi):
    t = ATTN_T
    q = qq_ref[0, pl.ds(pl.multiple_of(qi * t, t), t), :]
    first = lax.broadcasted_iota(jnp.int32, q.shape, 1) < DIFF_HEAD_DIM
    zero = jnp.zeros_like(q)
    return jnp.where(first, q, zero), jnp.where(first, zero, q)


def _attn_q_tile(qi, nq, odd, start, lamv_ref, g_ref, qq_ref, kk_ref, vt_ref, o_ref, s_ref):
    t = ATTN_T
    q_rows = pl.ds(pl.multiple_of(qi * t, t), t)
    q1, q2 = _masked_queries(qq_ref, qi)
    next_q = _masked_queries(qq_ref, jnp.minimum(qi + 1, nq - 1))

    def keys(j):
        return kk_ref[0, pl.ds(pl.multiple_of(j * t, t), t), :]

    def consume(j, buf, carry, masked, nxt):
        vt = vt_ref[0, j]
        k_next = keys(nxt[0])
        new = []
        for mp, ((m, l, acc), qh) in enumerate(zip(carry, nxt[2])):
            s_ref[nxt[1], mp] = _nt_dot(k_next, qh)
            s = s_ref[buf, mp]
            if masked:
                key = lax.broadcasted_iota(jnp.int32, s.shape, 0)
                qry = lax.broadcasted_iota(jnp.int32, s.shape, 1)
                s = jnp.where(key <= qry, s, -jnp.inf)
            m_new = jnp.maximum(m, jnp.max(s, axis=0, keepdims=True))
            alpha = jnp.exp2(m - m_new)
            p = jnp.exp2(s - m_new)
            l_new = alpha * l + jnp.sum(p, axis=0, keepdims=True)
            acc_new = alpha * acc + jnp.dot(vt, p.astype(BF16), preferred_element_type=F32)
            new.append((m_new, l_new, acc_new))
        return tuple(new)

    def init():
        return (jnp.full((1, t), -jnp.inf, F32), jnp.zeros((1, t), F32),
                jnp.zeros((V_HEAD_DIM, t), F32))

    def finish(carry):
        (_, l1, acc1), (_, l2, acc2) = carry
        lv = lamv_ref[...]
        lam = (jnp.exp(jnp.sum(lv[0:1] * lv[1:2], axis=-1, keepdims=True))
               - jnp.exp(jnp.sum(lv[2:3] * lv[3:4], axis=-1, keepdims=True)) + LAMBDA_INIT)
        o = acc1 / l1 - lam * (acc2 / l2)
        ms = jnp.mean(o * o, axis=0, keepdims=True)
        o = o * lax.rsqrt(ms + LN_EPS) * g_ref[...] * (1.0 - LAMBDA_INIT)
        o_ref[0, q_rows, :] = o.T.astype(BF16)

    own = (q1, q2)
    a, b = start, 1 - start

    def pair(jj, c):
        j = 2 * jj
        c = consume(j, a, c, False, (j + 1, b, own))
        return consume(j + 1, b, c, False, (j + 2, a, own))

    carry = lax.fori_loop(0, qi // 2, pair, (init(), init()))
    if odd:
        carry = consume(qi - 1, a, carry, False, (qi, b, own))
        finish(consume(qi, b, carry, True, (0, a, next_q)))
    else:
        finish(consume(qi, a, carry, True, (0, b, next_q)))


def _attention(qq, kk, vt, lamv, subln_g_col):
    b, s, _ = qq.shape
    t = ATTN_T
    seqspec = pl.BlockSpec((1, s, V_HEAD_DIM), lambda bi, h: (bi, 0, h))
    vtspec = pl.BlockSpec((1, s // t, V_HEAD_DIM, t), lambda bi, h: (bi, 0, h, 0))
    const = lambda bi, h: (0, 0)
    return pl.pallas_call(
        _attn_kernel,
        out_shape=jax.ShapeDtypeStruct((b, s, DIFF_WIDTH), BF16),
        grid=(b, DIFF_HEADS),
        in_specs=[pl.BlockSpec(lamv.shape, const), pl.BlockSpec(subln_g_col.shape, const),
                  seqspec, seqspec, vtspec],
        out_specs=seqspec,
        scratch_shapes=[pltpu.VMEM((2, 2, t, t), F32)],
        compiler_params=pltpu.CompilerParams(
            dimension_semantics=("arbitrary",) * 2, vmem_limit_bytes=VMEM_LIMIT),
        name="attn",
    )(lamv, subln_g_col, qq, kk, vt)


def _mix_kernel(attn_ref, u_ref, vn_ref, x_ref, wsp_ref, bsp_ref, wo_ref, g1_ref, b1_ref,
                wrt_ref, br_ref,
                x1_ref, eid_ref, gate_ref, rank_ref, cnt_ref, tcnt_ref,
                cat_ref, carry_ref):
    tm = MOE_TM

    @pl.when(pl.program_id(0) == 0)
    def _():
        carry_ref[...] = jnp.zeros_like(carry_ref)

    ri = lax.broadcasted_iota(jnp.int32, (CHUNK, CHUNK), 0)
    ci = lax.broadcasted_iota(jnp.int32, (CHUNK, CHUNK), 1)
    tril = ci <= ri
    first = ci < GMLP_GROUP_DIM
    w = [jnp.where(tril, wsp_ref[g], 0.0).astype(BF16) for g in range(GMLP_GROUPS)]
    wr = wrt_ref[...]
    wh = wr.astype(BF16)
    wl = (wr - wh.astype(F32)).astype(BF16)
    eio = lax.broadcasted_iota(jnp.int32, (N_EXPERTS, tm), 0).astype(F32)
    ti = lax.broadcasted_iota(jnp.int32, (tm, tm), 0)
    tj = lax.broadcasted_iota(jnp.int32, (tm, tm), 1)
    before = jnp.where(ti < tj, 1.0, 0.0).astype(BF16)

    halves = [slice(h * tm, (h + 1) * tm) for h in range(MIX_SUB)]

    for rs in halves:
        cat_ref[rs, :DIFF_WIDTH] = attn_ref[rs, :]
    for c in range(MIX_SUB * tm // CHUNK):
        rows = slice(c * CHUNK, (c + 1) * CHUNK)
        for jb in range(GMLP_WIDTH // LANES):
            cols = slice(jb * LANES, (jb + 1) * LANES)
            vb = vn_ref[rows, cols]
            zero = jnp.zeros_like(vb)
            z = (jnp.dot(w[2 * jb], jnp.where(first, vb, zero), preferred_element_type=F32)
                 + jnp.dot(w[2 * jb + 1], jnp.where(first, zero, vb), preferred_element_type=F32))
            gated = u_ref[rows, cols].astype(F32) * (z + bsp_ref[:, cols])
            cat_ref[rows, DIFF_WIDTH + jb * LANES:DIFF_WIDTH + (jb + 1) * LANES] = gated.astype(BF16)

    x1s = []
    for rs in halves:
        mixed = jnp.dot(cat_ref[rs, :], wo_ref[...], preferred_element_type=F32)
        x1 = _layer_norm(DEEPNORM_ALPHA * x_ref[rs, :] + mixed, g1_ref[...], b1_ref[...])
        x1_ref[rs, :] = x1
        x1s.append(x1)

    curs = []
    for x1 in x1s:
        xh = x1.astype(BF16)
        xl = (x1 - xh.astype(F32)).astype(BF16)
        curs.append(_nt_dot(wh, xh) + _nt_dot(wl, xh) + _nt_dot(wh, xl) + br_ref[...])

    vals = [[] for _ in halves]
    idxs = [[] for _ in halves]
    sels = [[] for _ in halves]
    for _ in range(TOP_K):
        for h in range(MIX_SUB):
            mx = jnp.max(curs[h], axis=0, keepdims=True)
            idx = jnp.min(jnp.where(curs[h] == mx, eio, float(N_EXPERTS)), axis=0, keepdims=True)
            sel = eio == idx
            vals[h].append(mx)
            idxs[h].append(idx)
            sels[h].append(sel)
            curs[h] = jnp.where(sel, -jnp.inf, curs[h])

    carry = carry_ref[:, 0:1]
    for h, rs in enumerate(halves):
        ex = [jnp.exp(vk - vals[h][0]) for vk in vals[h]]
        denom = ex[0] + ex[1] + ex[2] + ex[3]
        gate_ref[:, rs] = jnp.concatenate([e / denom for e in ex], axis=0)
        eid_ref[:, rs] = jnp.concatenate(idxs[h], axis=0).astype(jnp.int32)

        chosen = (sels[h][0] | sels[h][1] | sels[h][2] | sels[h][3])
        onehot = jnp.where(chosen, 1.0, 0.0)
        cnt_before = jnp.dot(onehot.astype(BF16), before, preferred_element_type=F32) + carry
        ranks = [jnp.sum(jnp.where(s, cnt_before, 0.0), axis=0, keepdims=True) for s in sels[h]]
        rank_ref[:, rs] = jnp.concatenate(ranks, axis=0).astype(jnp.int32)
        tile_cnt = jnp.sum(onehot, axis=1, keepdims=True)
        tcnt_ref[h * N_EXPERTS:(h + 1) * N_EXPERTS, :] = jnp.broadcast_to(tile_cnt, (N_EXPERTS, LANES))
        carry = carry + tile_cnt

    carry_ref[...] = jnp.broadcast_to(carry, carry_ref.shape)
    cnt_ref[...] = jnp.broadcast_to(carry, cnt_ref.shape)


def _mix(attn, u, vn, x2, wsp, bsp, wo, g1, b1, wrt, br):
    n, d = x2.shape
    tm = MOE_TM * MIX_SUB
    row = lambda i: (i, 0)
    col = lambda i: (0, i)
    const2 = lambda i: (0, 0)
    const3 = lambda i: (0, 0, 0)
    tok = lambda dt: jax.ShapeDtypeStruct((TOP_K, n), dt)
    return pl.pallas_call(
        _mix_kernel,
        out_shape=(jax.ShapeDtypeStruct((n, d), F32), tok(jnp.int32), tok(F32), tok(jnp.int32),
                   jax.ShapeDtypeStruct((N_EXPERTS, LANES), F32),
                   jax.ShapeDtypeStruct((n // MOE_TM * N_EXPERTS, LANES), F32)),
        grid=(n // tm,),
        in_specs=[
            pl.BlockSpec((tm, DIFF_WIDTH), row),
            pl.BlockSpec((tm, GMLP_WIDTH), row),
            pl.BlockSpec((tm, GMLP_WIDTH), row),
            pl.BlockSpec((tm, d), row),
            pl.BlockSpec(wsp.shape, const3),
            pl.BlockSpec(bsp.shape, const2),
            pl.BlockSpec(wo.shape, const2),
            pl.BlockSpec(g1.shape, const2),
            pl.BlockSpec(b1.shape, const2),
            pl.BlockSpec(wrt.shape, const2),
            pl.BlockSpec(br.shape, const2),
        ],
        out_specs=[
            pl.BlockSpec((tm, d), row),
            pl.BlockSpec((TOP_K, tm), col),
            pl.BlockSpec((TOP_K, tm), col),
            pl.BlockSpec((TOP_K, tm), col),
            pl.BlockSpec((N_EXPERTS, LANES), const2),
            pl.BlockSpec((MIX_SUB * N_EXPERTS, LANES), row),
        ],
        scratch_shapes=[pltpu.VMEM((tm, DIFF_WIDTH + GMLP_WIDTH), BF16),
                        pltpu.VMEM((N_EXPERTS, LANES), F32)],
        compiler_params=pltpu.CompilerParams(
            dimension_semantics=("arbitrary",), vmem_limit_bytes=VMEM_LIMIT),
        name="mix",
    )(attn, u, vn, x2, wsp, bsp, wo, g1, b1, wrt, br)


def _for_each_run(tile, tcnt_ref, off_ref, dst_ref, make_copy):
    def per_expert(e, carry):
        n = tcnt_ref[tile * N_EXPERTS + e]
        off = off_ref[tile * N_EXPERTS + e]
        dst = dst_ref[tile * N_EXPERTS + e]
        for bit in range(RUN_BITS):
            done = (n >> (bit + 1)) << (bit + 1)

            @pl.when(((n >> bit) & 1) == 1)
            def _():
                make_copy(off + done, dst + done, 1 << bit).start()
        return carry

    lax.fori_loop(0, N_EXPERTS, per_expert, 0)


def _dispatch_kernel(tcnt_ref, off_ref, dst_ref, pad_start_ref, pad_len_ref, nused_ref,
                     pos_ref, x1_ref, xs_hbm, sorted_ref, zero_ref, sem, zsem):
    tm = MOE_TM
    rows = TOP_K * tm
    d = x1_ref.shape[1]
    sub = d // LANES
    nblk = xs_hbm.shape[0] // (FFN_TM * sub)
    i = pl.program_id(0)
    last = pl.num_programs(0) - 1
    slot = lax.rem(i, 2)

    def wait_runs(s):
        pltpu.make_async_copy(sorted_ref.at[pl.ds(s * rows * sub, rows * sub)],
                              xs_hbm.at[pl.ds(0, rows * sub)], sem.at[s]).wait()

    @pl.when(i >= 2)
    def _():
        wait_runs(slot)

    pos = pos_ref[...]
    ri = lax.broadcasted_iota(jnp.int32, (rows, tm), 0)
    hit = (ri == pos[0:1]) | (ri == pos[1:2]) | (ri == pos[2:3]) | (ri == pos[3:4])
    perm = jnp.where(hit, 1.0, 0.0).astype(BF16)
    srt = jnp.dot(perm, x1_ref[...].astype(BF16), preferred_element_type=F32)
    base = pl.multiple_of(slot * (rows * sub), rows * sub)
    _rows_to_slabs(sorted_ref, base, srt, sub)

    def run_copy(local_row, global_row, nrows):
        return pltpu.make_async_copy(
            sorted_ref.at[pl.ds(pl.multiple_of((slot * rows + local_row) * sub, sub), nrows * sub)],
            xs_hbm.at[pl.ds(pl.multiple_of(global_row * sub, sub), nrows * sub)], sem.at[slot])

    _for_each_run(i, tcnt_ref, off_ref, dst_ref, run_copy)

    @pl.when(i == 0)
    def _():
        zero_ref[...] = jnp.zeros_like(zero_ref)

        def zero_copy(r, nrows):
            return pltpu.make_async_copy(
                zero_ref.at[pl.ds(0, nrows * sub)],
                xs_hbm.at[pl.ds(pl.multiple_of(r * sub, sub), nrows * sub)], zsem)

        def tail_copy(blk):
            return pltpu.make_async_copy(
                zero_ref, xs_hbm.at[pl.ds(pl.multiple_of(blk * (FFN_TM * sub), FFN_TM * sub),
                                          FFN_TM * sub)], zsem)

        def tail_start(blk, c):
            tail_copy(blk).start()
            return c

        def tail_wait(blk, c):
            tail_copy(blk).wait()
            return c

        lax.fori_loop(nused_ref[0], nblk, tail_start, 0)
        lax.fori_loop(nused_ref[0], nblk, tail_wait, 0)

        def per_expert(e, _):
            s = pad_start_ref[e]
            cnt = pad_len_ref[e]
            for wait in (False, True):
                for bit in range(FFN_TM.bit_length() - 1):
                    done = (cnt >> (bit + 1)) << (bit + 1)

                    @pl.when(((cnt >> bit) & 1) == 1)
                    def _():
                        piece = zero_copy(s + done, 1 << bit)
                        piece.wait() if wait else piece.start()
            return 0

        lax.fori_loop(0, N_EXPERTS, per_expert, 0)

    @pl.when(i == last)
    def _():
        wait_runs(slot)

        @pl.when(last >= 1)
        def _():
            wait_runs(1 - slot)


def _dispatch(x1, pos, tcnt, off, dst, pad_start, pad_len, nused, p_rows):
    n, d = x1.shape
    tm = MOE_TM
    sub = d // LANES
    return pl.pallas_call(
        _dispatch_kernel,
        out_shape=jax.ShapeDtypeStruct((p_rows * sub, LANES), F32),
        grid_spec=pltpu.PrefetchScalarGridSpec(
            num_scalar_prefetch=6,
            grid=(n // tm,),
            in_specs=[
                pl.BlockSpec((TOP_K, tm), lambda i, *_: (0, i)),
                pl.BlockSpec((tm, d), lambda i, *_: (i, 0)),
            ],
            out_specs=pl.BlockSpec(memory_space=pl.ANY),
            scratch_shapes=[pltpu.VMEM((2 * TOP_K * tm * sub, LANES), F32),
                            pltpu.VMEM((FFN_TM * sub, LANES), F32),
                            pltpu.SemaphoreType.DMA((2,)), pltpu.SemaphoreType.DMA(())],
        ),
        compiler_params=pltpu.CompilerParams(
            dimension_semantics=("arbitrary",), vmem_limit_bytes=VMEM_LIMIT),
        name="dispatch",
    )(tcnt, off, dst, pad_start, pad_len, nused, pos, x1)


def _ffn_kernel(blk_e_ref, nused_ref, next_e_ref, xs_ref, wup_hbm, bup_ref, wdn_hbm, bdn_ref, y_ref,
                wup_f, wdn_f, wup_b, wdn_b, wsem):
    tm = FFN_TM
    dff = wdn_f.shape[0]
    sub = wdn_f.shape[1] // LANES
    i = pl.program_id(0)
    used = i < nused_ref[0]

    @pl.when(jnp.logical_not(used))
    def _():
        y_ref[...] = jnp.zeros_like(y_ref)

    def weight_copies(e):
        return (pltpu.make_async_copy(wup_hbm.at[e], wup_f, wsem.at[0]),
                pltpu.make_async_copy(wdn_hbm.at[e], wdn_f, wsem.at[1]))

    e = blk_e_ref[i]
    new_expert = jnp.logical_or(i == 0, e != blk_e_ref[jnp.maximum(i - 1, 0)])

    @pl.when(jnp.logical_and(used, new_expert))
    def _():
        @pl.when(i == 0)
        def _():
            for cp in weight_copies(e):
                cp.start()

        for cp in weight_copies(e):
            cp.wait()
        for c in range(0, 2 * dff, WEIGHT_CAST_COLS):
            wup_b[:, c:c + WEIGHT_CAST_COLS] = wup_f[:, c:c + WEIGHT_CAST_COLS].astype(BF16)
        for c in range(0, wdn_f.shape[1], WEIGHT_CAST_COLS):
            wdn_b[:, c:c + WEIGHT_CAST_COLS] = wdn_f[:, c:c + WEIGHT_CAST_COLS].astype(BF16)

        @pl.when(next_e_ref[e] != e)
        def _():
            for cp in weight_copies(next_e_ref[e]):
                cp.start()

    @pl.when(used)
    def _():
        xb = _slabs_to_rows(xs_ref, 0, tm, sub).astype(BF16)
        h = jnp.dot(xb, wup_b[...], preferred_element_type=F32) + bup_ref[0]
        gate = jnp.minimum(h[:, :dff], SWIGLU_LIMIT)
        lin = jnp.clip(h[:, dff:], -SWIGLU_LIMIT, SWIGLU_LIMIT)
        act = (lin + 1.0) * gate * jax.nn.sigmoid(SWIGLU_ALPHA * gate)
        y = jnp.dot(act.astype(BF16), wdn_b[...], preferred_element_type=F32) + bdn_ref[0]
        _rows_to_slabs(y_ref, 0, y, sub)


def _ffn(xs, blk_e, nused, next_e, wup, bup, wdn, bdn):
    tm = FFN_TM
    e, d, dff2 = wup.shape
    dff = dff2 // 2
    sub = d // LANES
    rowblk = lambda i, be, nu, ne: (i, 0)
    bsel = lambda i, be, nu, ne: (be[i], 0, 0)
    return pl.pallas_call(
        _ffn_kernel,
        out_shape=jax.ShapeDtypeStruct(xs.shape, F32),
        grid_spec=pltpu.PrefetchScalarGridSpec(
            num_scalar_prefetch=3,
            grid=(xs.shape[0] // (tm * sub),),
            in_specs=[
                pl.BlockSpec((tm * sub, LANES), rowblk),
                pl.BlockSpec(memory_space=pl.ANY),
                pl.BlockSpec((1, 1, dff2), bsel),
                pl.BlockSpec(memory_space=pl.ANY),
                pl.BlockSpec((1, 1, d), bsel),
            ],
            out_specs=pl.BlockSpec((tm * sub, LANES), rowblk),
            scratch_shapes=[pltpu.VMEM((d, dff2), F32), pltpu.VMEM((dff, d), F32),
                            pltpu.VMEM((d, dff2), BF16), pltpu.VMEM((dff, d), BF16),
                            pltpu.SemaphoreType.DMA((2,))],
        ),
        compiler_params=pltpu.CompilerParams(
            dimension_semantics=("arbitrary",), vmem_limit_bytes=VMEM_LIMIT),
        name="ffn",
    )(blk_e, nused, next_e, xs, wup, bup, wdn, bdn)


def _combine_kernel(tcnt_ref, off_ref, dst_ref, pos_ref, gate_ref, x1_ref, g2_ref, b2_ref, y_hbm,
                    o_ref, stage_ref, sem):
    tm = MOE_TM
    rows = TOP_K * tm
    sub = x1_ref.shape[1] // LANES
    i = pl.program_id(0)
    nsteps = pl.num_programs(0)
    slot = lax.rem(i, 2)

    def stage_row(s, h, local_row):
        return pl.multiple_of(((s * COMBINE_SUB + h) * rows + local_row) * sub, sub)

    def fetch_runs(step, s):
        for h in range(COMBINE_SUB):
            def run_copy(local_row, global_row, nrows, h=h):
                return pltpu.make_async_copy(
                    y_hbm.at[pl.ds(pl.multiple_of(global_row * sub, sub), nrows * sub)],
                    stage_ref.at[pl.ds(stage_row(s, h, local_row), nrows * sub)], sem.at[s])
            _for_each_run(step * COMBINE_SUB + h, tcnt_ref, off_ref, dst_ref, run_copy)

    @pl.when(i == 0)
    def _():
        fetch_runs(0, 0)

    @pl.when(i + 1 < nsteps)
    def _():
        fetch_runs(i + 1, 1 - slot)

    pltpu.make_async_copy(y_hbm.at[pl.ds(0, COMBINE_SUB * rows * sub)],
                          stage_ref.at[pl.ds(stage_row(slot, 0, 0), COMBINE_SUB * rows * sub)],
                          sem.at[slot]).wait()

    tiles = range(COMBINE_SUB)
    ci = lax.broadcasted_iota(jnp.int32, (tm, rows), 1)
    ws = []
    for h in tiles:
        pos = pos_ref[h * tm:(h + 1) * tm, :]
        gates = gate_ref[h * tm:(h + 1) * tm, :]
        w = jnp.where(ci == pos[:, 0:1], gates[:, 0:1], 0.0)
        for k in range(1, TOP_K):
            w = w + jnp.where(ci == pos[:, k:k + 1], gates[:, k:k + 1], 0.0)
        ws.append(w)
    wh = [w.astype(BF16) for w in ws]
    wl = [(w - hi.astype(F32)).astype(BF16) for w, hi in zip(ws, wh)]
    ys = [_slabs_to_rows(stage_ref, stage_row(slot, h, 0), rows, sub) for h in tiles]
    yh = [y.astype(BF16) for y in ys]
    yl = [(y - hi.astype(F32)).astype(BF16) for y, hi in zip(ys, yh)]
    for h in tiles:
        ffn = (jnp.dot(wh[h], yh[h], preferred_element_type=F32)
               + jnp.dot(wl[h], yh[h], preferred_element_type=F32)
               + jnp.dot(wh[h], yl[h], preferred_element_type=F32))
        rs = slice(h * tm, (h + 1) * tm)
        o_ref[rs, :] = _layer_norm(DEEPNORM_ALPHA * x1_ref[rs, :] + ffn, g2_ref[...], b2_ref[...])


def _combine(y, pos_tok, gates_tok, tcnt, off, dst, x1, g2, b2):
    n, d = x1.shape
    tm = MOE_TM * COMBINE_SUB
    sub = d // LANES
    row = lambda i, *_: (i, 0)
    const = lambda i, *_: (0, 0)
    return pl.pallas_call(
        _combine_kernel,
        out_shape=jax.ShapeDtypeStruct((n, d), F32),
        grid_spec=pltpu.PrefetchScalarGridSpec(
            num_scalar_prefetch=3,
            grid=(n // tm,),
            in_specs=[
                pl.BlockSpec((tm, TOP_K), row),
                pl.BlockSpec((tm, TOP_K), row),
                pl.BlockSpec((tm, d), row),
                pl.BlockSpec(g2.shape, const),
                pl.BlockSpec(b2.shape, const),
                pl.BlockSpec(memory_space=pl.ANY),
            ],
            out_specs=pl.BlockSpec((tm, d), row),
            scratch_shapes=[pltpu.VMEM((2 * TOP_K * tm * sub, LANES), F32),
                            pltpu.SemaphoreType.DMA((2,))],
        ),
        compiler_params=pltpu.CompilerParams(
            dimension_semantics=("arbitrary",), vmem_limit_bytes=VMEM_LIMIT),
        name="combine",
    )(tcnt, off, dst, pos_tok, gates_tok, x1, g2, b2, y)


def _head_pairs(w_a, w_b):
    d = w_a.shape[0]
    a = w_a.reshape(d, DIFF_HEADS, DIFF_HEAD_DIM)
    b = w_b.reshape(d, DIFF_HEADS, DIFF_HEAD_DIM)
    return jnp.concatenate([a, b], axis=-1).reshape(d, DIFF_HEADS * V_HEAD_DIM)


def _lookup(table, eid):
    eids = jnp.arange(N_EXPERTS, dtype=jnp.int32)[:, None, None]
    return jnp.sum(jnp.where(eid[None] == eids, table[:, None, :], 0), axis=0)


def kernel(x, w_in, lambda_q1, lambda_k1, lambda_q2, lambda_k2, subln_g, gmlp_ln_g, gmlp_ln_b,
           w_spatial, b_spatial, w_o, ln1_g, ln1_b, w_router, b_router, w_up, b_up,
           w_down, b_down, ln2_g, ln2_b):
    b, s, d = x.shape
    n = b * s
    x2 = x.reshape(n, d)

    w = w_in[0]
    c = QK_WIDTH
    scale = DIFF_HEAD_DIM ** -0.5 * LOG2_E
    wq = _head_pairs(w[:, 0:c], w[:, c:2 * c]) * scale
    wk = _head_pairs(w[:, 2 * c:3 * c], w[:, 3 * c:4 * c])
    wqk = jnp.concatenate([wq, wk], axis=1).astype(BF16)
    wvt = w[:, 4 * c:4 * c + DIFF_WIDTH].T.astype(BF16)
    wg = w[:, 4 * c + DIFF_WIDTH:].astype(BF16)
    lamv = jnp.concatenate([lambda_q1, lambda_k1, lambda_q2, lambda_k2], axis=0)
    bsp = jnp.repeat(b_spatial[0].T, GMLP_GROUP_DIM, axis=1)

    assert PROJ_TM == ATTN_T and s % ATTN_T == 0 and n % (MOE_TM * MIX_SUB) == 0
    qq, kk, vt, u, vn = _proj(x2, wqk, wvt, wg, gmlp_ln_g, gmlp_ln_b)
    attn = _attention(qq.reshape(b, s, -1), kk.reshape(b, s, -1),
                      vt.reshape(b, s // ATTN_T, DIFF_WIDTH, ATTN_T),
                      lamv, subln_g.reshape(V_HEAD_DIM, 1)).reshape(n, DIFF_WIDTH)
    x1, eid, gates, rank, cnt, tcnt_l = _mix(attn, u, vn, x2, w_spatial[0], bsp,
                                             w_o[0].astype(BF16), ln1_g, ln1_b,
                                             w_router[0].T, b_router[0][:, None])

    ntiles = n // MOE_TM
    counts = cnt[:, 0].astype(jnp.int32)
    padded = ((counts + FFN_TM - 1) // FFN_TM) * FFN_TM
    end_padded = jnp.cumsum(padded)
    start_padded = end_padded - padded
    tcnt = tcnt_l.reshape(ntiles, N_EXPERTS, LANES)[:, :, 0].astype(jnp.int32)
    before = jnp.cumsum(tcnt, axis=0) - tcnt
    off = jnp.cumsum(tcnt, axis=1) - tcnt
    dst = start_padded[None, :] + before
    tile_base = jnp.repeat((off - before).T, MOE_TM, axis=1)
    pos = _lookup(tile_base, eid) + rank

    p_rows = n * TOP_K + N_EXPERTS * FFN_TM
    nblk = p_rows // FFN_TM
    nused = (end_padded[-1:] // FFN_TM).astype(jnp.int32)
    blk_start = jnp.arange(nblk, dtype=jnp.int32) * FFN_TM
    blk_e = jnp.minimum(jnp.sum((end_padded[None, :] <= blk_start[:, None]).astype(jnp.int32), axis=1),
                        N_EXPERTS - 1)
    tcnt_f, off_f, dst_f = tcnt.reshape(-1), off.reshape(-1), dst.reshape(-1)

    xs = _dispatch(x1, pos, tcnt_f, off_f, dst_f, start_padded + counts, padded - counts, nused,
                   p_rows)
    e_ids = jnp.arange(N_EXPERTS, dtype=jnp.int32)
    later_used = jnp.logical_and(e_ids[None, :] > e_ids[:, None], padded[None, :] > 0)
    next_e = jnp.min(jnp.where(later_used, e_ids[None, :], N_EXPERTS), axis=1)
    next_e = jnp.where(next_e == N_EXPERTS, e_ids, next_e)
    y = _ffn(xs, blk_e, nused, next_e, w_up[0], b_up[0][:, None, :], w_down[0],
             b_down[0][:, None, :])
    out = _combine(y, pos.T, gates.T, tcnt_f, off_f, dst_f, x1, ln2_g, ln2_b)
    return out.reshape(b, s, d)
```

```python
import jax
import jax.numpy as jnp
from jax import lax
from jax.experimental import pallas as pl
from jax.experimental.pallas import tpu as pltpu

DIFF_HEADS = 4
DIFF_HEAD_DIM = 64
V_HEAD_DIM = 2 * DIFF_HEAD_DIM
QK_WIDTH = DIFF_HEADS * DIFF_HEAD_DIM
DIFF_WIDTH = DIFF_HEADS * V_HEAD_DIM
GMLP_GROUPS = 8
GMLP_GROUP_DIM = 64
GMLP_WIDTH = GMLP_GROUPS * GMLP_GROUP_DIM
CHUNK = 128
N_EXPERTS = 32
TOP_K = 4
SWIGLU_LIMIT = 7.0
SWIGLU_ALPHA = 1.702
LN_EPS = 1e-5
DEPTH = 1
DEEPNORM_ALPHA = (2.0 * DEPTH) ** 0.25
LAMBDA_INIT = 0.8 - 0.6 * 1.0
LOG2_E = 1.4426950408889634

LANES = 128

PROJ_TM = 512
ATTN_T = 512
ATTN_GROUP = 4
MOE_TM = 256
MIX_SUB = 4
COMBINE_SUB = 2
FFN_TM = 512
WEIGHT_CAST_COLS = 256
RUN_BITS = MOE_TM.bit_length()
RUN_SMALL_BITS = 6

VMEM_LIMIT = 48 * 1024 * 1024

BF16 = jnp.bfloat16
F32 = jnp.float32


def _layer_norm(y, g, b):
    mu = jnp.mean(y, axis=-1, keepdims=True)
    yc = y - mu
    var = jnp.mean(yc * yc, axis=-1, keepdims=True)
    return yc * lax.rsqrt(var + LN_EPS) * g + b


def _gelu(x):
    return 0.5 * x * (1.0 + lax.erf(x * (2.0 ** -0.5)))


def _nt_dot(a, b):
    return lax.dot_general(a, b, (((1,), (1,)), ((), ())), preferred_element_type=F32)


def _slabs_to_rows(ref, first, nrows, sub):
    return jnp.concatenate([ref[pl.ds(first + j, nrows, stride=sub), :] for j in range(sub)], axis=1)


def _rows_to_slabs(ref, first, rows, sub):
    for j in range(sub):
        ref[pl.ds(first + j, rows.shape[0], stride=sub), :] = rows[:, j * LANES:(j + 1) * LANES]


def _proj_kernel(x_ref, wqk_ref, wvt_ref, wg_ref, lng_ref, lnb_ref,
                 qq_ref, kk_ref, vt_ref, u_ref, vn_ref):
    xb = x_ref[...].astype(BF16)
    qk = jnp.dot(xb, wqk_ref[...], preferred_element_type=F32)
    qq_ref[...] = qk[:, :DIFF_WIDTH].astype(BF16)
    kk_ref[...] = qk[:, DIFF_WIDTH:].astype(BF16)
    vt_ref[0] = _nt_dot(wvt_ref[...], xb).astype(BF16)
    g = jnp.dot(xb, wg_ref[...], preferred_element_type=F32)
    u_ref[...] = _gelu(g[:, :GMLP_WIDTH]).astype(BF16)
    gv = _gelu(g[:, GMLP_WIDTH:])
    vn_ref[...] = _layer_norm(gv, lng_ref[...], lnb_ref[...]).astype(BF16)


def _proj(x2, wqk, wvt, wg, lng, lnb):
    n, d = x2.shape
    tm = PROJ_TM
    row = lambda i: (i, 0)
    const = lambda i: (0, 0)
    out = jax.ShapeDtypeStruct((n, DIFF_WIDTH), BF16)
    vt_out = jax.ShapeDtypeStruct((n // tm, DIFF_WIDTH, tm), BF16)
    rowspec = pl.BlockSpec((tm, DIFF_WIDTH), row)
    return pl.pallas_call(
        _proj_kernel,
        out_shape=(out, out, vt_out, out, out),
        grid=(n // tm,),
        in_specs=[
            pl.BlockSpec((tm, d), row),
            pl.BlockSpec(wqk.shape, const),
            pl.BlockSpec(wvt.shape, const),
            pl.BlockSpec(wg.shape, const),
            pl.BlockSpec(lng.shape, const),
            pl.BlockSpec(lnb.shape, const),
        ],
        out_specs=[rowspec, rowspec, pl.BlockSpec((1, DIFF_WIDTH, tm), lambda i: (i, 0, 0)),
                   rowspec, rowspec],
        compiler_params=pltpu.CompilerParams(
            dimension_semantics=("arbitrary",), vmem_limit_bytes=VMEM_LIMIT),
        name="proj",
    )(x2, wqk, wvt, wg, lng, lnb)


def _attn_kernel(lamv_ref, g_ref, qq_ref, kk_ref, vt_ref, o_ref, s_ref):
    t = ATTN_T
    nq = qq_ref.shape[1] // t
    starts = (0, 1, 1, 0)

    def group(g, carry):
        for r in range(ATTN_GROUP):
            _attn_q_tile(ATTN_GROUP * g + r, nq, r % 2, starts[r],
                         lamv_ref, g_ref, qq_ref, kk_ref, vt_ref, o_ref, s_ref)
        return carry

    q1, q2 = _masked_queries(qq_ref, 0)
    k0 = kk_ref[0, 0:t, :]
    s_ref[0, 0] = _nt_dot(k0, q1)
    s_ref[0, 1] = _nt_dot(k0, q2)
    lax.fori_loop(0, nq // ATTN_GROUP, group, 0)


def _masked_queries(qq_ref, qi):
    t = ATTN_T
    q = qq_ref[0, pl.ds(pl.multiple_of(qi * t, t), t), :]
    first = lax.broadcasted_iota(jnp.int32, q.shape, 1) < DIFF_HEAD_DIM
    zero = jnp.zeros_like(q)
    return jnp.where(first, q, zero), jnp.where(first, zero, q)


def _attn_q_tile(qi, nq, odd, start, lamv_ref, g_ref, qq_ref, kk_ref, vt_ref, o_ref, s_ref):
    t = ATTN_T
    q_rows = pl.ds(pl.multiple_of(qi * t, t), t)
    q1, q2 = _masked_queries(qq_ref, qi)
    next_q = _masked_queries(qq_ref, jnp.minimum(qi + 1, nq - 1))

    def keys(j):
        return kk_ref[0, pl.ds(pl.multiple_of(j * t, t), t), :]

    def consume(j, buf, carry, masked, nxt):
        vt = vt_ref[0, j]
        k_next = keys(nxt[0])
        new = []
        for mp, ((m, l, acc), qh) in enumerate(zip(carry, nxt[2])):
            s_ref[nxt[1], mp] = _nt_dot(k_next, qh)
            s = s_ref[buf, mp]
            if masked:
                key = lax.broadcasted_iota(jnp.int32, s.shape, 0)
                qry = lax.broadcasted_iota(jnp.int32, s.shape, 1)
                s = jnp.where(key <= qry, s, -jnp.inf)
            m_new = jnp.maximum(m, jnp.max(s, axis=0, keepdims=True))
            alpha = jnp.exp2(m - m_new)
            p = jnp.exp2(s - m_new)
            l_new = alpha * l + jnp.sum(p, axis=0, keepdims=True)
            acc_new = alpha * acc + jnp.dot(vt, p.astype(BF16), preferred_element_type=F32)
            new.append((m_new, l_new, acc_new))
        return tuple(new)

    def init():
        return (jnp.full((1, t), -jnp.inf, F32), jnp.zeros((1, t), F32),
                jnp.zeros((V_HEAD_DIM, t), F32))

    def finish(carry):
        (_, l1, acc1), (_, l2, acc2) = carry
        lv = lamv_ref[...]
        lam = (jnp.exp(jnp.sum(lv[0:1] * lv[1:2], axis=-1, keepdims=True))
               - jnp.exp(jnp.sum(lv[2:3] * lv[3:4], axis=-1, keepdims=True)) + LAMBDA_INIT)
        o = acc1 / l1 - lam * (acc2 / l2)
        ms = jnp.mean(o * o, axis=0, keepdims=True)
        o = o * lax.rsqrt(ms + LN_EPS) * g_ref[...] * (1.0 - LAMBDA_INIT)
        o_ref[0, q_rows, :] = o.T.astype(BF16)

    own = (q1, q2)
    a, b = start, 1 - start

    def pair(jj, c):
        j = 2 * jj
        c = consume(j, a, c, False, (j + 1, b, own))
        return consume(j + 1, b, c, False, (j + 2, a, own))

    carry = lax.fori_loop(0, qi // 2, pair, (init(), init()))
    if odd:
        carry = consume(qi - 1, a, carry, False, (qi, b, own))
        finish(consume(qi, b, carry, True, (0, a, next_q)))
    else:
        finish(consume(qi, a, carry, True, (0, b, next_q)))


def _attention(qq, kk, vt, lamv, subln_g_col):
    b, s, _ = qq.shape
    t = ATTN_T
    seqspec = pl.BlockSpec((1, s, V_HEAD_DIM), lambda bi, h: (bi, 0, h))
    vtspec = pl.BlockSpec((1, s // t, V_HEAD_DIM, t), lambda bi, h: (bi, 0, h, 0))
    const = lambda bi, h: (0, 0)
    return pl.pallas_call(
        _attn_kernel,
        out_shape=jax.ShapeDtypeStruct((b, s, DIFF_WIDTH), BF16),
        grid=(b, DIFF_HEADS),
        in_specs=[pl.BlockSpec(lamv.shape, const), pl.BlockSpec(subln_g_col.shape, const),
                  seqspec, seqspec, vtspec],
        out_specs=seqspec,
        scratch_shapes=[pltpu.VMEM((2, 2, t, t), F32)],
        compiler_params=pltpu.CompilerParams(
            dimension_semantics=("arbitrary",) * 2, vmem_limit_bytes=VMEM_LIMIT),
        name="attn",
    )(lamv, subln_g_col, qq, kk, vt)


def _mix_kernel(attn_ref, u_ref, vn_ref, x_ref, wsp_ref, bsp_ref, wo_ref, g1_ref, b1_ref,
                wrt_ref, br_ref,
                x1_ref, eid_ref, gate_ref, rank_ref, cnt_ref, tcnt_ref,
                cat_ref, carry_ref):
    tm = MOE_TM

    @pl.when(pl.program_id(0) == 0)
    def _():
        carry_ref[...] = jnp.zeros_like(carry_ref)

    ri = lax.broadcasted_iota(jnp.int32, (CHUNK, CHUNK), 0)
    ci = lax.broadcasted_iota(jnp.int32, (CHUNK, CHUNK), 1)
    tril = ci <= ri
    first = ci < GMLP_GROUP_DIM
    w = [jnp.where(tril, wsp_ref[g], 0.0).astype(BF16) for g in range(GMLP_GROUPS)]
    wr = wrt_ref[...]
    wh = wr.astype(BF16)
    wl = (wr - wh.astype(F32)).astype(BF16)
    eio = lax.broadcasted_iota(jnp.int32, (N_EXPERTS, tm), 0).astype(F32)
    ti = lax.broadcasted_iota(jnp.int32, (tm, tm), 0)
    tj = lax.broadcasted_iota(jnp.int32, (tm, tm), 1)
    before = jnp.where(ti < tj, 1.0, 0.0).astype(BF16)

    halves = [slice(h * tm, (h + 1) * tm) for h in range(MIX_SUB)]

    for rs in halves:
        cat_ref[rs, :DIFF_WIDTH] = attn_ref[rs, :]
    for c in range(MIX_SUB * tm // CHUNK):
        rows = slice(c * CHUNK, (c + 1) * CHUNK)
        for jb in range(GMLP_WIDTH // LANES):
            cols = slice(jb * LANES, (jb + 1) * LANES)
            vb = vn_ref[rows, cols]
            zero = jnp.zeros_like(vb)
            z = (jnp.dot(w[2 * jb], jnp.where(first, vb, zero), preferred_element_type=F32)
                 + jnp.dot(w[2 * jb + 1], jnp.where(first, zero, vb), preferred_element_type=F32))
            gated = u_ref[rows, cols].astype(F32) * (z + bsp_ref[:, cols])
            cat_ref[rows, DIFF_WIDTH + jb * LANES:DIFF_WIDTH + (jb + 1) * LANES] = gated.astype(BF16)

    x1s = []
    for rs in halves:
        mixed = jnp.dot(cat_ref[rs, :], wo_ref[...], preferred_element_type=F32)
        x1 = _layer_norm(DEEPNORM_ALPHA * x_ref[rs, :] + mixed, g1_ref[...], b1_ref[...])
        x1_ref[rs, :] = x1
        x1s.append(x1)

    curs = []
    for x1 in x1s:
        xh = x1.astype(BF16)
        xl = (x1 - xh.astype(F32)).astype(BF16)
        curs.append(_nt_dot(wh, xh) + _nt_dot(wl, xh) + _nt_dot(wh, xl) + br_ref[...])

    vals = [[] for _ in halves]
    idxs = [[] for _ in halves]
    sels = [[] for _ in halves]
    for _ in range(TOP_K):
        for h in range(MIX_SUB):
            mx = jnp.max(curs[h], axis=0, keepdims=True)
            idx = jnp.min(jnp.where(curs[h] == mx, eio, float(N_EXPERTS)), axis=0, keepdims=True)
            sel = eio == idx
            vals[h].append(mx)
            idxs[h].append(idx)
            sels[h].append(sel)
            curs[h] = jnp.where(sel, -jnp.inf, curs[h])

    carry = carry_ref[:, 0:1]
    for h, rs in enumerate(halves):
        ex = [jnp.exp(vk - vals[h][0]) for vk in vals[h]]
        denom = ex[0] + ex[1] + ex[2] + ex[3]
        gate_ref[:, rs] = jnp.concatenate([e / denom for e in ex], axis=0)
        eid_ref[:, rs] = jnp.concatenate(idxs[h], axis=0).astype(jnp.int32)

        chosen = (sels[h][0] | sels[h][1] | sels[h][2] | sels[h][3])
        onehot = jnp.where(chosen, 1.0, 0.0)
        cnt_before = jnp.dot(onehot.astype(BF16), before, preferred_element_type=F32) + carry
        ranks = [jnp.sum(jnp.where(s, cnt_before, 0.0), axis=0, keepdims=True) for s in sels[h]]
        rank_ref[:, rs] = jnp.concatenate(ranks, axis=0).astype(jnp.int32)
        tile_cnt = jnp.sum(onehot, axis=1, keepdims=True)
        tcnt_ref[h * N_EXPERTS:(h + 1) * N_EXPERTS, :] = jnp.broadcast_to(tile_cnt, (N_EXPERTS, LANES))
        carry = carry + tile_cnt

    carry_ref[...] = jnp.broadcast_to(carry, carry_ref.shape)
    cnt_ref[...] = jnp.broadcast_to(carry, cnt_ref.shape)


def _mix(attn, u, vn, x2, wsp, bsp, wo, g1, b1, wrt, br):
    n, d = x2.shape
    tm = MOE_TM * MIX_SUB
    row = lambda i: (i, 0)
    col = lambda i: (0, i)
    const2 = lambda i: (0, 0)
    const3 = lambda i: (0, 0, 0)
    tok = lambda dt: jax.ShapeDtypeStruct((TOP_K, n), dt)
    return pl.pallas_call(
        _mix_kernel,
        out_shape=(jax.ShapeDtypeStruct((n, d), F32), tok(jnp.int32), tok(F32), tok(jnp.int32),
                   jax.ShapeDtypeStruct((N_EXPERTS, LANES), F32),
                   jax.ShapeDtypeStruct((n // MOE_TM * N_EXPERTS, LANES), F32)),
        grid=(n // tm,),
        in_specs=[
            pl.BlockSpec((tm, DIFF_WIDTH), row),
            pl.BlockSpec((tm, GMLP_WIDTH), row),
            pl.BlockSpec((tm, GMLP_WIDTH), row),
            pl.BlockSpec((tm, d), row),
            pl.BlockSpec(wsp.shape, const3),
            pl.BlockSpec(bsp.shape, const2),
            pl.BlockSpec(wo.shape, const2),
            pl.BlockSpec(g1.shape, const2),
            pl.BlockSpec(b1.shape, const2),
            pl.BlockSpec(wrt.shape, const2),
            pl.BlockSpec(br.shape, const2),
        ],
        out_specs=[
            pl.BlockSpec((tm, d), row),
            pl.BlockSpec((TOP_K, tm), col),
            pl.BlockSpec((TOP_K, tm), col),
            pl.BlockSpec((TOP_K, tm), col),
            pl.BlockSpec((N_EXPERTS, LANES), const2),
            pl.BlockSpec((MIX_SUB * N_EXPERTS, LANES), row),
        ],
        scratch_shapes=[pltpu.VMEM((tm, DIFF_WIDTH + GMLP_WIDTH), BF16),
                        pltpu.VMEM((N_EXPERTS, LANES), F32)],
        compiler_params=pltpu.CompilerParams(
            dimension_semantics=("arbitrary",), vmem_limit_bytes=VMEM_LIMIT),
        name="mix",
    )(attn, u, vn, x2, wsp, bsp, wo, g1, b1, wrt, br)


def _for_each_run(tile, tcnt_ref, off_ref, dst_ref, make_copy):
    def per_expert(e, carry):
        n = tcnt_ref[tile * N_EXPERTS + e]
        off = off_ref[tile * N_EXPERTS + e]
        dst = dst_ref[tile * N_EXPERTS + e]
        def piece(bit):
            done = (n >> (bit + 1)) << (bit + 1)

            @pl.when(((n >> bit) & 1) == 1)
            def _():
                make_copy(off + done, dst + done, 1 << bit).start()

        for bit in range(RUN_SMALL_BITS):
            piece(bit)

        @pl.when(n >= (1 << RUN_SMALL_BITS))
        def _():
            for bit in range(RUN_SMALL_BITS, RUN_BITS):
                piece(bit)
        return carry

    lax.fori_loop(0, N_EXPERTS, per_expert, 0)


def _dispatch_kernel(tcnt_ref, off_ref, dst_ref, pad_start_ref, pad_len_ref, nused_ref,
                     pos_ref, x1_ref, xs_hbm, sorted_ref, zero_ref, sem, zsem):
    tm = MOE_TM
    rows = TOP_K * tm
    d = x1_ref.shape[1]
    sub = d // LANES
    nblk = xs_hbm.shape[0] // (FFN_TM * sub)
    i = pl.program_id(0)
    last = pl.num_programs(0) - 1
    slot = lax.rem(i, 2)

    def wait_runs(s):
        pltpu.make_async_copy(sorted_ref.at[pl.ds(s * rows * sub, rows * sub)],
                              xs_hbm.at[pl.ds(0, rows * sub)], sem.at[s]).wait()

    @pl.when(i >= 2)
    def _():
        wait_runs(slot)

    pos = pos_ref[...]
    ri = lax.broadcasted_iota(jnp.int32, (rows, tm), 0)
    hit = (ri == pos[0:1]) | (ri == pos[1:2]) | (ri == pos[2:3]) | (ri == pos[3:4])
    perm = jnp.where(hit, 1.0, 0.0).astype(BF16)
    srt = jnp.dot(perm, x1_ref[...].astype(BF16), preferred_element_type=F32)
    base = pl.multiple_of(slot * (rows * sub), rows * sub)
    _rows_to_slabs(sorted_ref, base, srt, sub)

    def run_copy(local_row, global_row, nrows):
        return pltpu.make_async_copy(
            sorted_ref.at[pl.ds(pl.multiple_of((slot * rows + local_row) * sub, sub), nrows * sub)],
            xs_hbm.at[pl.ds(pl.multiple_of(global_row * sub, sub), nrows * sub)], sem.at[slot])

    _for_each_run(i, tcnt_ref, off_ref, dst_ref, run_copy)

    @pl.when(i == 0)
    def _():
        zero_ref[...] = jnp.zeros_like(zero_ref)

        def zero_copy(r, nrows):
            return pltpu.make_async_copy(
                zero_ref.at[pl.ds(0, nrows * sub)],
                xs_hbm.at[pl.ds(pl.multiple_of(r * sub, sub), nrows * sub)], zsem)

        def tail_copy(blk):
            return pltpu.make_async_copy(
                zero_ref, xs_hbm.at[pl.ds(pl.multiple_of(blk * (FFN_TM * sub), FFN_TM * sub),
                                          FFN_TM * sub)], zsem)

        def tail_start(blk, c):
            tail_copy(blk).start()
            return c

        def tail_wait(blk, c):
            tail_copy(blk).wait()
            return c

        lax.fori_loop(nused_ref[0], nblk, tail_start, 0)
        lax.fori_loop(nused_ref[0], nblk, tail_wait, 0)

        def per_expert(e, _):
            s = pad_start_ref[e]
            cnt = pad_len_ref[e]
            for wait in (False, True):
                for bit in range(FFN_TM.bit_length() - 1):
                    done = (cnt >> (bit + 1)) << (bit + 1)

                    @pl.when(((cnt >> bit) & 1) == 1)
                    def _():
                        piece = zero_copy(s + done, 1 << bit)
                        piece.wait() if wait else piece.start()
            return 0

        lax.fori_loop(0, N_EXPERTS, per_expert, 0)

    @pl.when(i == last)
    def _():
        wait_runs(slot)

        @pl.when(last >= 1)
        def _():
            wait_runs(1 - slot)


def _dispatch(x1, pos, tcnt, off, dst, pad_start, pad_len, nused, p_rows):
    n, d = x1.shape
    tm = MOE_TM
    sub = d // LANES
    return pl.pallas_call(
        _dispatch_kernel,
        out_shape=jax.ShapeDtypeStruct((p_rows * sub, LANES), F32),
        grid_spec=pltpu.PrefetchScalarGridSpec(
            num_scalar_prefetch=6,
            grid=(n // tm,),
            in_specs=[
                pl.BlockSpec((TOP_K, tm), lambda i, *_: (0, i)),
                pl.BlockSpec((tm, d), lambda i, *_: (i, 0)),
            ],
            out_specs=pl.BlockSpec(memory_space=pl.ANY),
            scratch_shapes=[pltpu.VMEM((2 * TOP_K * tm * sub, LANES), F32),
                            pltpu.VMEM((FFN_TM * sub, LANES), F32),
                            pltpu.SemaphoreType.DMA((2,)), pltpu.SemaphoreType.DMA(())],
        ),
        compiler_params=pltpu.CompilerParams(
            dimension_semantics=("arbitrary",), vmem_limit_bytes=VMEM_LIMIT),
        name="dispatch",
    )(tcnt, off, dst, pad_start, pad_len, nused, pos, x1)


def _ffn_kernel(blk_e_ref, nused_ref, next_e_ref, valid_ref, xs_ref, wup_hbm, bup_ref, wdn_hbm,
                bdn_ref, y_ref, wup_f, wdn_f, wup_b, wdn_b, wsem):
    tm = FFN_TM
    dff = wdn_f.shape[0]
    sub = wdn_f.shape[1] // LANES
    i = pl.program_id(0)
    used = i < nused_ref[0]

    @pl.when(jnp.logical_not(used))
    def _():
        y_ref[...] = jnp.zeros_like(y_ref)

    def weight_copies(e):
        return (pltpu.make_async_copy(wup_hbm.at[e], wup_f, wsem.at[0]),
                pltpu.make_async_copy(wdn_hbm.at[e], wdn_f, wsem.at[1]))

    e = blk_e_ref[i]
    new_expert = jnp.logical_or(i == 0, e != blk_e_ref[jnp.maximum(i - 1, 0)])

    @pl.when(jnp.logical_and(used, new_expert))
    def _():
        @pl.when(i == 0)
        def _():
            for cp in weight_copies(e):
                cp.start()

        for cp in weight_copies(e):
            cp.wait()
        for c in range(0, 2 * dff, WEIGHT_CAST_COLS):
            wup_b[:, c:c + WEIGHT_CAST_COLS] = wup_f[:, c:c + WEIGHT_CAST_COLS].astype(BF16)
        for c in range(0, wdn_f.shape[1], WEIGHT_CAST_COLS):
            wdn_b[:, c:c + WEIGHT_CAST_COLS] = wdn_f[:, c:c + WEIGHT_CAST_COLS].astype(BF16)

        @pl.when(next_e_ref[e] != e)
        def _():
            for cp in weight_copies(next_e_ref[e]):
                cp.start()

    def expert_mlp(nrows):
        xb = _slabs_to_rows(xs_ref, 0, nrows, sub).astype(BF16)
        h = jnp.dot(xb, wup_b[...], preferred_element_type=F32) + bup_ref[0]
        gate = jnp.minimum(h[:, :dff], SWIGLU_LIMIT)
        lin = jnp.clip(h[:, dff:], -SWIGLU_LIMIT, SWIGLU_LIMIT)
        act = (lin + 1.0) * gate * jax.nn.sigmoid(SWIGLU_ALPHA * gate)
        y = jnp.dot(act.astype(BF16), wdn_b[...], preferred_element_type=F32) + bdn_ref[0]
        _rows_to_slabs(y_ref, 0, y, sub)

    half = tm // 2
    more_than_half = valid_ref[i] > half

    @pl.when(jnp.logical_and(used, more_than_half))
    def _():
        expert_mlp(tm)

    @pl.when(jnp.logical_and(used, jnp.logical_not(more_than_half)))
    def _():
        expert_mlp(half)
        y_ref[half * sub:, :] = jnp.zeros((half * sub, LANES), F32)


def _ffn(xs, blk_e, nused, next_e, blk_valid, wup, bup, wdn, bdn):
    tm = FFN_TM
    e, d, dff2 = wup.shape
    dff = dff2 // 2
    sub = d // LANES
    rowblk = lambda i, be, nu, ne, bv: (i, 0)
    bsel = lambda i, be, nu, ne, bv: (be[i], 0, 0)
    return pl.pallas_call(
        _ffn_kernel,
        out_shape=jax.ShapeDtypeStruct(xs.shape, F32),
        grid_spec=pltpu.PrefetchScalarGridSpec(
            num_scalar_prefetch=4,
            grid=(xs.shape[0] // (tm * sub),),
            in_specs=[
                pl.BlockSpec((tm * sub, LANES), rowblk),
                pl.BlockSpec(memory_space=pl.ANY),
                pl.BlockSpec((1, 1, dff2), bsel),
                pl.BlockSpec(memory_space=pl.ANY),
                pl.BlockSpec((1, 1, d), bsel),
            ],
            out_specs=pl.BlockSpec((tm * sub, LANES), rowblk),
            scratch_shapes=[pltpu.VMEM((d, dff2), F32), pltpu.VMEM((dff, d), F32),
                            pltpu.VMEM((d, dff2), BF16), pltpu.VMEM((dff, d), BF16),
                            pltpu.SemaphoreType.DMA((2,))],
        ),
        compiler_params=pltpu.CompilerParams(
            dimension_semantics=("arbitrary",), vmem_limit_bytes=VMEM_LIMIT),
        name="ffn",
    )(blk_e, nused, next_e, blk_valid, xs, wup, bup, wdn, bdn)


def _combine_kernel(tcnt_ref, off_ref, dst_ref, pos_ref, gate_ref, x1_ref, g2_ref, b2_ref, y_hbm,
                    o_ref, stage_ref, sem):
    tm = MOE_TM
    rows = TOP_K * tm
    sub = x1_ref.shape[1] // LANES
    i = pl.program_id(0)
    nsteps = pl.num_programs(0)
    slot = lax.rem(i, 2)

    def stage_row(s, h, local_row):
        return pl.multiple_of(((s * COMBINE_SUB + h) * rows + local_row) * sub, sub)

    def fetch_runs(step, s):
        for h in range(COMBINE_SUB):
            def run_copy(local_row, global_row, nrows, h=h):
                return pltpu.make_async_copy(
                    y_hbm.at[pl.ds(pl.multiple_of(global_row * sub, sub), nrows * sub)],
                    stage_ref.at[pl.ds(stage_row(s, h, local_row), nrows * sub)], sem.at[s])
            _for_each_run(step * COMBINE_SUB + h, tcnt_ref, off_ref, dst_ref, run_copy)

    @pl.when(i == 0)
    def _():
        fetch_runs(0, 0)

    @pl.when(i + 1 < nsteps)
    def _():
        fetch_runs(i + 1, 1 - slot)

    pltpu.make_async_copy(y_hbm.at[pl.ds(0, COMBINE_SUB * rows * sub)],
                          stage_ref.at[pl.ds(stage_row(slot, 0, 0), COMBINE_SUB * rows * sub)],
                          sem.at[slot]).wait()

    tiles = range(COMBINE_SUB)
    ci = lax.broadcasted_iota(jnp.int32, (tm, rows), 1)
    ws = []
    for h in tiles:
        pos = pos_ref[h * tm:(h + 1) * tm, :]
        gates = gate_ref[h * tm:(h + 1) * tm, :]
        w = jnp.where(ci == pos[:, 0:1], gates[:, 0:1], 0.0)
        for k in range(1, TOP_K):
            w = w + jnp.where(ci == pos[:, k:k + 1], gates[:, k:k + 1], 0.0)
        ws.append(w)
    wh = [w.astype(BF16) for w in ws]
    wl = [(w - hi.astype(F32)).astype(BF16) for w, hi in zip(ws, wh)]
    ys = [_slabs_to_rows(stage_ref, stage_row(slot, h, 0), rows, sub) for h in tiles]
    yh = [y.astype(BF16) for y in ys]
    yl = [(y - hi.astype(F32)).astype(BF16) for y, hi in zip(ys, yh)]
    for h in tiles:
        ffn = (jnp.dot(wh[h], yh[h], preferred_element_type=F32)
               + jnp.dot(wl[h], yh[h], preferred_element_type=F32)
               + jnp.dot(wh[h], yl[h], preferred_element_type=F32))
        rs = slice(h * tm, (h + 1) * tm)
        o_ref[rs, :] = _layer_norm(DEEPNORM_ALPHA * x1_ref[rs, :] + ffn, g2_ref[...], b2_ref[...])


def _combine(y, pos_tok, gates_tok, tcnt, off, dst, x1, g2, b2):
    n, d = x1.shape
    tm = MOE_TM * COMBINE_SUB
    sub = d // LANES
    row = lambda i, *_: (i, 0)
    const = lambda i, *_: (0, 0)
    return pl.pallas_call(
        _combine_kernel,
        out_shape=jax.ShapeDtypeStruct((n, d), F32),
        grid_spec=pltpu.PrefetchScalarGridSpec(
            num_scalar_prefetch=3,
            grid=(n // tm,),
            in_specs=[
                pl.BlockSpec((tm, TOP_K), row),
                pl.BlockSpec((tm, TOP_K), row),
                pl.BlockSpec((tm, d), row),
                pl.BlockSpec(g2.shape, const),
                pl.BlockSpec(b2.shape, const),
                pl.BlockSpec(memory_space=pl.ANY),
            ],
            out_specs=pl.BlockSpec((tm, d), row),
            scratch_shapes=[pltpu.VMEM((2 * TOP_K * tm * sub, LANES), F32),
                            pltpu.SemaphoreType.DMA((2,))],
        ),
        compiler_params=pltpu.CompilerParams(
            dimension_semantics=("arbitrary",), vmem_limit_bytes=VMEM_LIMIT),
        name="combine",
    )(tcnt, off, dst, pos_tok, gates_tok, x1, g2, b2, y)


def _head_pairs(w_a, w_b):
    d = w_a.shape[0]
    a = w_a.reshape(d, DIFF_HEADS, DIFF_HEAD_DIM)
    b = w_b.reshape(d, DIFF_HEADS, DIFF_HEAD_DIM)
    return jnp.concatenate([a, b], axis=-1).reshape(d, DIFF_HEADS * V_HEAD_DIM)


def _lookup(table, eid):
    eids = jnp.arange(N_EXPERTS, dtype=jnp.int32)[:, None, None]
    return jnp.sum(jnp.where(eid[None] == eids, table[:, None, :], 0), axis=0)


def kernel(x, w_in, lambda_q1, lambda_k1, lambda_q2, lambda_k2, subln_g, gmlp_ln_g, gmlp_ln_b,
           w_spatial, b_spatial, w_o, ln1_g, ln1_b, w_router, b_router, w_up, b_up,
           w_down, b_down, ln2_g, ln2_b):
    b, s, d = x.shape
    n = b * s
    x2 = x.reshape(n, d)

    w = w_in[0]
    c = QK_WIDTH
    scale = DIFF_HEAD_DIM ** -0.5 * LOG2_E
    wq = _head_pairs(w[:, 0:c], w[:, c:2 * c]) * scale
    wk = _head_pairs(w[:, 2 * c:3 * c], w[:, 3 * c:4 * c])
    wqk = jnp.concatenate([wq, wk], axis=1).astype(BF16)
    wvt = w[:, 4 * c:4 * c + DIFF_WIDTH].T.astype(BF16)
    wg = w[:, 4 * c + DIFF_WIDTH:].astype(BF16)
    lamv = jnp.concatenate([lambda_q1, lambda_k1, lambda_q2, lambda_k2], axis=0)
    bsp = jnp.repeat(b_spatial[0].T, GMLP_GROUP_DIM, axis=1)

    assert PROJ_TM == ATTN_T and s % ATTN_T == 0 and n % (MOE_TM * MIX_SUB) == 0
    qq, kk, vt, u, vn = _proj(x2, wqk, wvt, wg, gmlp_ln_g, gmlp_ln_b)
    attn = _attention(qq.reshape(b, s, -1), kk.reshape(b, s, -1),
                      vt.reshape(b, s // ATTN_T, DIFF_WIDTH, ATTN_T),
                      lamv, subln_g.reshape(V_HEAD_DIM, 1)).reshape(n, DIFF_WIDTH)
    x1, eid, gates, rank, cnt, tcnt_l = _mix(attn, u, vn, x2, w_spatial[0], bsp,
                                             w_o[0].astype(BF16), ln1_g, ln1_b,
                                             w_router[0].T, b_router[0][:, None])

    ntiles = n // MOE_TM
    counts = cnt[:, 0].astype(jnp.int32)
    padded = ((counts + FFN_TM - 1) // FFN_TM) * FFN_TM
    end_padded = jnp.cumsum(padded)
    start_padded = end_padded - padded
    tcnt = tcnt_l.reshape(ntiles, N_EXPERTS, LANES)[:, :, 0].astype(jnp.int32)
    before = jnp.cumsum(tcnt, axis=0) - tcnt
    off = jnp.cumsum(tcnt, axis=1) - tcnt
    dst = start_padded[None, :] + before
    tile_base = jnp.repeat((off - before).T, MOE_TM, axis=1)
    pos = _lookup(tile_base, eid) + rank

    p_rows = n * TOP_K + N_EXPERTS * FFN_TM
    nblk = p_rows // FFN_TM
    nused = (end_padded[-1:] // FFN_TM).astype(jnp.int32)
    blk_start = jnp.arange(nblk, dtype=jnp.int32) * FFN_TM
    blk_e = jnp.minimum(jnp.sum((end_padded[None, :] <= blk_start[:, None]).astype(jnp.int32), axis=1),
                        N_EXPERTS - 1)
    tcnt_f, off_f, dst_f = tcnt.reshape(-1), off.reshape(-1), dst.reshape(-1)

    xs = _dispatch(x1, pos, tcnt_f, off_f, dst_f, start_padded + counts, padded - counts, nused,
                   p_rows)
    e_ids = jnp.arange(N_EXPERTS, dtype=jnp.int32)
    later_used = jnp.logical_and(e_ids[None, :] > e_ids[:, None], padded[None, :] > 0)
    next_e = jnp.min(jnp.where(later_used, e_ids[None, :], N_EXPERTS), axis=1)
    next_e = jnp.where(next_e == N_EXPERTS, e_ids, next_e)
    is_e = blk_e[:, None] == e_ids[None, :]
    filled = jnp.sum(jnp.where(is_e, (start_padded + counts)[None, :], 0), axis=1) - blk_start
    blk_valid = jnp.clip(filled, 0, FFN_TM).astype(jnp.int32)
    y = _ffn(xs, blk_e, nused, next_e, blk_valid, w_up[0], b_up[0][:, None, :], w_down[0],
             b_down[0][:, None, :])
    out = _combine(y, pos.T, gates.T, tcnt_f, off_f, dst_f, x1, ln2_g, ln2_b)
    return out.reshape(b, s, d)
```

```python
import jax
import jax.numpy as jnp
from jax import lax
from jax.experimental import pallas as pl
from jax.experimental.pallas import tpu as pltpu

DIFF_HEADS = 4
DIFF_HEAD_DIM = 64
V_HEAD_DIM = 2 * DIFF_HEAD_DIM
QK_WIDTH = DIFF_HEADS * DIFF_HEAD_DIM
DIFF_WIDTH = DIFF_HEADS * V_HEAD_DIM
GMLP_GROUPS = 8
GMLP_GROUP_DIM = 64
GMLP_WIDTH = GMLP_GROUPS * GMLP_GROUP_DIM
CHUNK = 128
N_EXPERTS = 32
TOP_K = 4
SWIGLU_LIMIT = 7.0
SWIGLU_ALPHA = 1.702
LN_EPS = 1e-5
DEPTH = 1
DEEPNORM_ALPHA = (2.0 * DEPTH) ** 0.25
LAMBDA_INIT = 0.8 - 0.6 * 1.0
LOG2_E = 1.4426950408889634

LANES = 128

PROJ_TM = 512
PROJ_CHUNK = 256
ATTN_T = 512
ATTN_GROUP = 4
MOE_TM = 256
MIX_SUB = 4
COMBINE_SUB = 2
DISPATCH_SUB = 2
FFN_TM = 512
WEIGHT_CAST_COLS = 256
RUN_BITS = MOE_TM.bit_length()
VMEM_LIMIT = 48 * 1024 * 1024

BF16 = jnp.bfloat16
F32 = jnp.float32


def _layer_norm(y, g, b):
    mu = jnp.mean(y, axis=-1, keepdims=True)
    yc = y - mu
    var = jnp.mean(yc * yc, axis=-1, keepdims=True)
    return yc * lax.rsqrt(var + LN_EPS) * g + b


def _gelu(x):
    return 0.5 * x * (1.0 + lax.erf(x * (2.0 ** -0.5)))


def _nt_dot(a, b):
    return lax.dot_general(a, b, (((1,), (1,)), ((), ())), preferred_element_type=F32)


def _slabs_to_rows(ref, first, nrows, sub):
    return jnp.concatenate([ref[pl.ds(first + j, nrows, stride=sub), :] for j in range(sub)], axis=1)


def _rows_to_slabs(ref, first, rows, sub):
    for j in range(sub):
        ref[pl.ds(first + j, rows.shape[0], stride=sub), :] = rows[:, j * LANES:(j + 1) * LANES]


def _proj_kernel(x_ref, wqk_ref, wvt_ref, wg_ref, lng_ref, lnb_ref,
                 qq_ref, kk_ref, vt_ref, u_ref, vn_ref):
    xb = x_ref[...].astype(BF16)

    def gelu_cols(lo):
        return _gelu(jnp.dot(xb, wg_ref[:, lo:lo + PROJ_CHUNK], preferred_element_type=F32))

    gv = [gelu_cols(GMLP_WIDTH + c) for c in range(0, GMLP_WIDTH, PROJ_CHUNK)]
    qk = jnp.dot(xb, wqk_ref[...], preferred_element_type=F32)
    qq_ref[...] = qk[:, :DIFF_WIDTH].astype(BF16)
    kk_ref[...] = qk[:, DIFF_WIDTH:].astype(BF16)
    for c in range(0, GMLP_WIDTH, PROJ_CHUNK):
        u_ref[:, c:c + PROJ_CHUNK] = gelu_cols(c).astype(BF16)
    vt_ref[0] = _nt_dot(wvt_ref[...], xb).astype(BF16)
    vn_ref[...] = _layer_norm(jnp.concatenate(gv, axis=1), lng_ref[...], lnb_ref[...]).astype(BF16)


def _proj(x2, wqk, wvt, wg, lng, lnb):
    n, d = x2.shape
    tm = PROJ_TM
    row = lambda i: (i, 0)
    const = lambda i: (0, 0)
    out = jax.ShapeDtypeStruct((n, DIFF_WIDTH), BF16)
    vt_out = jax.ShapeDtypeStruct((n // tm, DIFF_WIDTH, tm), BF16)
    rowspec = pl.BlockSpec((tm, DIFF_WIDTH), row)
    return pl.pallas_call(
        _proj_kernel,
        out_shape=(out, out, vt_out, out, out),
        grid=(n // tm,),
        in_specs=[
            pl.BlockSpec((tm, d), row),
            pl.BlockSpec(wqk.shape, const),
            pl.BlockSpec(wvt.shape, const),
            pl.BlockSpec(wg.shape, const),
            pl.BlockSpec(lng.shape, const),
            pl.BlockSpec(lnb.shape, const),
        ],
        out_specs=[rowspec, rowspec, pl.BlockSpec((1, DIFF_WIDTH, tm), lambda i: (i, 0, 0)),
                   rowspec, rowspec],
        compiler_params=pltpu.CompilerParams(
            dimension_semantics=("arbitrary",), vmem_limit_bytes=VMEM_LIMIT),
        name="proj",
    )(x2, wqk, wvt, wg, lng, lnb)


def _attn_kernel(lamv_ref, g_ref, qq_ref, kk_ref, vt_ref, o_ref, s_ref):
    t = ATTN_T
    nq = qq_ref.shape[1] // t
    starts = (0, 1, 1, 0)

    def group(g, carry):
        for r in range(ATTN_GROUP):
            _attn_q_tile(ATTN_GROUP * g + r, nq, r % 2, starts[r],
                         lamv_ref, g_ref, qq_ref, kk_ref, vt_ref, o_ref, s_ref)
        return carry

    q1, q2 = _masked_queries(qq_ref, 0)
    k0 = kk_ref[0, 0:t, :]
    s_ref[0, 0] = _nt_dot(k0, q1)
    s_ref[0, 1] = _nt_dot(k0, q2)
    lax.fori_loop(0, nq // ATTN_GROUP, group, 0)


def _masked_queries(qq_ref, qi):
    t = ATTN_T
    q = qq_ref[0, pl.ds(pl.multiple_of(qi * t, t), t), :]
    first = lax.broadcasted_iota(jnp.int32, q.shape, 1) < DIFF_HEAD_DIM
    zero = jnp.zeros_like(q)
    return jnp.where(first, q, zero), jnp.where(first, zero, q)


def _attn_q_tile(qi, nq, odd, start, lamv_ref, g_ref, qq_ref, kk_ref, vt_ref, o_ref, s_ref):
    t = ATTN_T
    q_rows = pl.ds(pl.multiple_of(qi * t, t), t)
    q1, q2 = _masked_queries(qq_ref, qi)
    next_q = _masked_queries(qq_ref, jnp.minimum(qi + 1, nq - 1))

    def keys(j):
        return kk_ref[0, pl.ds(pl.multiple_of(j * t, t), t), :]

    def consume(j, buf, carry, masked, nxt):
        vt = vt_ref[0, j]
        k_next = keys(nxt[0])
        new = []
        for mp, ((m, l, acc), qh) in enumerate(zip(carry, nxt[2])):
            s_ref[nxt[1], mp] = _nt_dot(k_next, qh)
            s = s_ref[buf, mp]
            if masked:
                key = lax.broadcasted_iota(jnp.int32, s.shape, 0)
                qry = lax.broadcasted_iota(jnp.int32, s.shape, 1)
                s = jnp.where(key <= qry, s, -jnp.inf)
            m_new = jnp.maximum(m, jnp.max(s, axis=0, keepdims=True))
            alpha = jnp.exp2(m - m_new)
            p = jnp.exp2(s - m_new)
            l_new = alpha * l + jnp.sum(p, axis=0, keepdims=True)
            acc_new = alpha * acc + jnp.dot(vt, p.astype(BF16), preferred_element_type=F32)
            new.append((m_new, l_new, acc_new))
        return tuple(new)

    def init():
        return (jnp.full((1, t), -jnp.inf, F32), jnp.zeros((1, t), F32),
                jnp.zeros((V_HEAD_DIM, t), F32))

    def finish(carry):
        (_, l1, acc1), (_, l2, acc2) = carry
        lv = lamv_ref[...]
        lam = (jnp.exp(jnp.sum(lv[0:1] * lv[1:2], axis=-1, keepdims=True))
               - jnp.exp(jnp.sum(lv[2:3] * lv[3:4], axis=-1, keepdims=True)) + LAMBDA_INIT)
        o = acc1 / l1 - lam * (acc2 / l2)
        ms = jnp.mean(o * o, axis=0, keepdims=True)
        o = o * lax.rsqrt(ms + LN_EPS) * g_ref[...] * (1.0 - LAMBDA_INIT)
        o_ref[0, q_rows, :] = o.T.astype(BF16)

    own = (q1, q2)
    a, b = start, 1 - start

    def pair(jj, c):
        j = 2 * jj
        c = consume(j, a, c, False, (j + 1, b, own))
        return consume(j + 1, b, c, False, (j + 2, a, own))

    carry = lax.fori_loop(0, qi // 2, pair, (init(), init()))
    if odd:
        carry = consume(qi - 1, a, carry, False, (qi, b, own))
        finish(consume(qi, b, carry, True, (0, a, next_q)))
    else:
        finish(consume(qi, a, carry, True, (0, b, next_q)))


def _attention(qq, kk, vt, lamv, subln_g_col):
    b, s, _ = qq.shape
    t = ATTN_T
    seqspec = pl.BlockSpec((1, s, V_HEAD_DIM), lambda bi, h: (bi, 0, h))
    vtspec = pl.BlockSpec((1, s // t, V_HEAD_DIM, t), lambda bi, h: (bi, 0, h, 0))
    const = lambda bi, h: (0, 0)
    return pl.pallas_call(
        _attn_kernel,
        out_shape=jax.ShapeDtypeStruct((b, s, DIFF_WIDTH), BF16),
        grid=(b, DIFF_HEADS),
        in_specs=[pl.BlockSpec(lamv.shape, const), pl.BlockSpec(subln_g_col.shape, const),
                  seqspec, seqspec, vtspec],
        out_specs=seqspec,
        scratch_shapes=[pltpu.VMEM((2, 2, t, t), F32)],
        compiler_params=pltpu.CompilerParams(
            dimension_semantics=("arbitrary",) * 2, vmem_limit_bytes=VMEM_LIMIT),
        name="attn",
    )(lamv, subln_g_col, qq, kk, vt)


def _mix_kernel(attn_ref, u_ref, vn_ref, x_ref, wsp_ref, bsp_ref, wo_ref, g1_ref, b1_ref,
                wrt_ref, br_ref,
                x1_ref, eid_ref, gate_ref, rank_ref, cnt_ref, tcnt_ref,
                cat_ref, carry_ref):
    tm = MOE_TM

    @pl.when(pl.program_id(0) == 0)
    def _():
        carry_ref[...] = jnp.zeros_like(carry_ref)

    ri = lax.broadcasted_iota(jnp.int32, (CHUNK, CHUNK), 0)
    ci = lax.broadcasted_iota(jnp.int32, (CHUNK, CHUNK), 1)
    tril = ci <= ri
    first = ci < GMLP_GROUP_DIM
    w = [jnp.where(tril, wsp_ref[g], 0.0).astype(BF16) for g in range(GMLP_GROUPS)]
    wr = wrt_ref[...]
    wh = wr.astype(BF16)
    wl = (wr - wh.astype(F32)).astype(BF16)
    eio = lax.broadcasted_iota(jnp.int32, (N_EXPERTS, tm), 0).astype(F32)
    ti = lax.broadcasted_iota(jnp.int32, (tm, tm), 0)
    tj = lax.broadcasted_iota(jnp.int32, (tm, tm), 1)
    before = jnp.where(ti < tj, 1.0, 0.0).astype(BF16)

    halves = [slice(h * tm, (h + 1) * tm) for h in range(MIX_SUB)]

    for rs in halves:
        cat_ref[rs, :DIFF_WIDTH] = attn_ref[rs, :]
    for c in range(MIX_SUB * tm // CHUNK):
        rows = slice(c * CHUNK, (c + 1) * CHUNK)
        for jb in range(GMLP_WIDTH // LANES):
            cols = slice(jb * LANES, (jb + 1) * LANES)
            vb = vn_ref[rows, cols]
            zero = jnp.zeros_like(vb)
            z = (jnp.dot(w[2 * jb], jnp.where(first, vb, zero), preferred_element_type=F32)
                 + jnp.dot(w[2 * jb + 1], jnp.where(first, zero, vb), preferred_element_type=F32))
            gated = u_ref[rows, cols].astype(F32) * (z + bsp_ref[:, cols])
            cat_ref[rows, DIFF_WIDTH + jb * LANES:DIFF_WIDTH + (jb + 1) * LANES] = gated.astype(BF16)

    x1s = []
    for rs in halves:
        mixed = jnp.dot(cat_ref[rs, :], wo_ref[...], preferred_element_type=F32)
        x1 = _layer_norm(DEEPNORM_ALPHA * x_ref[rs, :] + mixed, g1_ref[...], b1_ref[...])
        x1_ref[rs, :] = x1
        x1s.append(x1)

    curs = []
    for x1 in x1s:
        xh = x1.astype(BF16)
        xl = (x1 - xh.astype(F32)).astype(BF16)
        curs.append(_nt_dot(wh, xh) + _nt_dot(wl, xh) + _nt_dot(wh, xl) + br_ref[...])

    vals = [[] for _ in halves]
    idxs = [[] for _ in halves]
    sels = [[] for _ in halves]
    for _ in range(TOP_K):
        for h in range(MIX_SUB):
            mx = jnp.max(curs[h], axis=0, keepdims=True)
            idx = jnp.min(jnp.where(curs[h] == mx, eio, float(N_EXPERTS)), axis=0, keepdims=True)
            sel = eio == idx
            vals[h].append(mx)
            idxs[h].append(idx)
            sels[h].append(sel)
            curs[h] = jnp.where(sel, -jnp.inf, curs[h])

    carry = carry_ref[:, 0:1]
    for h, rs in enumerate(halves):
        ex = [jnp.exp(vk - vals[h][0]) for vk in vals[h]]
        denom = ex[0] + ex[1] + ex[2] + ex[3]
        gate_ref[:, rs] = jnp.concatenate([e / denom for e in ex], axis=0)
        eid_ref[:, rs] = jnp.concatenate(idxs[h], axis=0).astype(jnp.int32)

        chosen = (sels[h][0] | sels[h][1] | sels[h][2] | sels[h][3])
        onehot = jnp.where(chosen, 1.0, 0.0)
        cnt_before = jnp.dot(onehot.astype(BF16), before, preferred_element_type=F32) + carry
        ranks = [jnp.sum(jnp.where(s, cnt_before, 0.0), axis=0, keepdims=True) for s in sels[h]]
        rank_ref[:, rs] = jnp.concatenate(ranks, axis=0).astype(jnp.int32)
        tile_cnt = jnp.sum(onehot, axis=1, keepdims=True)
        tcnt_ref[h * N_EXPERTS:(h + 1) * N_EXPERTS, :] = jnp.broadcast_to(tile_cnt, (N_EXPERTS, LANES))
        carry = carry + tile_cnt

    carry_ref[...] = jnp.broadcast_to(carry, carry_ref.shape)
    cnt_ref[...] = jnp.broadcast_to(carry, cnt_ref.shape)


def _mix(attn, u, vn, x2, wsp, bsp, wo, g1, b1, wrt, br):
    n, d = x2.shape
    tm = MOE_TM * MIX_SUB
    row = lambda i: (i, 0)
    col = lambda i: (0, i)
    const2 = lambda i: (0, 0)
    const3 = lambda i: (0, 0, 0)
    tok = lambda dt: jax.ShapeDtypeStruct((TOP_K, n), dt)
    return pl.pallas_call(
        _mix_kernel,
        out_shape=(jax.ShapeDtypeStruct((n, d), F32), tok(jnp.int32), tok(F32), tok(jnp.int32),
                   jax.ShapeDtypeStruct((N_EXPERTS, LANES), F32),
                   jax.ShapeDtypeStruct((n // MOE_TM * N_EXPERTS, LANES), F32)),
        grid=(n // tm,),
        in_specs=[
            pl.BlockSpec((tm, DIFF_WIDTH), row),
            pl.BlockSpec((tm, GMLP_WIDTH), row),
            pl.BlockSpec((tm, GMLP_WIDTH), row),
            pl.BlockSpec((tm, d), row),
            pl.BlockSpec(wsp.shape, const3),
            pl.BlockSpec(bsp.shape, const2),
            pl.BlockSpec(wo.shape, const2),
            pl.BlockSpec(g1.shape, const2),
            pl.BlockSpec(b1.shape, const2),
            pl.BlockSpec(wrt.shape, const2),
            pl.BlockSpec(br.shape, const2),
        ],
        out_specs=[
            pl.BlockSpec((tm, d), row),
            pl.BlockSpec((TOP_K, tm), col),
            pl.BlockSpec((TOP_K, tm), col),
            pl.BlockSpec((TOP_K, tm), col),
            pl.BlockSpec((N_EXPERTS, LANES), const2),
            pl.BlockSpec((MIX_SUB * N_EXPERTS, LANES), row),
        ],
        scratch_shapes=[pltpu.VMEM((tm, DIFF_WIDTH + GMLP_WIDTH), BF16),
                        pltpu.VMEM((N_EXPERTS, LANES), F32)],
        compiler_params=pltpu.CompilerParams(
            dimension_semantics=("arbitrary",), vmem_limit_bytes=VMEM_LIMIT),
        name="mix",
    )(attn, u, vn, x2, wsp, bsp, wo, g1, b1, wrt, br)


def _for_each_run(tile, tcnt_ref, off_ref, dst_ref, make_copy):
    def per_expert(e, carry):
        n = tcnt_ref[tile * N_EXPERTS + e]
        off = off_ref[tile * N_EXPERTS + e]
        dst = dst_ref[tile * N_EXPERTS + e]
        for bit in range(RUN_BITS):
            done = (n >> (bit + 1)) << (bit + 1)

            @pl.when(((n >> bit) & 1) == 1)
            def _():
                make_copy(off + done, dst + done, 1 << bit).start()
        return carry

    lax.fori_loop(0, N_EXPERTS, per_expert, 0)


def _dispatch_kernel(tcnt_ref, off_ref, dst_ref, pad_start_ref, pad_len_ref, nused_ref,
                     pos_ref, x1_ref, xs_hbm, sorted_ref, zero_ref, sem, zsem):
    tm = MOE_TM
    rows = TOP_K * tm
    d = x1_ref.shape[1]
    sub = d // LANES
    nblk = xs_hbm.shape[0] // (FFN_TM * sub)
    i = pl.program_id(0)
    last = pl.num_programs(0) - 1
    slot = lax.rem(i, 2)

    def sorted_row(s, h, local_row):
        return pl.multiple_of(((s * DISPATCH_SUB + h) * rows + local_row) * sub, sub)

    def wait_runs(s):
        pltpu.make_async_copy(sorted_ref.at[pl.ds(sorted_row(s, 0, 0), DISPATCH_SUB * rows * sub)],
                              xs_hbm.at[pl.ds(0, DISPATCH_SUB * rows * sub)], sem.at[s]).wait()

    @pl.when(i >= 2)
    def _():
        wait_runs(slot)

    ri = lax.broadcasted_iota(jnp.int32, (rows, tm), 0)
    perms = []
    for h in range(DISPATCH_SUB):
        pos = pos_ref[:, h * tm:(h + 1) * tm]
        hit = (ri == pos[0:1]) | (ri == pos[1:2]) | (ri == pos[2:3]) | (ri == pos[3:4])
        perms.append(jnp.where(hit, 1.0, 0.0).astype(BF16))
    srts = [jnp.dot(perms[h], x1_ref[h * tm:(h + 1) * tm, :].astype(BF16),
                    preferred_element_type=F32) for h in range(DISPATCH_SUB)]
    for h in range(DISPATCH_SUB):
        _rows_to_slabs(sorted_ref, sorted_row(slot, h, 0), srts[h], sub)

    for h in range(DISPATCH_SUB):
        def run_copy(local_row, global_row, nrows, h=h):
            return pltpu.make_async_copy(
                sorted_ref.at[pl.ds(sorted_row(slot, h, local_row), nrows * sub)],
                xs_hbm.at[pl.ds(pl.multiple_of(global_row * sub, sub), nrows * sub)], sem.at[slot])

        _for_each_run(i * DISPATCH_SUB + h, tcnt_ref, off_ref, dst_ref, run_copy)

    @pl.when(i == 0)
    def _():
        zero_ref[...] = jnp.zeros_like(zero_ref)

        def zero_copy(r, nrows):
            return pltpu.make_async_copy(
                zero_ref.at[pl.ds(0, nrows * sub)],
                xs_hbm.at[pl.ds(pl.multiple_of(r * sub, sub), nrows * sub)], zsem)

        def tail_copy(blk):
            return pltpu.make_async_copy(
                zero_ref, xs_hbm.at[pl.ds(pl.multiple_of(blk * (FFN_TM * sub), FFN_TM * sub),
                                          FFN_TM * sub)], zsem)

        def tail_start(blk, c):
            tail_copy(blk).start()
            return c

        def tail_wait(blk, c):
            tail_copy(blk).wait()
            return c

        lax.fori_loop(nused_ref[0], nblk, tail_start, 0)
        lax.fori_loop(nused_ref[0], nblk, tail_wait, 0)

        def per_expert(e, _):
            s = pad_start_ref[e]
            cnt = pad_len_ref[e]
            for wait in (False, True):
                for bit in range(FFN_TM.bit_length() - 1):
                    done = (cnt >> (bit + 1)) << (bit + 1)

                    @pl.when(((cnt >> bit) & 1) == 1)
                    def _():
                        piece = zero_copy(s + done, 1 << bit)
                        piece.wait() if wait else piece.start()
            return 0

        lax.fori_loop(0, N_EXPERTS, per_expert, 0)

    @pl.when(i == last)
    def _():
        wait_runs(slot)

        @pl.when(last >= 1)
        def _():
            wait_runs(1 - slot)


def _dispatch(x1, pos, tcnt, off, dst, pad_start, pad_len, nused, p_rows):
    n, d = x1.shape
    tm = MOE_TM * DISPATCH_SUB
    sub = d // LANES
    return pl.pallas_call(
        _dispatch_kernel,
        out_shape=jax.ShapeDtypeStruct((p_rows * sub, LANES), F32),
        grid_spec=pltpu.PrefetchScalarGridSpec(
            num_scalar_prefetch=6,
            grid=(n // tm,),
            in_specs=[
                pl.BlockSpec((TOP_K, tm), lambda i, *_: (0, i)),
                pl.BlockSpec((tm, d), lambda i, *_: (i, 0)),
            ],
            out_specs=pl.BlockSpec(memory_space=pl.ANY),
            scratch_shapes=[pltpu.VMEM((2 * TOP_K * tm * sub, LANES), F32),
                            pltpu.VMEM((FFN_TM * sub, LANES), F32),
                            pltpu.SemaphoreType.DMA((2,)), pltpu.SemaphoreType.DMA(())],
        ),
        compiler_params=pltpu.CompilerParams(
            dimension_semantics=("arbitrary",), vmem_limit_bytes=VMEM_LIMIT),
        name="dispatch",
    )(tcnt, off, dst, pad_start, pad_len, nused, pos, x1)


def _ffn_kernel(blk_e_ref, nused_ref, next_e_ref, valid_ref, xs_ref, wup_hbm, bup_ref, wdn_hbm,
                bdn_ref, y_ref, wup_f, wdn_f, wup_b, wdn_b, wsem):
    tm = FFN_TM
    dff = wdn_f.shape[0]
    sub = wdn_f.shape[1] // LANES
    i = pl.program_id(0)
    used = i < nused_ref[0]

    @pl.when(jnp.logical_not(used))
    def _():
        y_ref[...] = jnp.zeros_like(y_ref)

    def weight_copies(e):
        return (pltpu.make_async_copy(wup_hbm.at[e], wup_f, wsem.at[0]),
                pltpu.make_async_copy(wdn_hbm.at[e], wdn_f, wsem.at[1]))

    e = blk_e_ref[i]
    new_expert = jnp.logical_or(i == 0, e != blk_e_ref[jnp.maximum(i - 1, 0)])

    @pl.when(jnp.logical_and(used, new_expert))
    def _():
        @pl.when(i == 0)
        def _():
            for cp in weight_copies(e):
                cp.start()

        for cp in weight_copies(e):
            cp.wait()
        for c in range(0, 2 * dff, WEIGHT_CAST_COLS):
            wup_b[:, c:c + WEIGHT_CAST_COLS] = wup_f[:, c:c + WEIGHT_CAST_COLS].astype(BF16)
        for c in range(0, wdn_f.shape[1], WEIGHT_CAST_COLS):
            wdn_b[:, c:c + WEIGHT_CAST_COLS] = wdn_f[:, c:c + WEIGHT_CAST_COLS].astype(BF16)

        @pl.when(next_e_ref[e] != e)
        def _():
            for cp in weight_copies(next_e_ref[e]):
                cp.start()

    def expert_mlp(nrows):
        xb = _slabs_to_rows(xs_ref, 0, nrows, sub).astype(BF16)
        h = jnp.dot(xb, wup_b[...], preferred_element_type=F32) + bup_ref[0]
        gate = jnp.minimum(h[:, :dff], SWIGLU_LIMIT)
        lin = jnp.clip(h[:, dff:], -SWIGLU_LIMIT, SWIGLU_LIMIT)
        act = (lin + 1.0) * gate * jax.nn.sigmoid(SWIGLU_ALPHA * gate)
        y = jnp.dot(act.astype(BF16), wdn_b[...], preferred_element_type=F32) + bdn_ref[0]
        _rows_to_slabs(y_ref, 0, y, sub)

    half = tm // 2
    more_than_half = valid_ref[i] > half

    @pl.when(jnp.logical_and(used, more_than_half))
    def _():
        expert_mlp(tm)

    @pl.when(jnp.logical_and(used, jnp.logical_not(more_than_half)))
    def _():
        expert_mlp(half)
        y_ref[half * sub:, :] = jnp.zeros((half * sub, LANES), F32)


def _ffn(xs, blk_e, nused, next_e, blk_valid, wup, bup, wdn, bdn):
    tm = FFN_TM
    e, d, dff2 = wup.shape
    dff = dff2 // 2
    sub = d // LANES
    rowblk = lambda i, be, nu, ne, bv: (i, 0)
    bsel = lambda i, be, nu, ne, bv: (be[i], 0, 0)
    return pl.pallas_call(
        _ffn_kernel,
        out_shape=jax.ShapeDtypeStruct(xs.shape, F32),
        grid_spec=pltpu.PrefetchScalarGridSpec(
            num_scalar_prefetch=4,
            grid=(xs.shape[0] // (tm * sub),),
            in_specs=[
                pl.BlockSpec((tm * sub, LANES), rowblk),
                pl.BlockSpec(memory_space=pl.ANY),
                pl.BlockSpec((1, 1, dff2), bsel),
                pl.BlockSpec(memory_space=pl.ANY),
                pl.BlockSpec((1, 1, d), bsel),
            ],
            out_specs=pl.BlockSpec((tm * sub, LANES), rowblk),
            scratch_shapes=[pltpu.VMEM((d, dff2), F32), pltpu.VMEM((dff, d), F32),
                            pltpu.VMEM((d, dff2), BF16), pltpu.VMEM((dff, d), BF16),
                            pltpu.SemaphoreType.DMA((2,))],
        ),
        compiler_params=pltpu.CompilerParams(
            dimension_semantics=("arbitrary",), vmem_limit_bytes=VMEM_LIMIT),
        name="ffn",
    )(blk_e, nused, next_e, blk_valid, xs, wup, bup, wdn, bdn)


def _combine_kernel(tcnt_ref, off_ref, dst_ref, pos_ref, gate_ref, x1_ref, g2_ref, b2_ref, y_hbm,
                    o_ref, stage_ref, sem):
    tm = MOE_TM
    rows = TOP_K * tm
    sub = x1_ref.shape[1] // LANES
    i = pl.program_id(0)
    nsteps = pl.num_programs(0)
    slot = lax.rem(i, 2)

    def stage_row(s, h, local_row):
        return pl.multiple_of(((s * COMBINE_SUB + h) * rows + local_row) * sub, sub)

    def fetch_runs(step, s):
        for h in range(COMBINE_SUB):
            def run_copy(local_row, global_row, nrows, h=h):
                return pltpu.make_async_copy(
                    y_hbm.at[pl.ds(pl.multiple_of(global_row * sub, sub), nrows * sub)],
                    stage_ref.at[pl.ds(stage_row(s, h, local_row), nrows * sub)], sem.at[s])
            _for_each_run(step * COMBINE_SUB + h, tcnt_ref, off_ref, dst_ref, run_copy)

    @pl.when(i == 0)
    def _():
        fetch_runs(0, 0)

    @pl.when(i + 1 < nsteps)
    def _():
        fetch_runs(i + 1, 1 - slot)

    pltpu.make_async_copy(y_hbm.at[pl.ds(0, COMBINE_SUB * rows * sub)],
                          stage_ref.at[pl.ds(stage_row(slot, 0, 0), COMBINE_SUB * rows * sub)],
                          sem.at[slot]).wait()

    tiles = range(COMBINE_SUB)
    ci = lax.broadcasted_iota(jnp.int32, (tm, rows), 1)
    ws = []
    for h in tiles:
        pos = pos_ref[h * tm:(h + 1) * tm, :]
        gates = gate_ref[h * tm:(h + 1) * tm, :]
        w = jnp.where(ci == pos[:, 0:1], gates[:, 0:1], 0.0)
        for k in range(1, TOP_K):
            w = w + jnp.where(ci == pos[:, k:k + 1], gates[:, k:k + 1], 0.0)
        ws.append(w)
    wh = [w.astype(BF16) for w in ws]
    wl = [(w - hi.astype(F32)).astype(BF16) for w, hi in zip(ws, wh)]
    ys = [_slabs_to_rows(stage_ref, stage_row(slot, h, 0), rows, sub) for h in tiles]
    yh = [y.astype(BF16) for y in ys]
    yl = [(y - hi.astype(F32)).astype(BF16) for y, hi in zip(ys, yh)]
    for h in tiles:
        ffn = (jnp.dot(wh[h], yh[h], preferred_element_type=F32)
               + jnp.dot(wl[h], yh[h], preferred_element_type=F32)
               + jnp.dot(wh[h], yl[h], preferred_element_type=F32))
        rs = slice(h * tm, (h + 1) * tm)
        o_ref[rs, :] = _layer_norm(DEEPNORM_ALPHA * x1_ref[rs, :] + ffn, g2_ref[...], b2_ref[...])


def _combine(y, pos_tok, gates_tok, tcnt, off, dst, x1, g2, b2):
    n, d = x1.shape
    tm = MOE_TM * COMBINE_SUB
    sub = d // LANES
    row = lambda i, *_: (i, 0)
    const = lambda i, *_: (0, 0)
    return pl.pallas_call(
        _combine_kernel,
        out_shape=jax.ShapeDtypeStruct((n, d), F32),
        grid_spec=pltpu.PrefetchScalarGridSpec(
            num_scalar_prefetch=3,
            grid=(n // tm,),
            in_specs=[
                pl.BlockSpec((tm, TOP_K), row),
                pl.BlockSpec((tm, TOP_K), row),
                pl.BlockSpec((tm, d), row),
                pl.BlockSpec(g2.shape, const),
                pl.BlockSpec(b2.shape, const),
                pl.BlockSpec(memory_space=pl.ANY),
            ],
            out_specs=pl.BlockSpec((tm, d), row),
            scratch_shapes=[pltpu.VMEM((2 * TOP_K * tm * sub, LANES), F32),
                            pltpu.SemaphoreType.DMA((2,))],
        ),
        compiler_params=pltpu.CompilerParams(
            dimension_semantics=("arbitrary",), vmem_limit_bytes=VMEM_LIMIT),
        name="combine",
    )(tcnt, off, dst, pos_tok, gates_tok, x1, g2, b2, y)


def _head_pairs(w_a, w_b):
    d = w_a.shape[0]
    a = w_a.reshape(d, DIFF_HEADS, DIFF_HEAD_DIM)
    b = w_b.reshape(d, DIFF_HEADS, DIFF_HEAD_DIM)
    return jnp.concatenate([a, b], axis=-1).reshape(d, DIFF_HEADS * V_HEAD_DIM)


def _lookup(table, eid):
    eids = jnp.arange(N_EXPERTS, dtype=jnp.int32)[:, None, None]
    return jnp.sum(jnp.where(eid[None] == eids, table[:, None, :], 0), axis=0)


def kernel(x, w_in, lambda_q1, lambda_k1, lambda_q2, lambda_k2, subln_g, gmlp_ln_g, gmlp_ln_b,
           w_spatial, b_spatial, w_o, ln1_g, ln1_b, w_router, b_router, w_up, b_up,
           w_down, b_down, ln2_g, ln2_b):
    b, s, d = x.shape
    n = b * s
    x2 = x.reshape(n, d)

    w = w_in[0]
    c = QK_WIDTH
    scale = DIFF_HEAD_DIM ** -0.5 * LOG2_E
    wq = _head_pairs(w[:, 0:c], w[:, c:2 * c]) * scale
    wk = _head_pairs(w[:, 2 * c:3 * c], w[:, 3 * c:4 * c])
    wqk = jnp.concatenate([wq, wk], axis=1).astype(BF16)
    wvt = w[:, 4 * c:4 * c + DIFF_WIDTH].T.astype(BF16)
    wg = w[:, 4 * c + DIFF_WIDTH:].astype(BF16)
    lamv = jnp.concatenate([lambda_q1, lambda_k1, lambda_q2, lambda_k2], axis=0)
    bsp = jnp.repeat(b_spatial[0].T, GMLP_GROUP_DIM, axis=1)

    assert PROJ_TM == ATTN_T and s % ATTN_T == 0 and n % (MOE_TM * MIX_SUB) == 0
    qq, kk, vt, u, vn = _proj(x2, wqk, wvt, wg, gmlp_ln_g, gmlp_ln_b)
    attn = _attention(qq.reshape(b, s, -1), kk.reshape(b, s, -1),
                      vt.reshape(b, s // ATTN_T, DIFF_WIDTH, ATTN_T),
                      lamv, subln_g.reshape(V_HEAD_DIM, 1)).reshape(n, DIFF_WIDTH)
    x1, eid, gates, rank, cnt, tcnt_l = _mix(attn, u, vn, x2, w_spatial[0], bsp,
                                             w_o[0].astype(BF16), ln1_g, ln1_b,
                                             w_router[0].T, b_router[0][:, None])

    ntiles = n // MOE_TM
    counts = cnt[:, 0].astype(jnp.int32)
    padded = ((counts + FFN_TM - 1) // FFN_TM) * FFN_TM
    end_padded = jnp.cumsum(padded)
    start_padded = end_padded - padded
    tcnt = tcnt_l.reshape(ntiles, N_EXPERTS, LANES)[:, :, 0].astype(jnp.int32)
    before = jnp.cumsum(tcnt, axis=0) - tcnt
    off = jnp.cumsum(tcnt, axis=1) - tcnt
    dst = start_padded[None, :] + before
    tile_base = jnp.repeat((off - before).T, MOE_TM, axis=1)
    pos = _lookup(tile_base, eid) + rank

    p_rows = n * TOP_K + N_EXPERTS * FFN_TM
    nblk = p_rows // FFN_TM
    nused = (end_padded[-1:] // FFN_TM).astype(jnp.int32)
    blk_start = jnp.arange(nblk, dtype=jnp.int32) * FFN_TM
    blk_e = jnp.minimum(jnp.sum((end_padded[None, :] <= blk_start[:, None]).astype(jnp.int32), axis=1),
                        N_EXPERTS - 1)
    tcnt_f, off_f, dst_f = tcnt.reshape(-1), off.reshape(-1), dst.reshape(-1)

    xs = _dispatch(x1, pos, tcnt_f, off_f, dst_f, start_padded + counts, padded - counts, nused,
                   p_rows)
    e_ids = jnp.arange(N_EXPERTS, dtype=jnp.int32)
    later_used = jnp.logical_and(e_ids[None, :] > e_ids[:, None], padded[None, :] > 0)
    next_e = jnp.min(jnp.where(later_used, e_ids[None, :], N_EXPERTS), axis=1)
    next_e = jnp.where(next_e == N_EXPERTS, e_ids, next_e)
    is_e = blk_e[:, None] == e_ids[None, :]
    filled = jnp.sum(jnp.where(is_e, (start_padded + counts)[None, :], 0), axis=1) - blk_start
    blk_valid = jnp.clip(filled, 0, FFN_TM).astype(jnp.int32)
    y = _ffn(xs, blk_e, nused, next_e, blk_valid, w_up[0], b_up[0][:, None, :], w_down[0],
             b_down[0][:, None, :])
    out = _combine(y, pos.T, gates.T, tcnt_f, off_f, dst_f, x1, ln2_g, ln2_b)
    return out.reshape(b, s, d)
```

```python
import jax
import jax.numpy as jnp
from jax import lax
from jax.experimental import pallas as pl
from jax.experimental.pallas import tpu as pltpu

DIFF_HEADS = 4
DIFF_HEAD_DIM = 64
V_HEAD_DIM = 2 * DIFF_HEAD_DIM
QK_WIDTH = DIFF_HEADS * DIFF_HEAD_DIM
DIFF_WIDTH = DIFF_HEADS * V_HEAD_DIM
GMLP_GROUPS = 8
GMLP_GROUP_DIM = 64
GMLP_WIDTH = GMLP_GROUPS * GMLP_GROUP_DIM
CHUNK = 128
N_EXPERTS = 32
TOP_K = 4
SWIGLU_LIMIT = 7.0
SWIGLU_ALPHA = 1.702
LN_EPS = 1e-5
DEPTH = 1
DEEPNORM_ALPHA = (2.0 * DEPTH) ** 0.25
LAMBDA_INIT = 0.8 - 0.6 * 1.0
LOG2_E = 1.4426950408889634

LANES = 128
SUBLANES = 8

PROJ_TM = 512
PROJ_CHUNK = 256
ATTN_T = 512
ATTN_GROUP = 4
MOE_TM = 256
MIX_SUB = 4
COMBINE_SUB = 2
DISPATCH_SUB = 2
FFN_TM = 512
WEIGHT_CAST_COLS = 256
RUN_BITS = MOE_TM.bit_length()
VMEM_LIMIT = 48 * 1024 * 1024

BF16 = jnp.bfloat16
F32 = jnp.float32


def _layer_norm(y, g, b):
    mu = jnp.mean(y, axis=-1, keepdims=True)
    yc = y - mu
    var = jnp.mean(yc * yc, axis=-1, keepdims=True)
    return yc * lax.rsqrt(var + LN_EPS) * g + b


def _gelu(x):
    return 0.5 * x * (1.0 + lax.erf(x * (2.0 ** -0.5)))


def _nt_dot(a, b):
    return lax.dot_general(a, b, (((1,), (1,)), ((), ())), preferred_element_type=F32)


def _slabs_to_rows(ref, first, nrows, sub):
    return jnp.concatenate([ref[pl.ds(first + j, nrows, stride=sub), :] for j in range(sub)], axis=1)


def _rows_to_slabs(ref, first, rows, sub):
    for j in range(sub):
        ref[pl.ds(first + j, rows.shape[0], stride=sub), :] = rows[:, j * LANES:(j + 1) * LANES]


def _proj_kernel(x_ref, wqk_ref, wvt_ref, wg_ref, lng_ref, lnb_ref,
                 qq_ref, kk_ref, vt_ref, u_ref, vn_ref):
    xb = x_ref[...].astype(BF16)

    def gelu_cols(lo):
        return _gelu(jnp.dot(xb, wg_ref[:, lo:lo + PROJ_CHUNK], preferred_element_type=F32))

    gv = [gelu_cols(GMLP_WIDTH + c) for c in range(0, GMLP_WIDTH, PROJ_CHUNK)]
    qk = jnp.dot(xb, wqk_ref[...], preferred_element_type=F32)
    qq_ref[...] = qk[:, :DIFF_WIDTH].astype(BF16)
    kk_ref[...] = qk[:, DIFF_WIDTH:].astype(BF16)
    for c in range(0, GMLP_WIDTH, PROJ_CHUNK):
        u_ref[:, c:c + PROJ_CHUNK] = gelu_cols(c).astype(BF16)
    vt_ref[0] = _nt_dot(wvt_ref[...], xb).astype(BF16)
    vn_ref[...] = _layer_norm(jnp.concatenate(gv, axis=1), lng_ref[...], lnb_ref[...]).astype(BF16)


def _proj(x2, wqk, wvt, wg, lng, lnb):
    n, d = x2.shape
    tm = PROJ_TM
    row = lambda i: (i, 0)
    const = lambda i: (0, 0)
    out = jax.ShapeDtypeStruct((n, DIFF_WIDTH), BF16)
    vt_out = jax.ShapeDtypeStruct((n // tm, DIFF_WIDTH, tm), BF16)
    rowspec = pl.BlockSpec((tm, DIFF_WIDTH), row)
    return pl.pallas_call(
        _proj_kernel,
        out_shape=(out, out, vt_out, out, out),
        grid=(n // tm,),
        in_specs=[
            pl.BlockSpec((tm, d), row),
            pl.BlockSpec(wqk.shape, const),
            pl.BlockSpec(wvt.shape, const),
            pl.BlockSpec(wg.shape, const),
            pl.BlockSpec(lng.shape, const),
            pl.BlockSpec(lnb.shape, const),
        ],
        out_specs=[rowspec, rowspec, pl.BlockSpec((1, DIFF_WIDTH, tm), lambda i: (i, 0, 0)),
                   rowspec, rowspec],
        compiler_params=pltpu.CompilerParams(
            dimension_semantics=("arbitrary",), vmem_limit_bytes=VMEM_LIMIT),
        name="proj",
    )(x2, wqk, wvt, wg, lng, lnb)


def _attn_kernel(lamv_ref, g_ref, qq_ref, kk_ref, vt_ref, o_ref, s_ref):
    t = ATTN_T
    nq = qq_ref.shape[1] // t
    starts = (0, 1, 1, 0)

    def group(g, carry):
        for r in range(ATTN_GROUP):
            _attn_q_tile(ATTN_GROUP * g + r, nq, r % 2, starts[r],
                         lamv_ref, g_ref, qq_ref, kk_ref, vt_ref, o_ref, s_ref)
        return carry

    q1, q2 = _masked_queries(qq_ref, 0)
    k0 = kk_ref[0, 0:t, :]
    s_ref[0, 0] = _nt_dot(k0, q1)
    s_ref[0, 1] = _nt_dot(k0, q2)
    lax.fori_loop(0, nq // ATTN_GROUP, group, 0)


def _masked_queries(qq_ref, qi):
    t = ATTN_T
    q = qq_ref[0, pl.ds(pl.multiple_of(qi * t, t), t), :]
    first = lax.broadcasted_iota(jnp.int32, q.shape, 1) < DIFF_HEAD_DIM
    zero = jnp.zeros_like(q)
    return jnp.where(first, q, zero), jnp.where(first, zero, q)


def _attn_q_tile(qi, nq, odd, start, lamv_ref, g_ref, qq_ref, kk_ref, vt_ref, o_ref, s_ref):
    t = ATTN_T
    q_rows = pl.ds(pl.multiple_of(qi * t, t), t)
    q1, q2 = _masked_queries(qq_ref, qi)
    next_q = _masked_queries(qq_ref, jnp.minimum(qi + 1, nq - 1))

    def keys(j):
        return kk_ref[0, pl.ds(pl.multiple_of(j * t, t), t), :]

    def consume(j, buf, carry, masked, nxt):
        vt = vt_ref[0, j]
        k_next = keys(nxt[0])
        new = []
        for mp, ((m, l, acc), qh) in enumerate(zip(carry, nxt[2])):
            s_ref[nxt[1], mp] = _nt_dot(k_next, qh)
            s = s_ref[buf, mp]
            if masked:
                key = lax.broadcasted_iota(jnp.int32, s.shape, 0)
                qry = lax.broadcasted_iota(jnp.int32, s.shape, 1)
                s = jnp.where(key <= qry, s, -jnp.inf)
            m_new = jnp.maximum(m, jnp.max(s, axis=0, keepdims=True))
            alpha = jnp.exp2(m - m_new)
            p = jnp.exp2(s - m_new)
            l_new = alpha * l + jnp.sum(p, axis=0, keepdims=True)
            acc_new = alpha * acc + jnp.dot(vt, p.astype(BF16), preferred_element_type=F32)
            new.append((m_new, l_new, acc_new))
        return tuple(new)

    def init():
        return (jnp.full((1, t), -jnp.inf, F32), jnp.zeros((1, t), F32),
                jnp.zeros((V_HEAD_DIM, t), F32))

    def finish(carry):
        (_, l1, acc1), (_, l2, acc2) = carry
        lv = lamv_ref[...]
        lam = (jnp.exp(jnp.sum(lv[0:1] * lv[1:2], axis=-1, keepdims=True))
               - jnp.exp(jnp.sum(lv[2:3] * lv[3:4], axis=-1, keepdims=True)) + LAMBDA_INIT)
        o = acc1 / l1 - lam * (acc2 / l2)
        ms = jnp.mean(o * o, axis=0, keepdims=True)
        o = o * lax.rsqrt(ms + LN_EPS) * g_ref[...] * (1.0 - LAMBDA_INIT)
        o_ref[0, q_rows, :] = o.T.astype(BF16)

    own = (q1, q2)
    a, b = start, 1 - start

    def pair(jj, c):
        j = 2 * jj
        c = consume(j, a, c, False, (j + 1, b, own))
        return consume(j + 1, b, c, False, (j + 2, a, own))

    carry = lax.fori_loop(0, qi // 2, pair, (init(), init()))
    if odd:
        carry = consume(qi - 1, a, carry, False, (qi, b, own))
        finish(consume(qi, b, carry, True, (0, a, next_q)))
    else:
        finish(consume(qi, a, carry, True, (0, b, next_q)))


def _attention(qq, kk, vt, lamv, subln_g_col):
    b, s, _ = qq.shape
    t = ATTN_T
    seqspec = pl.BlockSpec((1, s, V_HEAD_DIM), lambda bi, h: (bi, 0, h))
    vtspec = pl.BlockSpec((1, s // t, V_HEAD_DIM, t), lambda bi, h: (bi, 0, h, 0))
    const = lambda bi, h: (0, 0)
    return pl.pallas_call(
        _attn_kernel,
        out_shape=jax.ShapeDtypeStruct((b, s, DIFF_WIDTH), BF16),
        grid=(b, DIFF_HEADS),
        in_specs=[pl.BlockSpec(lamv.shape, const), pl.BlockSpec(subln_g_col.shape, const),
                  seqspec, seqspec, vtspec],
        out_specs=seqspec,
        scratch_shapes=[pltpu.VMEM((2, 2, t, t), F32)],
        compiler_params=pltpu.CompilerParams(
            dimension_semantics=("arbitrary",) * 2, vmem_limit_bytes=VMEM_LIMIT),
        name="attn",
    )(lamv, subln_g_col, qq, kk, vt)


def _mix_kernel(attn_ref, u_ref, vn_ref, x_ref, wsp_ref, bsp_ref, wo_ref, g1_ref, b1_ref,
                wrt_ref, br_ref,
                x1_ref, eid_ref, gate_ref, rank_ref, cnt_ref, tcnt_ref,
                cat_ref, carry_ref):
    tm = MOE_TM

    @pl.when(pl.program_id(0) == 0)
    def _():
        carry_ref[...] = jnp.zeros_like(carry_ref)

    ri = lax.broadcasted_iota(jnp.int32, (CHUNK, CHUNK), 0)
    ci = lax.broadcasted_iota(jnp.int32, (CHUNK, CHUNK), 1)
    tril = ci <= ri
    first = ci < GMLP_GROUP_DIM
    w = [jnp.where(tril, wsp_ref[g], 0.0).astype(BF16) for g in range(GMLP_GROUPS)]
    wr = wrt_ref[...]
    wh = wr.astype(BF16)
    wl = (wr - wh.astype(F32)).astype(BF16)
    eio = lax.broadcasted_iota(jnp.int32, (N_EXPERTS, tm), 0).astype(F32)
    ti = lax.broadcasted_iota(jnp.int32, (tm, tm), 0)
    tj = lax.broadcasted_iota(jnp.int32, (tm, tm), 1)
    before = jnp.where(ti < tj, 1.0, 0.0).astype(BF16)

    halves = [slice(h * tm, (h + 1) * tm) for h in range(MIX_SUB)]

    for rs in halves:
        cat_ref[rs, :DIFF_WIDTH] = attn_ref[rs, :]
    for c in range(MIX_SUB * tm // CHUNK):
        rows = slice(c * CHUNK, (c + 1) * CHUNK)
        for jb in range(GMLP_WIDTH // LANES):
            cols = slice(jb * LANES, (jb + 1) * LANES)
            vb = vn_ref[rows, cols]
            zero = jnp.zeros_like(vb)
            z = (jnp.dot(w[2 * jb], jnp.where(first, vb, zero), preferred_element_type=F32)
                 + jnp.dot(w[2 * jb + 1], jnp.where(first, zero, vb), preferred_element_type=F32))
            gated = u_ref[rows, cols].astype(F32) * (z + bsp_ref[:, cols])
            cat_ref[rows, DIFF_WIDTH + jb * LANES:DIFF_WIDTH + (jb + 1) * LANES] = gated.astype(BF16)

    x1s = []
    for rs in halves:
        mixed = jnp.dot(cat_ref[rs, :], wo_ref[...], preferred_element_type=F32)
        x1 = _layer_norm(DEEPNORM_ALPHA * x_ref[rs, :] + mixed, g1_ref[...], b1_ref[...])
        x1_ref[rs, :] = x1
        x1s.append(x1)

    curs = []
    for x1 in x1s:
        xh = x1.astype(BF16)
        xl = (x1 - xh.astype(F32)).astype(BF16)
        curs.append(_nt_dot(wh, xh) + _nt_dot(wl, xh) + _nt_dot(wh, xl) + br_ref[...])

    vals = [[] for _ in halves]
    idxs = [[] for _ in halves]
    sels = [[] for _ in halves]
    for _ in range(TOP_K):
        for h in range(MIX_SUB):
            mx = jnp.max(curs[h], axis=0, keepdims=True)
            idx = jnp.min(jnp.where(curs[h] == mx, eio, float(N_EXPERTS)), axis=0, keepdims=True)
            sel = eio == idx
            vals[h].append(mx)
            idxs[h].append(idx)
            sels[h].append(sel)
            curs[h] = jnp.where(sel, -jnp.inf, curs[h])

    carry = carry_ref[:, 0:1]
    for h, rs in enumerate(halves):
        ex = [jnp.exp(vk - vals[h][0]) for vk in vals[h]]
        denom = ex[0] + ex[1] + ex[2] + ex[3]
        gate_ref[:, rs] = jnp.concatenate([e / denom for e in ex], axis=0)
        eid_ref[:, rs] = jnp.concatenate(idxs[h], axis=0).astype(jnp.int32)

        chosen = (sels[h][0] | sels[h][1] | sels[h][2] | sels[h][3])
        onehot = jnp.where(chosen, 1.0, 0.0)
        cnt_before = jnp.dot(onehot.astype(BF16), before, preferred_element_type=F32) + carry
        ranks = [jnp.sum(jnp.where(s, cnt_before, 0.0), axis=0, keepdims=True) for s in sels[h]]
        rank_ref[:, rs] = jnp.concatenate(ranks, axis=0).astype(jnp.int32)
        tile_cnt = jnp.sum(onehot, axis=1, keepdims=True)
        tcnt_ref[h * N_EXPERTS:(h + 1) * N_EXPERTS, :] = jnp.broadcast_to(tile_cnt, (N_EXPERTS, LANES))
        carry = carry + tile_cnt

    carry_ref[...] = jnp.broadcast_to(carry, carry_ref.shape)
    cnt_ref[...] = jnp.broadcast_to(carry, cnt_ref.shape)


def _mix(attn, u, vn, x2, wsp, bsp, wo, g1, b1, wrt, br):
    n, d = x2.shape
    tm = MOE_TM * MIX_SUB
    row = lambda i: (i, 0)
    col = lambda i: (0, i)
    const2 = lambda i: (0, 0)
    const3 = lambda i: (0, 0, 0)
    tok = lambda dt: jax.ShapeDtypeStruct((TOP_K, n), dt)
    return pl.pallas_call(
        _mix_kernel,
        out_shape=(jax.ShapeDtypeStruct((n, d), F32), tok(jnp.int32), tok(F32), tok(jnp.int32),
                   jax.ShapeDtypeStruct((N_EXPERTS, LANES), F32),
                   jax.ShapeDtypeStruct((n // MOE_TM * N_EXPERTS, LANES), F32)),
        grid=(n // tm,),
        in_specs=[
            pl.BlockSpec((tm, DIFF_WIDTH), row),
            pl.BlockSpec((tm, GMLP_WIDTH), row),
            pl.BlockSpec((tm, GMLP_WIDTH), row),
            pl.BlockSpec((tm, d), row),
            pl.BlockSpec(wsp.shape, const3),
            pl.BlockSpec(bsp.shape, const2),
            pl.BlockSpec(wo.shape, const2),
            pl.BlockSpec(g1.shape, const2),
            pl.BlockSpec(b1.shape, const2),
            pl.BlockSpec(wrt.shape, const2),
            pl.BlockSpec(br.shape, const2),
        ],
        out_specs=[
            pl.BlockSpec((tm, d), row),
            pl.BlockSpec((TOP_K, tm), col),
            pl.BlockSpec((TOP_K, tm), col),
            pl.BlockSpec((TOP_K, tm), col),
            pl.BlockSpec((N_EXPERTS, LANES), const2),
            pl.BlockSpec((MIX_SUB * N_EXPERTS, LANES), row),
        ],
        scratch_shapes=[pltpu.VMEM((tm, DIFF_WIDTH + GMLP_WIDTH), BF16),
                        pltpu.VMEM((N_EXPERTS, LANES), F32)],
        compiler_params=pltpu.CompilerParams(
            dimension_semantics=("arbitrary",), vmem_limit_bytes=VMEM_LIMIT),
        name="mix",
    )(attn, u, vn, x2, wsp, bsp, wo, g1, b1, wrt, br)


def _for_each_run(tile, tcnt_ref, off_ref, dst_ref, make_copy):
    def per_expert(e, carry):
        n = tcnt_ref[tile * N_EXPERTS + e]
        off = off_ref[tile * N_EXPERTS + e]
        dst = dst_ref[tile * N_EXPERTS + e]
        for bit in range(RUN_BITS):
            done = (n >> (bit + 1)) << (bit + 1)

            @pl.when(((n >> bit) & 1) == 1)
            def _():
                make_copy(off + done, dst + done, 1 << bit).start()
        return carry

    lax.fori_loop(0, N_EXPERTS, per_expert, 0)


def _dispatch_kernel(tcnt_ref, off_ref, dst_ref, pad_start_ref, pad_len_ref, nused_ref,
                     pos_ref, x1_ref, xs_hbm, sorted_ref, zero_ref, sem, zsem):
    tm = MOE_TM
    rows = TOP_K * tm
    d = x1_ref.shape[1]
    sub = d // LANES
    nblk = xs_hbm.shape[0] // (FFN_TM * sub)
    i = pl.program_id(0)
    last = pl.num_programs(0) - 1
    slot = lax.rem(i, 2)

    def sorted_row(s, h, local_row):
        return pl.multiple_of(((s * DISPATCH_SUB + h) * rows + local_row) * sub, sub)

    def wait_runs(s):
        pltpu.make_async_copy(sorted_ref.at[pl.ds(sorted_row(s, 0, 0), DISPATCH_SUB * rows * sub)],
                              xs_hbm.at[pl.ds(0, DISPATCH_SUB * rows * sub)], sem.at[s]).wait()

    @pl.when(i >= 2)
    def _():
        wait_runs(slot)

    ri = lax.broadcasted_iota(jnp.int32, (rows, tm), 0)
    perms = []
    for h in range(DISPATCH_SUB):
        pos = pos_ref[:, h * tm:(h + 1) * tm]
        hit = (ri == pos[0:1]) | (ri == pos[1:2]) | (ri == pos[2:3]) | (ri == pos[3:4])
        perms.append(jnp.where(hit, 1.0, 0.0).astype(BF16))
    srts = [jnp.dot(perms[h], x1_ref[h * tm:(h + 1) * tm, :].astype(BF16),
                    preferred_element_type=F32) for h in range(DISPATCH_SUB)]
    for h in range(DISPATCH_SUB):
        _rows_to_slabs(sorted_ref, sorted_row(slot, h, 0), srts[h], sub)

    for h in range(DISPATCH_SUB):
        def run_copy(local_row, global_row, nrows, h=h):
            return pltpu.make_async_copy(
                sorted_ref.at[pl.ds(sorted_row(slot, h, local_row), nrows * sub)],
                xs_hbm.at[pl.ds(pl.multiple_of(global_row * sub, sub), nrows * sub)], sem.at[slot])

        _for_each_run(i * DISPATCH_SUB + h, tcnt_ref, off_ref, dst_ref, run_copy)

    @pl.when(i == 0)
    def _():
        zero_ref[...] = jnp.zeros_like(zero_ref)

        def zero_copy(r, nrows):
            return pltpu.make_async_copy(
                zero_ref.at[pl.ds(0, nrows * sub)],
                xs_hbm.at[pl.ds(pl.multiple_of(r * sub, sub), nrows * sub)], zsem)

        def tail_copy(blk):
            return pltpu.make_async_copy(
                zero_ref, xs_hbm.at[pl.ds(pl.multiple_of(blk * (FFN_TM * sub), FFN_TM * sub),
                                          FFN_TM * sub)], zsem)

        def tail_start(blk, c):
            tail_copy(blk).start()
            return c

        def tail_wait(blk, c):
            tail_copy(blk).wait()
            return c

        lax.fori_loop(nused_ref[0], nblk, tail_start, 0)
        lax.fori_loop(nused_ref[0], nblk, tail_wait, 0)

        def per_expert(e, _):
            s = pad_start_ref[e]
            cnt = pad_len_ref[e]
            for wait in (False, True):
                for bit in range(FFN_TM.bit_length() - 1):
                    done = (cnt >> (bit + 1)) << (bit + 1)

                    @pl.when(((cnt >> bit) & 1) == 1)
                    def _():
                        piece = zero_copy(s + done, 1 << bit)
                        piece.wait() if wait else piece.start()
            return 0

        lax.fori_loop(0, N_EXPERTS, per_expert, 0)

    @pl.when(i == last)
    def _():
        wait_runs(slot)

        @pl.when(last >= 1)
        def _():
            wait_runs(1 - slot)


def _dispatch(x1, pos, tcnt, off, dst, pad_start, pad_len, nused, p_rows):
    n, d = x1.shape
    tm = MOE_TM * DISPATCH_SUB
    sub = d // LANES
    return pl.pallas_call(
        _dispatch_kernel,
        out_shape=jax.ShapeDtypeStruct((p_rows * sub, LANES), F32),
        grid_spec=pltpu.PrefetchScalarGridSpec(
            num_scalar_prefetch=6,
            grid=(n // tm,),
            in_specs=[
                pl.BlockSpec((TOP_K, tm), lambda i, *_: (0, i)),
                pl.BlockSpec((tm, d), lambda i, *_: (i, 0)),
            ],
            out_specs=pl.BlockSpec(memory_space=pl.ANY),
            scratch_shapes=[pltpu.VMEM((2 * TOP_K * tm * sub, LANES), F32),
                            pltpu.VMEM((FFN_TM * sub, LANES), F32),
                            pltpu.SemaphoreType.DMA((2,)), pltpu.SemaphoreType.DMA(())],
        ),
        compiler_params=pltpu.CompilerParams(
            dimension_semantics=("arbitrary",), vmem_limit_bytes=VMEM_LIMIT),
        name="dispatch",
    )(tcnt, off, dst, pad_start, pad_len, nused, pos, x1)


def _ffn_kernel(blk_e_ref, nused_ref, next_e_ref, valid_ref, xs_ref, wup_hbm, bup_ref, wdn_hbm,
                bdn_ref, y_ref, wup_f, wdn_f, wup_b, wdn_b, wsem):
    tm = FFN_TM
    dff = wdn_f.shape[0]
    sub = wdn_f.shape[1] // LANES
    i = pl.program_id(0)
    used = i < nused_ref[0]

    @pl.when(jnp.logical_not(used))
    def _():
        y_ref[...] = jnp.zeros_like(y_ref)

    def weight_copies(e):
        return (pltpu.make_async_copy(wup_hbm.at[e], wup_f, wsem.at[0]),
                pltpu.make_async_copy(wdn_hbm.at[e], wdn_f, wsem.at[1]))

    e = blk_e_ref[i]
    new_expert = jnp.logical_or(i == 0, e != blk_e_ref[jnp.maximum(i - 1, 0)])

    @pl.when(jnp.logical_and(used, new_expert))
    def _():
        @pl.when(i == 0)
        def _():
            for cp in weight_copies(e):
                cp.start()

        for cp in weight_copies(e):
            cp.wait()
        for c in range(0, 2 * dff, WEIGHT_CAST_COLS):
            wup_b[:, c:c + WEIGHT_CAST_COLS] = wup_f[:, c:c + WEIGHT_CAST_COLS].astype(BF16)
        for c in range(0, wdn_f.shape[1], WEIGHT_CAST_COLS):
            wdn_b[:, c:c + WEIGHT_CAST_COLS] = wdn_f[:, c:c + WEIGHT_CAST_COLS].astype(BF16)

        @pl.when(next_e_ref[e] != e)
        def _():
            for cp in weight_copies(next_e_ref[e]):
                cp.start()

    def expert_mlp(nrows):
        xb = _slabs_to_rows(xs_ref, 0, nrows, sub).astype(BF16)
        h = jnp.dot(xb, wup_b[...], preferred_element_type=F32) + bup_ref[0]
        gate = jnp.minimum(h[:, :dff], SWIGLU_LIMIT)
        lin = jnp.clip(h[:, dff:], -SWIGLU_LIMIT, SWIGLU_LIMIT)
        act = (lin + 1.0) * gate * jax.nn.sigmoid(SWIGLU_ALPHA * gate)
        y = jnp.dot(act.astype(BF16), wdn_b[...], preferred_element_type=F32) + bdn_ref[0]
        _rows_to_slabs(y_ref, 0, y, sub)

    half = tm // 2
    more_than_half = valid_ref[i] > half

    @pl.when(jnp.logical_and(used, more_than_half))
    def _():
        expert_mlp(tm)

    @pl.when(jnp.logical_and(used, jnp.logical_not(more_than_half)))
    def _():
        expert_mlp(half)
        y_ref[half * sub:, :] = jnp.zeros((half * sub, LANES), F32)


def _ffn(xs, blk_e, nused, next_e, blk_valid, wup, bup, wdn, bdn):
    tm = FFN_TM
    e, d, dff2 = wup.shape
    dff = dff2 // 2
    sub = d // LANES
    rowblk = lambda i, be, nu, ne, bv: (i, 0)
    bsel = lambda i, be, nu, ne, bv: (be[i], 0, 0)
    return pl.pallas_call(
        _ffn_kernel,
        out_shape=jax.ShapeDtypeStruct(xs.shape, F32),
        grid_spec=pltpu.PrefetchScalarGridSpec(
            num_scalar_prefetch=4,
            grid=(xs.shape[0] // (tm * sub),),
            in_specs=[
                pl.BlockSpec((tm * sub, LANES), rowblk),
                pl.BlockSpec(memory_space=pl.ANY),
                pl.BlockSpec((1, 1, dff2), bsel),
                pl.BlockSpec(memory_space=pl.ANY),
                pl.BlockSpec((1, 1, d), bsel),
            ],
            out_specs=pl.BlockSpec((tm * sub, LANES), rowblk),
            scratch_shapes=[pltpu.VMEM((d, dff2), F32), pltpu.VMEM((dff, d), F32),
                            pltpu.VMEM((d, dff2), BF16), pltpu.VMEM((dff, d), BF16),
                            pltpu.SemaphoreType.DMA((2,))],
        ),
        compiler_params=pltpu.CompilerParams(
            dimension_semantics=("arbitrary",), vmem_limit_bytes=VMEM_LIMIT),
        name="ffn",
    )(blk_e, nused, next_e, blk_valid, xs, wup, bup, wdn, bdn)


def _combine_kernel(tcnt_ref, off_ref, dst_ref, pos_ref, gate_ref, x1_ref, g2_ref, b2_ref, y_hbm,
                    o_ref, stage_ref, sem):
    tm = MOE_TM
    rows = TOP_K * tm
    sub = x1_ref.shape[1] // LANES
    i = pl.program_id(0)
    nsteps = pl.num_programs(0)
    slot = lax.rem(i, 2)

    def stage_row(s, h, local_row):
        return pl.multiple_of(((s * COMBINE_SUB + h) * rows + local_row) * sub, sub)

    def fetch_runs(step, s):
        for h in range(COMBINE_SUB):
            def run_copy(local_row, global_row, nrows, h=h):
                return pltpu.make_async_copy(
                    y_hbm.at[pl.ds(pl.multiple_of(global_row * sub, sub), nrows * sub)],
                    stage_ref.at[pl.ds(stage_row(s, h, local_row), nrows * sub)], sem.at[s])
            _for_each_run(step * COMBINE_SUB + h, tcnt_ref, off_ref, dst_ref, run_copy)

    @pl.when(i == 0)
    def _():
        fetch_runs(0, 0)

    @pl.when(i + 1 < nsteps)
    def _():
        fetch_runs(i + 1, 1 - slot)

    pltpu.make_async_copy(y_hbm.at[pl.ds(0, COMBINE_SUB * rows * sub)],
                          stage_ref.at[pl.ds(stage_row(slot, 0, 0), COMBINE_SUB * rows * sub)],
                          sem.at[slot]).wait()

    tiles = range(COMBINE_SUB)
    ci = lax.broadcasted_iota(jnp.int32, (tm, rows), 1)
    ws = []
    for h in tiles:
        pos = pos_ref[h * tm:(h + 1) * tm, :]
        gates = gate_ref[h * tm:(h + 1) * tm, :]
        w = jnp.where(ci == pos[:, 0:1], gates[:, 0:1], 0.0)
        for k in range(1, TOP_K):
            w = w + jnp.where(ci == pos[:, k:k + 1], gates[:, k:k + 1], 0.0)
        ws.append(w)
    wh = [w.astype(BF16) for w in ws]
    wl = [(w - hi.astype(F32)).astype(BF16) for w, hi in zip(ws, wh)]
    ys = [_slabs_to_rows(stage_ref, stage_row(slot, h, 0), rows, sub) for h in tiles]
    yh = [y.astype(BF16) for y in ys]
    yl = [(y - hi.astype(F32)).astype(BF16) for y, hi in zip(ys, yh)]
    for h in tiles:
        ffn = (jnp.dot(wh[h], yh[h], preferred_element_type=F32)
               + jnp.dot(wl[h], yh[h], preferred_element_type=F32)
               + jnp.dot(wh[h], yl[h], preferred_element_type=F32))
        rs = slice(h * tm, (h + 1) * tm)
        o_ref[rs, :] = _layer_norm(DEEPNORM_ALPHA * x1_ref[rs, :] + ffn, g2_ref[...], b2_ref[...])


def _combine(y, pos_tok, gates_tok, tcnt, off, dst, x1, g2, b2):
    n, d = x1.shape
    tm = MOE_TM * COMBINE_SUB
    sub = d // LANES
    row = lambda i, *_: (i, 0)
    const = lambda i, *_: (0, 0)
    return pl.pallas_call(
        _combine_kernel,
        out_shape=jax.ShapeDtypeStruct((n, d), F32),
        grid_spec=pltpu.PrefetchScalarGridSpec(
            num_scalar_prefetch=3,
            grid=(n // tm,),
            in_specs=[
                pl.BlockSpec((tm, TOP_K), row),
                pl.BlockSpec((tm, TOP_K), row),
                pl.BlockSpec((tm, d), row),
                pl.BlockSpec(g2.shape, const),
                pl.BlockSpec(b2.shape, const),
                pl.BlockSpec(memory_space=pl.ANY),
            ],
            out_specs=pl.BlockSpec((tm, d), row),
            scratch_shapes=[pltpu.VMEM((2 * TOP_K * tm * sub, LANES), F32),
                            pltpu.SemaphoreType.DMA((2,))],
        ),
        compiler_params=pltpu.CompilerParams(
            dimension_semantics=("arbitrary",), vmem_limit_bytes=VMEM_LIMIT),
        name="combine",
    )(tcnt, off, dst, pos_tok, gates_tok, x1, g2, b2, y)


def _head_pairs(w_a, w_b):
    d = w_a.shape[0]
    a = w_a.reshape(d, DIFF_HEADS, DIFF_HEAD_DIM)
    b = w_b.reshape(d, DIFF_HEADS, DIFF_HEAD_DIM)
    return jnp.concatenate([a, b], axis=-1).reshape(d, DIFF_HEADS * V_HEAD_DIM)


def _lookup(table, eid):
    eids = jnp.arange(N_EXPERTS, dtype=jnp.int32)[:, None, None]
    return jnp.sum(jnp.where(eid[None] == eids, table[:, None, :], 0), axis=0)


def kernel(x, w_in, lambda_q1, lambda_k1, lambda_q2, lambda_k2, subln_g, gmlp_ln_g, gmlp_ln_b,
           w_spatial, b_spatial, w_o, ln1_g, ln1_b, w_router, b_router, w_up, b_up,
           w_down, b_down, ln2_g, ln2_b):
    b, s, d = x.shape
    n = b * s
    x2 = x.reshape(n, d)

    w = w_in[0]
    c = QK_WIDTH
    scale = DIFF_HEAD_DIM ** -0.5 * LOG2_E
    wq = _head_pairs(w[:, 0:c], w[:, c:2 * c]) * scale
    wk = _head_pairs(w[:, 2 * c:3 * c], w[:, 3 * c:4 * c])
    wqk = jnp.concatenate([wq, wk], axis=1).astype(BF16)
    wvt = w[:, 4 * c:4 * c + DIFF_WIDTH].T.astype(BF16)
    wg = w[:, 4 * c + DIFF_WIDTH:].astype(BF16)
    lamv = jnp.concatenate([lambda_q1, lambda_k1, lambda_q2, lambda_k2], axis=0)
    bsp = jnp.repeat(b_spatial[0].T, GMLP_GROUP_DIM, axis=1)

    assert w_in.shape[0] == DEPTH and d % (SUBLANES * LANES) == 0
    assert PROJ_TM == ATTN_T and s % (ATTN_T * ATTN_GROUP) == 0
    assert n % (MOE_TM * max(MIX_SUB, COMBINE_SUB, DISPATCH_SUB)) == 0
    qq, kk, vt, u, vn = _proj(x2, wqk, wvt, wg, gmlp_ln_g, gmlp_ln_b)
    attn = _attention(qq.reshape(b, s, -1), kk.reshape(b, s, -1),
                      vt.reshape(b, s // ATTN_T, DIFF_WIDTH, ATTN_T),
                      lamv, subln_g.reshape(V_HEAD_DIM, 1)).reshape(n, DIFF_WIDTH)
    x1, eid, gates, rank, cnt, tcnt_l = _mix(attn, u, vn, x2, w_spatial[0], bsp,
                                             w_o[0].astype(BF16), ln1_g, ln1_b,
                                             w_router[0].T, b_router[0][:, None])

    ntiles = n // MOE_TM
    counts = cnt[:, 0].astype(jnp.int32)
    padded = ((counts + FFN_TM - 1) // FFN_TM) * FFN_TM
    end_padded = jnp.cumsum(padded)
    start_padded = end_padded - padded
    tcnt = tcnt_l.reshape(ntiles, N_EXPERTS, LANES)[:, :, 0].astype(jnp.int32)
    before = jnp.cumsum(tcnt, axis=0) - tcnt
    off = jnp.cumsum(tcnt, axis=1) - tcnt
    dst = start_padded[None, :] + before
    tile_base = jnp.repeat((off - before).T, MOE_TM, axis=1)
    pos = _lookup(tile_base, eid) + rank

    p_rows = n * TOP_K + N_EXPERTS * FFN_TM
    nblk = p_rows // FFN_TM
    nused = (end_padded[-1:] // FFN_TM).astype(jnp.int32)
    blk_start = jnp.arange(nblk, dtype=jnp.int32) * FFN_TM
    blk_e = jnp.minimum(jnp.sum((end_padded[None, :] <= blk_start[:, None]).astype(jnp.int32), axis=1),
                        N_EXPERTS - 1)
    tcnt_f, off_f, dst_f = tcnt.reshape(-1), off.reshape(-1), dst.reshape(-1)

    xs = _dispatch(x1, pos, tcnt_f, off_f, dst_f, start_padded + counts, padded - counts, nused,
                   p_rows)
    e_ids = jnp.arange(N_EXPERTS, dtype=jnp.int32)
    later_used = jnp.logical_and(e_ids[None, :] > e_ids[:, None], padded[None, :] > 0)
    next_e = jnp.min(jnp.where(later_used, e_ids[None, :], N_EXPERTS), axis=1)
    next_e = jnp.where(next_e == N_EXPERTS, e_ids, next_e)
    is_e = blk_e[:, None] == e_ids[None, :]
    filled = jnp.sum(jnp.where(is_e, (start_padded + counts)[None, :], 0), axis=1) - blk_start
    blk_valid = jnp.clip(filled, 0, FFN_TM).astype(jnp.int32)
    y = _ffn(xs, blk_e, nused, next_e, blk_valid, w_up[0], b_up[0][:, None, :], w_down[0],
             b_down[0][:, None, :])
    out = _combine(y, pos.T, gates.T, tcnt_f, off_f, dst_f, x1, ln2_g, ln2_b)
    return out.reshape(b, s, d)
```

```python
import jax
import jax.numpy as jnp
from jax import lax
from jax.experimental import pallas as pl
from jax.experimental.pallas import tpu as pltpu

DIFF_HEADS = 4
DIFF_HEAD_DIM = 64
V_HEAD_DIM = 2 * DIFF_HEAD_DIM
QK_WIDTH = DIFF_HEADS * DIFF_HEAD_DIM
DIFF_WIDTH = DIFF_HEADS * V_HEAD_DIM
GMLP_GROUPS = 8
GMLP_GROUP_DIM = 64
GMLP_WIDTH = GMLP_GROUPS * GMLP_GROUP_DIM
CHUNK = 128
N_EXPERTS = 32
TOP_K = 4
SWIGLU_LIMIT = 7.0
SWIGLU_ALPHA = 1.702
LN_EPS = 1e-5
DEPTH = 1
DEEPNORM_ALPHA = (2.0 * DEPTH) ** 0.25
LAMBDA_INIT = 0.8 - 0.6 * 1.0
LOG2_E = 1.4426950408889634

LANES = 128
SUBLANES = 8

PROJ_TM = 512
PROJ_CHUNK = 256
ATTN_T = 512
ATTN_GROUP = 4
MOE_TM = 256
MIX_SUB = 4
COMBINE_SUB = 2
DISPATCH_SUB = 2
FFN_TM = 512
WEIGHT_CAST_COLS = 256
RUN_BITS = MOE_TM.bit_length()
VMEM_LIMIT = 48 * 1024 * 1024

BF16 = jnp.bfloat16
F32 = jnp.float32


def _layer_norm(y, g, b):
    mu = jnp.mean(y, axis=-1, keepdims=True)
    yc = y - mu
    var = jnp.mean(yc * yc, axis=-1, keepdims=True)
    return yc * lax.rsqrt(var + LN_EPS) * g + b


def _gelu(x):
    return 0.5 * x * (1.0 + lax.erf(x * (2.0 ** -0.5)))


def _nt_dot(a, b):
    return lax.dot_general(a, b, (((1,), (1,)), ((), ())), preferred_element_type=F32)


def _slabs_to_rows(ref, first, nrows, sub):
    return jnp.concatenate([ref[pl.ds(first + j, nrows, stride=sub), :] for j in range(sub)], axis=1)


def _rows_to_slabs(ref, first, rows, sub):
    for j in range(sub):
        ref[pl.ds(first + j, rows.shape[0], stride=sub), :] = rows[:, j * LANES:(j + 1) * LANES]


def _proj_kernel(x_ref, wqk_ref, wvt_ref, wg_ref, lng_ref, lnb_ref,
                 qq_ref, kk_ref, vt_ref, u_ref, vn_ref):
    xb = x_ref[...].astype(BF16)

    def gelu_cols(lo):
        return _gelu(jnp.dot(xb, wg_ref[:, lo:lo + PROJ_CHUNK], preferred_element_type=F32))

    gv = [gelu_cols(GMLP_WIDTH + c) for c in range(0, GMLP_WIDTH, PROJ_CHUNK)]
    qk = jnp.dot(xb, wqk_ref[...], preferred_element_type=F32)
    qq_ref[...] = qk[:, :DIFF_WIDTH].astype(BF16)
    kk_ref[...] = qk[:, DIFF_WIDTH:].astype(BF16)
    for c in range(0, GMLP_WIDTH, PROJ_CHUNK):
        u_ref[:, c:c + PROJ_CHUNK] = gelu_cols(c).astype(BF16)
    vt_ref[0] = _nt_dot(wvt_ref[...], xb).astype(BF16)
    vn_ref[...] = _layer_norm(jnp.concatenate(gv, axis=1), lng_ref[...], lnb_ref[...]).astype(BF16)


def _proj(x2, wqk, wvt, wg, lng, lnb):
    n, d = x2.shape
    tm = PROJ_TM
    row = lambda i: (i, 0)
    const = lambda i: (0, 0)
    out = jax.ShapeDtypeStruct((n, DIFF_WIDTH), BF16)
    vt_out = jax.ShapeDtypeStruct((n // tm, DIFF_WIDTH, tm), BF16)
    rowspec = pl.BlockSpec((tm, DIFF_WIDTH), row)
    return pl.pallas_call(
        _proj_kernel,
        out_shape=(out, out, vt_out, out, out),
        grid=(n // tm,),
        in_specs=[
            pl.BlockSpec((tm, d), row),
            pl.BlockSpec(wqk.shape, const),
            pl.BlockSpec(wvt.shape, const),
            pl.BlockSpec(wg.shape, const),
            pl.BlockSpec(lng.shape, const),
            pl.BlockSpec(lnb.shape, const),
        ],
        out_specs=[rowspec, rowspec, pl.BlockSpec((1, DIFF_WIDTH, tm), lambda i: (i, 0, 0)),
                   rowspec, rowspec],
        compiler_params=pltpu.CompilerParams(
            dimension_semantics=("arbitrary",), vmem_limit_bytes=VMEM_LIMIT),
        name="proj",
    )(x2, wqk, wvt, wg, lng, lnb)


def _attn_kernel(lamv_ref, g_ref, qq_ref, kk_ref, vt_ref, o_ref, s_ref):
    t = ATTN_T
    nq = qq_ref.shape[1] // t
    starts = (0, 1, 1, 0)

    def group(g, carry):
        for r in range(ATTN_GROUP):
            _attn_q_tile(ATTN_GROUP * g + r, nq, r % 2, starts[r],
                         lamv_ref, g_ref, qq_ref, kk_ref, vt_ref, o_ref, s_ref)
        return carry

    q1, q2 = _masked_queries(qq_ref, 0)
    k0 = kk_ref[0, 0:t, :]
    s_ref[0, 0] = _nt_dot(k0, q1)
    s_ref[0, 1] = _nt_dot(k0, q2)
    lax.fori_loop(0, nq // ATTN_GROUP, group, 0)


def _masked_queries(qq_ref, qi):
    t = ATTN_T
    q = qq_ref[0, pl.ds(pl.multiple_of(qi * t, t), t), :]
    first = lax.broadcasted_iota(jnp.int32, q.shape, 1) < DIFF_HEAD_DIM
    zero = jnp.zeros_like(q)
    return jnp.where(first, q, zero), jnp.where(first, zero, q)


def _attn_q_tile(qi, nq, odd, start, lamv_ref, g_ref, qq_ref, kk_ref, vt_ref, o_ref, s_ref):
    t = ATTN_T
    q_rows = pl.ds(pl.multiple_of(qi * t, t), t)
    q1, q2 = _masked_queries(qq_ref, qi)
    next_q = _masked_queries(qq_ref, jnp.minimum(qi + 1, nq - 1))

    def keys(j):
        return kk_ref[0, pl.ds(pl.multiple_of(j * t, t), t), :]

    def consume(j, buf, carry, masked, nxt):
        vt = vt_ref[0, j]
        k_next = keys(nxt[0])
        new = []
        for mp, ((m, l, acc), qh) in enumerate(zip(carry, nxt[2])):
            s_ref[nxt[1], mp] = _nt_dot(k_next, qh)
            s = s_ref[buf, mp]
            if masked:
                key = lax.broadcasted_iota(jnp.int32, s.shape, 0)
                qry = lax.broadcasted_iota(jnp.int32, s.shape, 1)
                s = jnp.where(key <= qry, s, -jnp.inf)
            m_new = jnp.maximum(m, jnp.max(s, axis=0, keepdims=True))
            alpha = jnp.exp2(m - m_new)
            p = jnp.exp2(s - m_new)
            l_new = alpha * l + jnp.sum(p, axis=0, keepdims=True)
            acc_new = alpha * acc + jnp.dot(vt, p.astype(BF16), preferred_element_type=F32)
            new.append((m_new, l_new, acc_new))
        return tuple(new)

    def init():
        return (jnp.full((1, t), -jnp.inf, F32), jnp.zeros((1, t), F32),
                jnp.zeros((V_HEAD_DIM, t), F32))

    def finish(carry):
        (_, l1, acc1), (_, l2, acc2) = carry
        lv = lamv_ref[...]
        lam = (jnp.exp(jnp.sum(lv[0:1] * lv[1:2], axis=-1, keepdims=True))
               - jnp.exp(jnp.sum(lv[2:3] * lv[3:4], axis=-1, keepdims=True)) + LAMBDA_INIT)
        o = acc1 / l1 - lam * (acc2 / l2)
        ms = jnp.mean(o * o, axis=0, keepdims=True)
        o = o * lax.rsqrt(ms + LN_EPS) * g_ref[...] * (1.0 - LAMBDA_INIT)
        o_ref[0, q_rows, :] = o.T.astype(BF16)

    own = (q1, q2)
    a, b = start, 1 - start

    def pair(jj, c):
        j = 2 * jj
        c = consume(j, a, c, False, (j + 1, b, own))
        return consume(j + 1, b, c, False, (j + 2, a, own))

    carry = lax.fori_loop(0, qi // 2, pair, (init(), init()))
    if odd:
        carry = consume(qi - 1, a, carry, False, (qi, b, own))
        finish(consume(qi, b, carry, True, (0, a, next_q)))
    else:
        finish(consume(qi, a, carry, True, (0, b, next_q)))


def _attention(qq, kk, vt, lamv, subln_g_col):
    b, s, _ = qq.shape
    t = ATTN_T
    seqspec = pl.BlockSpec((1, s, V_HEAD_DIM), lambda bi, h: (bi, 0, h))
    vtspec = pl.BlockSpec((1, s // t, V_HEAD_DIM, t), lambda bi, h: (bi, 0, h, 0))
    const = lambda bi, h: (0, 0)
    return pl.pallas_call(
        _attn_kernel,
        out_shape=jax.ShapeDtypeStruct((b, s, DIFF_WIDTH), BF16),
        grid=(b, DIFF_HEADS),
        in_specs=[pl.BlockSpec(lamv.shape, const), pl.BlockSpec(subln_g_col.shape, const),
                  seqspec, seqspec, vtspec],
        out_specs=seqspec,
        scratch_shapes=[pltpu.VMEM((2, 2, t, t), F32)],
        compiler_params=pltpu.CompilerParams(
            dimension_semantics=("arbitrary",) * 2, vmem_limit_bytes=VMEM_LIMIT),
        name="attn",
    )(lamv, subln_g_col, qq, kk, vt)


def _mix_kernel(attn_ref, u_ref, vn_ref, x_ref, wsp_ref, bsp_ref, wo_ref, g1_ref, b1_ref,
                wrt_ref, br_ref,
                x1_ref, eid_ref, gate_ref, rank_ref, cnt_ref, tcnt_ref,
                cat_ref, carry_ref):
    tm = MOE_TM

    @pl.when(pl.program_id(0) == 0)
    def _():
        carry_ref[...] = jnp.zeros_like(carry_ref)

    ri = lax.broadcasted_iota(jnp.int32, (CHUNK, CHUNK), 0)
    ci = lax.broadcasted_iota(jnp.int32, (CHUNK, CHUNK), 1)
    tril = ci <= ri
    first = ci < GMLP_GROUP_DIM
    w = [jnp.where(tril, wsp_ref[g], 0.0).astype(BF16) for g in range(GMLP_GROUPS)]
    wr = wrt_ref[...]
    wh = wr.astype(BF16)
    wl = (wr - wh.astype(F32)).astype(BF16)
    whl = jnp.concatenate([wh, wl], axis=0)
    eio = lax.broadcasted_iota(jnp.int32, (N_EXPERTS, tm), 0).astype(F32)
    ti = lax.broadcasted_iota(jnp.int32, (tm, tm), 0)
    tj = lax.broadcasted_iota(jnp.int32, (tm, tm), 1)
    before = jnp.where(ti < tj, 1.0, 0.0).astype(BF16)

    halves = [slice(h * tm, (h + 1) * tm) for h in range(MIX_SUB)]

    for rs in halves:
        cat_ref[rs, :DIFF_WIDTH] = attn_ref[rs, :]
    for c in range(MIX_SUB * tm // CHUNK):
        rows = slice(c * CHUNK, (c + 1) * CHUNK)
        for jb in range(GMLP_WIDTH // LANES):
            cols = slice(jb * LANES, (jb + 1) * LANES)
            vb = vn_ref[rows, cols]
            zero = jnp.zeros_like(vb)
            z = (jnp.dot(w[2 * jb], jnp.where(first, vb, zero), preferred_element_type=F32)
                 + jnp.dot(w[2 * jb + 1], jnp.where(first, zero, vb), preferred_element_type=F32))
            gated = u_ref[rows, cols].astype(F32) * (z + bsp_ref[:, cols])
            cat_ref[rows, DIFF_WIDTH + jb * LANES:DIFF_WIDTH + (jb + 1) * LANES] = gated.astype(BF16)

    x1s = []
    for rs in halves:
        mixed = jnp.dot(cat_ref[rs, :], wo_ref[...], preferred_element_type=F32)
        x1 = _layer_norm(DEEPNORM_ALPHA * x_ref[rs, :] + mixed, g1_ref[...], b1_ref[...])
        x1_ref[rs, :] = x1
        x1s.append(x1)

    curs = []
    for x1 in x1s:
        xh = x1.astype(BF16)
        xl = (x1 - xh.astype(F32)).astype(BF16)
        both = _nt_dot(whl, xh)
        curs.append(both[:N_EXPERTS] + both[N_EXPERTS:] + _nt_dot(wh, xl) + br_ref[...])

    vals = [[] for _ in halves]
    idxs = [[] for _ in halves]
    sels = [[] for _ in halves]
    for _ in range(TOP_K):
        for h in range(MIX_SUB):
            mx = jnp.max(curs[h], axis=0, keepdims=True)
            idx = jnp.min(jnp.where(curs[h] == mx, eio, float(N_EXPERTS)), axis=0, keepdims=True)
            sel = eio == idx
            vals[h].append(mx)
            idxs[h].append(idx)
            sels[h].append(sel)
            curs[h] = jnp.where(sel, -jnp.inf, curs[h])

    carry = carry_ref[:, 0:1]
    for h, rs in enumerate(halves):
        ex = [jnp.exp(vk - vals[h][0]) for vk in vals[h]]
        denom = ex[0] + ex[1] + ex[2] + ex[3]
        gate_ref[:, rs] = jnp.concatenate([e / denom for e in ex], axis=0)
        eid_ref[:, rs] = jnp.concatenate(idxs[h], axis=0).astype(jnp.int32)

        chosen = (sels[h][0] | sels[h][1] | sels[h][2] | sels[h][3])
        onehot = jnp.where(chosen, 1.0, 0.0)
        cnt_before = jnp.dot(onehot.astype(BF16), before, preferred_element_type=F32) + carry
        ranks = [jnp.sum(jnp.where(s, cnt_before, 0.0), axis=0, keepdims=True) for s in sels[h]]
        rank_ref[:, rs] = jnp.concatenate(ranks, axis=0).astype(jnp.int32)
        tile_cnt = jnp.sum(onehot, axis=1, keepdims=True)
        tcnt_ref[h * N_EXPERTS:(h + 1) * N_EXPERTS, :] = jnp.broadcast_to(tile_cnt, (N_EXPERTS, LANES))
        carry = carry + tile_cnt

    carry_ref[...] = jnp.broadcast_to(carry, carry_ref.shape)
    cnt_ref[...] = jnp.broadcast_to(carry, cnt_ref.shape)


def _mix(attn, u, vn, x2, wsp, bsp, wo, g1, b1, wrt, br):
    n, d = x2.shape
    tm = MOE_TM * MIX_SUB
    row = lambda i: (i, 0)
    col = lambda i: (0, i)
    const2 = lambda i: (0, 0)
    const3 = lambda i: (0, 0, 0)
    tok = lambda dt: jax.ShapeDtypeStruct((TOP_K, n), dt)
    return pl.pallas_call(
        _mix_kernel,
        out_shape=(jax.ShapeDtypeStruct((n, d), F32), tok(jnp.int32), tok(F32), tok(jnp.int32),
                   jax.ShapeDtypeStruct((N_EXPERTS, LANES), F32),
                   jax.ShapeDtypeStruct((n // MOE_TM * N_EXPERTS, LANES), F32)),
        grid=(n // tm,),
        in_specs=[
            pl.BlockSpec((tm, DIFF_WIDTH), row),
            pl.BlockSpec((tm, GMLP_WIDTH), row),
            pl.BlockSpec((tm, GMLP_WIDTH), row),
            pl.BlockSpec((tm, d), row),
            pl.BlockSpec(wsp.shape, const3),
            pl.BlockSpec(bsp.shape, const2),
            pl.BlockSpec(wo.shape, const2),
            pl.BlockSpec(g1.shape, const2),
            pl.BlockSpec(b1.shape, const2),
            pl.BlockSpec(wrt.shape, const2),
            pl.BlockSpec(br.shape, const2),
        ],
        out_specs=[
            pl.BlockSpec((tm, d), row),
            pl.BlockSpec((TOP_K, tm), col),
            pl.BlockSpec((TOP_K, tm), col),
            pl.BlockSpec((TOP_K, tm), col),
            pl.BlockSpec((N_EXPERTS, LANES), const2),
            pl.BlockSpec((MIX_SUB * N_EXPERTS, LANES), row),
        ],
        scratch_shapes=[pltpu.VMEM((tm, DIFF_WIDTH + GMLP_WIDTH), BF16),
                        pltpu.VMEM((N_EXPERTS, LANES), F32)],
        compiler_params=pltpu.CompilerParams(
            dimension_semantics=("arbitrary",), vmem_limit_bytes=VMEM_LIMIT),
        name="mix",
    )(attn, u, vn, x2, wsp, bsp, wo, g1, b1, wrt, br)


def _for_each_run(tile, tcnt_ref, off_ref, dst_ref, make_copy):
    def per_expert(e, carry):
        n = tcnt_ref[tile * N_EXPERTS + e]
        off = off_ref[tile * N_EXPERTS + e]
        dst = dst_ref[tile * N_EXPERTS + e]
        for bit in range(RUN_BITS):
            done = (n >> (bit + 1)) << (bit + 1)

            @pl.when(((n >> bit) & 1) == 1)
            def _():
                make_copy(off + done, dst + done, 1 << bit).start()
        return carry

    lax.fori_loop(0, N_EXPERTS, per_expert, 0, unroll=2)


def _dispatch_kernel(tcnt_ref, off_ref, dst_ref, pad_start_ref, pad_len_ref, nused_ref,
                     pos_ref, x1_ref, xs_hbm, sorted_ref, zero_ref, sem, zsem):
    tm = MOE_TM
    rows = TOP_K * tm
    d = x1_ref.shape[1]
    sub = d // LANES
    nblk = xs_hbm.shape[0] // (FFN_TM * sub)
    i = pl.program_id(0)
    last = pl.num_programs(0) - 1
    slot = lax.rem(i, 2)

    def sorted_row(s, h, local_row):
        return pl.multiple_of(((s * DISPATCH_SUB + h) * rows + local_row) * sub, sub)

    def wait_runs(s):
        pltpu.make_async_copy(sorted_ref.at[pl.ds(sorted_row(s, 0, 0), DISPATCH_SUB * rows * sub)],
                              xs_hbm.at[pl.ds(0, DISPATCH_SUB * rows * sub)], sem.at[s]).wait()

    @pl.when(i >= 2)
    def _():
        wait_runs(slot)

    ri = lax.broadcasted_iota(jnp.int32, (rows, tm), 0)
    perms = []
    for h in range(DISPATCH_SUB):
        pos = pos_ref[:, h * tm:(h + 1) * tm]
        hit = (ri == pos[0:1]) | (ri == pos[1:2]) | (ri == pos[2:3]) | (ri == pos[3:4])
        perms.append(jnp.where(hit, 1.0, 0.0).astype(BF16))
    srts = [jnp.dot(perms[h], x1_ref[h * tm:(h + 1) * tm, :].astype(BF16),
                    preferred_element_type=F32) for h in range(DISPATCH_SUB)]
    for h in range(DISPATCH_SUB):
        _rows_to_slabs(sorted_ref, sorted_row(slot, h, 0), srts[h], sub)

    for h in range(DISPATCH_SUB):
        def run_copy(local_row, global_row, nrows, h=h):
            return pltpu.make_async_copy(
                sorted_ref.at[pl.ds(sorted_row(slot, h, local_row), nrows * sub)],
                xs_hbm.at[pl.ds(pl.multiple_of(global_row * sub, sub), nrows * sub)], sem.at[slot])

        _for_each_run(i * DISPATCH_SUB + h, tcnt_ref, off_ref, dst_ref, run_copy)

    @pl.when(i == 0)
    def _():
        zero_ref[...] = jnp.zeros_like(zero_ref)

        def zero_copy(r, nrows):
            return pltpu.make_async_copy(
                zero_ref.at[pl.ds(0, nrows * sub)],
                xs_hbm.at[pl.ds(pl.multiple_of(r * sub, sub), nrows * sub)], zsem)

        def tail_copy(blk):
            return pltpu.make_async_copy(
                zero_ref, xs_hbm.at[pl.ds(pl.multiple_of(blk * (FFN_TM * sub), FFN_TM * sub),
                                          FFN_TM * sub)], zsem)

        def tail_start(blk, c):
            tail_copy(blk).start()
            return c

        def tail_wait(blk, c):
            tail_copy(blk).wait()
            return c

        lax.fori_loop(nused_ref[0], nblk, tail_start, 0)
        lax.fori_loop(nused_ref[0], nblk, tail_wait, 0)

        def per_expert(e, _):
            s = pad_start_ref[e]
            cnt = pad_len_ref[e]
            for wait in (False, True):
                for bit in range(FFN_TM.bit_length() - 1):
                    done = (cnt >> (bit + 1)) << (bit + 1)

                    @pl.when(((cnt >> bit) & 1) == 1)
                    def _():
                        piece = zero_copy(s + done, 1 << bit)
                        piece.wait() if wait else piece.start()
            return 0

        lax.fori_loop(0, N_EXPERTS, per_expert, 0)

    @pl.when(i == last)
    def _():
        wait_runs(slot)

        @pl.when(last >= 1)
        def _():
            wait_runs(1 - slot)


def _dispatch(x1, pos, tcnt, off, dst, pad_start, pad_len, nused, p_rows):
    n, d = x1.shape
    tm = MOE_TM * DISPATCH_SUB
    sub = d // LANES
    return pl.pallas_call(
        _dispatch_kernel,
        out_shape=jax.ShapeDtypeStruct((p_rows * sub, LANES), F32),
        grid_spec=pltpu.PrefetchScalarGridSpec(
            num_scalar_prefetch=6,
            grid=(n // tm,),
            in_specs=[
                pl.BlockSpec((TOP_K, tm), lambda i, *_: (0, i)),
                pl.BlockSpec((tm, d), lambda i, *_: (i, 0)),
            ],
            out_specs=pl.BlockSpec(memory_space=pl.ANY),
            scratch_shapes=[pltpu.VMEM((2 * TOP_K * tm * sub, LANES), F32),
                            pltpu.VMEM((FFN_TM * sub, LANES), F32),
                            pltpu.SemaphoreType.DMA((2,)), pltpu.SemaphoreType.DMA(())],
        ),
        compiler_params=pltpu.CompilerParams(
            dimension_semantics=("arbitrary",), vmem_limit_bytes=VMEM_LIMIT),
        name="dispatch",
    )(tcnt, off, dst, pad_start, pad_len, nused, pos, x1)


def _ffn_kernel(blk_e_ref, nused_ref, next_e_ref, valid_ref, xs_ref, wup_hbm, bup_ref, wdn_hbm,
                bdn_ref, y_ref, wup_f, wdn_f, wup_b, wdn_b, wsem):
    tm = FFN_TM
    dff = wdn_f.shape[0]
    sub = wdn_f.shape[1] // LANES
    i = pl.program_id(0)
    used = i < nused_ref[0]

    @pl.when(jnp.logical_not(used))
    def _():
        y_ref[...] = jnp.zeros_like(y_ref)

    def weight_copies(e):
        return (pltpu.make_async_copy(wup_hbm.at[e], wup_f, wsem.at[0]),
                pltpu.make_async_copy(wdn_hbm.at[e], wdn_f, wsem.at[1]))

    e = blk_e_ref[i]
    new_expert = jnp.logical_or(i == 0, e != blk_e_ref[jnp.maximum(i - 1, 0)])

    @pl.when(jnp.logical_and(used, new_expert))
    def _():
        @pl.when(i == 0)
        def _():
            for cp in weight_copies(e):
                cp.start()

        for cp in weight_copies(e):
            cp.wait()
        for c in range(0, 2 * dff, WEIGHT_CAST_COLS):
            wup_b[:, c:c + WEIGHT_CAST_COLS] = wup_f[:, c:c + WEIGHT_CAST_COLS].astype(BF16)
        for c in range(0, wdn_f.shape[1], WEIGHT_CAST_COLS):
            wdn_b[:, c:c + WEIGHT_CAST_COLS] = wdn_f[:, c:c + WEIGHT_CAST_COLS].astype(BF16)

        @pl.when(next_e_ref[e] != e)
        def _():
            for cp in weight_copies(next_e_ref[e]):
                cp.start()

    def expert_mlp(nrows):
        xb = _slabs_to_rows(xs_ref, 0, nrows, sub).astype(BF16)
        h = jnp.dot(xb, wup_b[...], preferred_element_type=F32) + bup_ref[0]
        gate = jnp.minimum(h[:, :dff], SWIGLU_LIMIT)
        lin = jnp.clip(h[:, dff:], -SWIGLU_LIMIT, SWIGLU_LIMIT)
        act = (lin + 1.0) * gate * jax.nn.sigmoid(SWIGLU_ALPHA * gate)
        y = jnp.dot(act.astype(BF16), wdn_b[...], preferred_element_type=F32) + bdn_ref[0]
        _rows_to_slabs(y_ref, 0, y, sub)

    half = tm // 2
    more_than_half = valid_ref[i] > half

    @pl.when(jnp.logical_and(used, more_than_half))
    def _():
        expert_mlp(tm)

    @pl.when(jnp.logical_and(used, jnp.logical_not(more_than_half)))
    def _():
        expert_mlp(half)
        y_ref[half * sub:, :] = jnp.zeros((half * sub, LANES), F32)


def _ffn(xs, blk_e, nused, next_e, blk_valid, wup, bup, wdn, bdn):
    tm = FFN_TM
    e, d, dff2 = wup.shape
    dff = dff2 // 2
    sub = d // LANES
    rowblk = lambda i, be, nu, ne, bv: (i, 0)
    bsel = lambda i, be, nu, ne, bv: (be[i], 0, 0)
    return pl.pallas_call(
        _ffn_kernel,
        out_shape=jax.ShapeDtypeStruct(xs.shape, F32),
        grid_spec=pltpu.PrefetchScalarGridSpec(
            num_scalar_prefetch=4,
            grid=(xs.shape[0] // (tm * sub),),
            in_specs=[
                pl.BlockSpec((tm * sub, LANES), rowblk),
                pl.BlockSpec(memory_space=pl.ANY),
                pl.BlockSpec((1, 1, dff2), bsel),
                pl.BlockSpec(memory_space=pl.ANY),
                pl.BlockSpec((1, 1, d), bsel),
            ],
            out_specs=pl.BlockSpec((tm * sub, LANES), rowblk),
            scratch_shapes=[pltpu.VMEM((d, dff2), F32), pltpu.VMEM((dff, d), F32),
                            pltpu.VMEM((d, dff2), BF16), pltpu.VMEM((dff, d), BF16),
                            pltpu.SemaphoreType.DMA((2,))],
        ),
        compiler_params=pltpu.CompilerParams(
            dimension_semantics=("arbitrary",), vmem_limit_bytes=VMEM_LIMIT),
        name="ffn",
    )(blk_e, nused, next_e, blk_valid, xs, wup, bup, wdn, bdn)


def _combine_kernel(tcnt_ref, off_ref, dst_ref, pos_ref, gate_ref, x1_ref, g2_ref, b2_ref, y_hbm,
                    o_ref, stage_ref, sem):
    tm = MOE_TM
    rows = TOP_K * tm
    sub = x1_ref.shape[1] // LANES
    i = pl.program_id(0)
    nsteps = pl.num_programs(0)
    slot = lax.rem(i, 2)

    def stage_row(s, h, local_row):
        return pl.multiple_of(((s * COMBINE_SUB + h) * rows + local_row) * sub, sub)

    def fetch_runs(step, s):
        for h in range(COMBINE_SUB):
            def run_copy(local_row, global_row, nrows, h=h):
                return pltpu.make_async_copy(
                    y_hbm.at[pl.ds(pl.multiple_of(global_row * sub, sub), nrows * sub)],
                    stage_ref.at[pl.ds(stage_row(s, h, local_row), nrows * sub)], sem.at[s])
            _for_each_run(step * COMBINE_SUB + h, tcnt_ref, off_ref, dst_ref, run_copy)

    @pl.when(i == 0)
    def _():
        fetch_runs(0, 0)

    @pl.when(i + 1 < nsteps)
    def _():
        fetch_runs(i + 1, 1 - slot)

    pltpu.make_async_copy(y_hbm.at[pl.ds(0, COMBINE_SUB * rows * sub)],
                          stage_ref.at[pl.ds(stage_row(slot, 0, 0), COMBINE_SUB * rows * sub)],
                          sem.at[slot]).wait()

    tiles = range(COMBINE_SUB)
    ci = lax.broadcasted_iota(jnp.int32, (tm, rows), 1)
    ws = []
    for h in tiles:
        pos = pos_ref[h * tm:(h + 1) * tm, :]
        gates = gate_ref[h * tm:(h + 1) * tm, :]
        w = jnp.where(ci == pos[:, 0:1], gates[:, 0:1], 0.0)
        for k in range(1, TOP_K):
            w = w + jnp.where(ci == pos[:, k:k + 1], gates[:, k:k + 1], 0.0)
        ws.append(w)
    wh = [w.astype(BF16) for w in ws]
    wl = [(w - hi.astype(F32)).astype(BF16) for w, hi in zip(ws, wh)]
    ys = [_slabs_to_rows(stage_ref, stage_row(slot, h, 0), rows, sub) for h in tiles]
    yh = [y.astype(BF16) for y in ys]
    yl = [(y - hi.astype(F32)).astype(BF16) for y, hi in zip(ys, yh)]
    for h in tiles:
        ffn = (jnp.dot(wh[h], yh[h], preferred_element_type=F32)
               + jnp.dot(wl[h], yh[h], preferred_element_type=F32)
               + jnp.dot(wh[h], yl[h], preferred_element_type=F32))
        rs = slice(h * tm, (h + 1) * tm)
        o_ref[rs, :] = _layer_norm(DEEPNORM_ALPHA * x1_ref[rs, :] + ffn, g2_ref[...], b2_ref[...])


def _combine(y, pos_tok, gates_tok, tcnt, off, dst, x1, g2, b2):
    n, d = x1.shape
    tm = MOE_TM * COMBINE_SUB
    sub = d // LANES
    row = lambda i, *_: (i, 0)
    const = lambda i, *_: (0, 0)
    return pl.pallas_call(
        _combine_kernel,
        out_shape=jax.ShapeDtypeStruct((n, d), F32),
        grid_spec=pltpu.PrefetchScalarGridSpec(
            num_scalar_prefetch=3,
            grid=(n // tm,),
            in_specs=[
                pl.BlockSpec((tm, TOP_K), row),
                pl.BlockSpec((tm, TOP_K), row),
                pl.BlockSpec((tm, d), row),
                pl.BlockSpec(g2.shape, const),
                pl.BlockSpec(b2.shape, const),
                pl.BlockSpec(memory_space=pl.ANY),
            ],
            out_specs=pl.BlockSpec((tm, d), row),
            scratch_shapes=[pltpu.VMEM((2 * TOP_K * tm * sub, LANES), F32),
                            pltpu.SemaphoreType.DMA((2,))],
        ),
        compiler_params=pltpu.CompilerParams(
            dimension_semantics=("arbitrary",), vmem_limit_bytes=VMEM_LIMIT),
        name="combine",
    )(tcnt, off, dst, pos_tok, gates_tok, x1, g2, b2, y)


def _head_pairs(w_a, w_b):
    d = w_a.shape[0]
    a = w_a.reshape(d, DIFF_HEADS, DIFF_HEAD_DIM)
    b = w_b.reshape(d, DIFF_HEADS, DIFF_HEAD_DIM)
    return jnp.concatenate([a, b], axis=-1).reshape(d, DIFF_HEADS * V_HEAD_DIM)


def _lookup(table, eid):
    eids = jnp.arange(N_EXPERTS, dtype=jnp.int32)[:, None, None]
    return jnp.sum(jnp.where(eid[None] == eids, table[:, None, :], 0), axis=0)


def kernel(x, w_in, lambda_q1, lambda_k1, lambda_q2, lambda_k2, subln_g, gmlp_ln_g, gmlp_ln_b,
           w_spatial, b_spatial, w_o, ln1_g, ln1_b, w_router, b_router, w_up, b_up,
           w_down, b_down, ln2_g, ln2_b):
    b, s, d = x.shape
    n = b * s
    x2 = x.reshape(n, d)

    w = w_in[0]
    c = QK_WIDTH
    scale = DIFF_HEAD_DIM ** -0.5 * LOG2_E
    wq = _head_pairs(w[:, 0:c], w[:, c:2 * c]) * scale
    wk = _head_pairs(w[:, 2 * c:3 * c], w[:, 3 * c:4 * c])
    wqk = jnp.concatenate([wq, wk], axis=1).astype(BF16)
    wvt = w[:, 4 * c:4 * c + DIFF_WIDTH].T.astype(BF16)
    wg = w[:, 4 * c + DIFF_WIDTH:].astype(BF16)
    lamv = jnp.concatenate([lambda_q1, lambda_k1, lambda_q2, lambda_k2], axis=0)
    bsp = jnp.repeat(b_spatial[0].T, GMLP_GROUP_DIM, axis=1)

    assert w_in.shape[0] == DEPTH and d % (SUBLANES * LANES) == 0
    assert PROJ_TM == ATTN_T and s % (ATTN_T * ATTN_GROUP) == 0
    assert n % (MOE_TM * max(MIX_SUB, COMBINE_SUB, DISPATCH_SUB)) == 0
    qq, kk, vt, u, vn = _proj(x2, wqk, wvt, wg, gmlp_ln_g, gmlp_ln_b)
    attn = _attention(qq.reshape(b, s, -1), kk.reshape(b, s, -1),
                      vt.reshape(b, s // ATTN_T, DIFF_WIDTH, ATTN_T),
                      lamv, subln_g.reshape(V_HEAD_DIM, 1)).reshape(n, DIFF_WIDTH)
    x1, eid, gates, rank, cnt, tcnt_l = _mix(attn, u, vn, x2, w_spatial[0], bsp,
                                             w_o[0].astype(BF16), ln1_g, ln1_b,
                                             w_router[0].T, b_router[0][:, None])

    ntiles = n // MOE_TM
    counts = cnt[:, 0].astype(jnp.int32)
    padded = ((counts + FFN_TM - 1) // FFN_TM) * FFN_TM
    end_padded = jnp.cumsum(padded)
    start_padded = end_padded - padded
    tcnt = tcnt_l.reshape(ntiles, N_EXPERTS, LANES)[:, :, 0].astype(jnp.int32)
    before = jnp.cumsum(tcnt, axis=0) - tcnt
    off = jnp.cumsum(tcnt, axis=1) - tcnt
    dst = start_padded[None, :] + before
    tile_base = jnp.repeat((off - before).T, MOE_TM, axis=1)
    pos = _lookup(tile_base, eid) + rank

    p_rows = n * TOP_K + N_EXPERTS * FFN_TM
    nblk = p_rows // FFN_TM
    nused = (end_padded[-1:] // FFN_TM).astype(jnp.int32)
    blk_start = jnp.arange(nblk, dtype=jnp.int32) * FFN_TM
    blk_e = jnp.minimum(jnp.sum((end_padded[None, :] <= blk_start[:, None]).astype(jnp.int32), axis=1),
                        N_EXPERTS - 1)
    tcnt_f, off_f, dst_f = tcnt.reshape(-1), off.reshape(-1), dst.reshape(-1)

    xs = _dispatch(x1, pos, tcnt_f, off_f, dst_f, start_padded + counts, padded - counts, nused,
                   p_rows)
    e_ids = jnp.arange(N_EXPERTS, dtype=jnp.int32)
    later_used = jnp.logical_and(e_ids[None, :] > e_ids[:, None], padded[None, :] > 0)
    next_e = jnp.min(jnp.where(later_used, e_ids[None, :], N_EXPERTS), axis=1)
    next_e = jnp.where(next_e == N_EXPERTS, e_ids, next_e)
    is_e = blk_e[:, None] == e_ids[None, :]
    filled = jnp.sum(jnp.where(is_e, (start_padded + counts)[None, :], 0), axis=1) - blk_start
    blk_valid = jnp.clip(filled, 0, FFN_TM).astype(jnp.int32)
    y = _ffn(xs, blk_e, nused, next_e, blk_valid, w_up[0], b_up[0][:, None, :], w_down[0],
             b_down[0][:, None, :])
    out = _combine(y, pos.T, gates.T, tcnt_f, off_f, dst_f, x1, ln2_g, ln2_b)
    return out.reshape(b, s, d)
```

```python
import jax
import jax.numpy as jnp
from jax import lax
from jax.experimental import pallas as pl
from jax.experimental.pallas import tpu as pltpu

DIFF_HEADS = 4
DIFF_HEAD_DIM = 64
V_HEAD_DIM = 2 * DIFF_HEAD_DIM
QK_WIDTH = DIFF_HEADS * DIFF_HEAD_DIM
DIFF_WIDTH = DIFF_HEADS * V_HEAD_DIM
GMLP_GROUPS = 8
GMLP_GROUP_DIM = 64
GMLP_WIDTH = GMLP_GROUPS * GMLP_GROUP_DIM
CHUNK = 128
N_EXPERTS = 32
TOP_K = 4
SWIGLU_LIMIT = 7.0
SWIGLU_ALPHA = 1.702
LN_EPS = 1e-5
DEPTH = 1
DEEPNORM_ALPHA = (2.0 * DEPTH) ** 0.25
LAMBDA_INIT = 0.8 - 0.6 * 1.0
LOG2_E = 1.4426950408889634

LANES = 128
SUBLANES = 8

PROJ_TM = 512
PROJ_CHUNK = 256
ATTN_T = 512
ATTN_GROUP = 4
MOE_TM = 256
MIX_SUB = 4
COMBINE_SUB = 2
DISPATCH_SUB = 2
FFN_TM = 512
WEIGHT_CAST_COLS = 256
RUN_BITS = MOE_TM.bit_length()
VMEM_LIMIT = 48 * 1024 * 1024

BF16 = jnp.bfloat16
F32 = jnp.float32


def _layer_norm(y, g, b):
    mu = jnp.mean(y, axis=-1, keepdims=True)
    yc = y - mu
    var = jnp.mean(yc * yc, axis=-1, keepdims=True)
    return yc * lax.rsqrt(var + LN_EPS) * g + b


def _gelu(x):
    return 0.5 * x * (1.0 + lax.erf(x * (2.0 ** -0.5)))


def _nt_dot(a, b):
    return lax.dot_general(a, b, (((1,), (1,)), ((), ())), preferred_element_type=F32)


def _slabs_to_rows(ref, first, nrows, sub):
    return jnp.concatenate([ref[pl.ds(first + j, nrows, stride=sub), :] for j in range(sub)], axis=1)


def _rows_to_slabs(ref, first, rows, sub):
    for j in range(sub):
        ref[pl.ds(first + j, rows.shape[0], stride=sub), :] = rows[:, j * LANES:(j + 1) * LANES]


def _proj_kernel(x_ref, wqk_ref, wvt_ref, wg_ref, lng_ref, lnb_ref,
                 qq_ref, kk_ref, vt_ref, u_ref, vn_ref):
    xb = x_ref[...].astype(BF16)

    def gelu_cols(lo):
        return _gelu(jnp.dot(xb, wg_ref[:, lo:lo + PROJ_CHUNK], preferred_element_type=F32))

    gv = [gelu_cols(GMLP_WIDTH + c) for c in range(0, GMLP_WIDTH, PROJ_CHUNK)]
    qk = jnp.dot(xb, wqk_ref[...], preferred_element_type=F32)
    qq_ref[...] = qk[:, :DIFF_WIDTH].astype(BF16)
    kk_ref[...] = qk[:, DIFF_WIDTH:].astype(BF16)
    for c in range(0, GMLP_WIDTH, PROJ_CHUNK):
        u_ref[:, c:c + PROJ_CHUNK] = gelu_cols(c).astype(BF16)
    vt_ref[0] = _nt_dot(wvt_ref[...], xb).astype(BF16)
    vn_ref[...] = _layer_norm(jnp.concatenate(gv, axis=1), lng_ref[...], lnb_ref[...]).astype(BF16)


def _proj(x2, wqk, wvt, wg, lng, lnb):
    n, d = x2.shape
    tm = PROJ_TM
    row = lambda i: (i, 0)
    const = lambda i: (0, 0)
    out = jax.ShapeDtypeStruct((n, DIFF_WIDTH), BF16)
    vt_out = jax.ShapeDtypeStruct((n // tm, DIFF_WIDTH, tm), BF16)
    rowspec = pl.BlockSpec((tm, DIFF_WIDTH), row)
    return pl.pallas_call(
        _proj_kernel,
        out_shape=(out, out, vt_out, out, out),
        grid=(n // tm,),
        in_specs=[
            pl.BlockSpec((tm, d), row),
            pl.BlockSpec(wqk.shape, const),
            pl.BlockSpec(wvt.shape, const),
            pl.BlockSpec(wg.shape, const),
            pl.BlockSpec(lng.shape, const),
            pl.BlockSpec(lnb.shape, const),
        ],
        out_specs=[rowspec, rowspec, pl.BlockSpec((1, DIFF_WIDTH, tm), lambda i: (i, 0, 0)),
                   rowspec, rowspec],
        compiler_params=pltpu.CompilerParams(
            dimension_semantics=("arbitrary",), vmem_limit_bytes=VMEM_LIMIT),
        name="proj",
    )(x2, wqk, wvt, wg, lng, lnb)


def _attn_kernel(lamv_ref, g_ref, qq_ref, kk_ref, vt_ref, o_ref, s_ref):
    t = ATTN_T
    nq = qq_ref.shape[1] // t
    starts = (0, 1, 1, 0)

    def group(g, carry):
        for r in range(ATTN_GROUP):
            _attn_q_tile(ATTN_GROUP * g + r, nq, r % 2, starts[r],
                         lamv_ref, g_ref, qq_ref, kk_ref, vt_ref, o_ref, s_ref)
        return carry

    q1, q2 = _masked_queries(qq_ref, 0)
    k0 = kk_ref[0, 0:t, :]
    s_ref[0, 0] = _nt_dot(k0, q1)
    s_ref[0, 1] = _nt_dot(k0, q2)
    lax.fori_loop(0, nq // ATTN_GROUP, group, 0)


def _masked_queries(qq_ref, qi):
    t = ATTN_T
    q = qq_ref[0, pl.ds(pl.multiple_of(qi * t, t), t), :]
    first = lax.broadcasted_iota(jnp.int32, q.shape, 1) < DIFF_HEAD_DIM
    zero = jnp.zeros_like(q)
    return jnp.where(first, q, zero), jnp.where(first, zero, q)


def _attn_q_tile(qi, nq, odd, start, lamv_ref, g_ref, qq_ref, kk_ref, vt_ref, o_ref, s_ref):
    t = ATTN_T
    q_rows = pl.ds(pl.multiple_of(qi * t, t), t)
    q1, q2 = _masked_queries(qq_ref, qi)
    next_q = _masked_queries(qq_ref, jnp.minimum(qi + 1, nq - 1))

    def keys(j):
        return kk_ref[0, pl.ds(pl.multiple_of(j * t, t), t), :]

    def consume(j, buf, carry, masked, nxt):
        vt = vt_ref[0, j]
        k_next = keys(nxt[0])
        new = []
        for mp, ((m, l, acc), qh) in enumerate(zip(carry, nxt[2])):
            s_ref[nxt[1], mp] = _nt_dot(k_next, qh)
            s = s_ref[buf, mp]
            if masked:
                key = lax.broadcasted_iota(jnp.int32, s.shape, 0)
                qry = lax.broadcasted_iota(jnp.int32, s.shape, 1)
                s = jnp.where(key <= qry, s, -jnp.inf)
            m_new = jnp.maximum(m, jnp.max(s, axis=0, keepdims=True))
            alpha = jnp.exp2(m - m_new)
            p = jnp.exp2(s - m_new)
            l_new = alpha * l + jnp.sum(p, axis=0, keepdims=True)
            acc_new = alpha * acc + jnp.dot(vt, p.astype(BF16), preferred_element_type=F32)
            new.append((m_new, l_new, acc_new))
        return tuple(new)

    def init():
        return (jnp.full((1, t), -jnp.inf, F32), jnp.zeros((1, t), F32),
                jnp.zeros((V_HEAD_DIM, t), F32))

    def finish(carry):
        (_, l1, acc1), (_, l2, acc2) = carry
        lv = lamv_ref[...]
        lam = (jnp.exp(jnp.sum(lv[0:1] * lv[1:2], axis=-1, keepdims=True))
               - jnp.exp(jnp.sum(lv[2:3] * lv[3:4], axis=-1, keepdims=True)) + LAMBDA_INIT)
        o = acc1 / l1 - lam * (acc2 / l2)
        ms = jnp.mean(o * o, axis=0, keepdims=True)
        o = o * lax.rsqrt(ms + LN_EPS) * g_ref[...] * (1.0 - LAMBDA_INIT)
        o_ref[0, q_rows, :] = o.T.astype(BF16)

    own = (q1, q2)
    a, b = start, 1 - start

    def pair(jj, c):
        j = 2 * jj
        c = consume(j, a, c, False, (j + 1, b, own))
        return consume(j + 1, b, c, False, (j + 2, a, own))

    carry = lax.fori_loop(0, qi // 2, pair, (init(), init()))
    if odd:
        carry = consume(qi - 1, a, carry, False, (qi, b, own))
        finish(consume(qi, b, carry, True, (0, a, next_q)))
    else:
        finish(consume(qi, a, carry, True, (0, b, next_q)))


def _attention(qq, kk, vt, lamv, subln_g_col):
    b, s, _ = qq.shape
    t = ATTN_T
    seqspec = pl.BlockSpec((1, s, V_HEAD_DIM), lambda bi, h: (bi, 0, h))
    vtspec = pl.BlockSpec((1, s // t, V_HEAD_DIM, t), lambda bi, h: (bi, 0, h, 0))
    const = lambda bi, h: (0, 0)
    return pl.pallas_call(
        _attn_kernel,
        out_shape=jax.ShapeDtypeStruct((b, s, DIFF_WIDTH), BF16),
        grid=(b, DIFF_HEADS),
        in_specs=[pl.BlockSpec(lamv.shape, const), pl.BlockSpec(subln_g_col.shape, const),
                  seqspec, seqspec, vtspec],
        out_specs=seqspec,
        scratch_shapes=[pltpu.VMEM((2, 2, t, t), F32)],
        compiler_params=pltpu.CompilerParams(
            dimension_semantics=("arbitrary",) * 2, vmem_limit_bytes=VMEM_LIMIT),
        name="attn",
    )(lamv, subln_g_col, qq, kk, vt)


def _mix_kernel(attn_ref, u_ref, vn_ref, x_ref, wsp_ref, bsp_ref, wo_ref, g1_ref, b1_ref,
                wrt_ref, br_ref,
                x1_ref, eid_ref, gate_ref, rank_ref, cnt_ref, tcnt_ref,
                cat_ref, carry_ref):
    tm = MOE_TM

    @pl.when(pl.program_id(0) == 0)
    def _():
        carry_ref[...] = jnp.zeros_like(carry_ref)

    ri = lax.broadcasted_iota(jnp.int32, (CHUNK, CHUNK), 0)
    ci = lax.broadcasted_iota(jnp.int32, (CHUNK, CHUNK), 1)
    tril = ci <= ri
    first = ci < GMLP_GROUP_DIM
    w = [jnp.where(tril, wsp_ref[g], 0.0).astype(BF16) for g in range(GMLP_GROUPS)]
    wr = wrt_ref[...]
    wh = wr.astype(BF16)
    wl = (wr - wh.astype(F32)).astype(BF16)
    whl = jnp.concatenate([wh, wl], axis=0)
    eio = lax.broadcasted_iota(jnp.int32, (N_EXPERTS, tm), 0).astype(F32)
    ti = lax.broadcasted_iota(jnp.int32, (tm, tm), 0)
    tj = lax.broadcasted_iota(jnp.int32, (tm, tm), 1)
    before = jnp.where(ti < tj, 1.0, 0.0).astype(BF16)

    halves = [slice(h * tm, (h + 1) * tm) for h in range(MIX_SUB)]

    for rs in halves:
        cat_ref[rs, :DIFF_WIDTH] = attn_ref[rs, :]
    for c in range(MIX_SUB * tm // CHUNK):
        rows = slice(c * CHUNK, (c + 1) * CHUNK)
        for jb in range(GMLP_WIDTH // LANES):
            cols = slice(jb * LANES, (jb + 1) * LANES)
            vb = vn_ref[rows, cols]
            zero = jnp.zeros_like(vb)
            z = (jnp.dot(w[2 * jb], jnp.where(first, vb, zero), preferred_element_type=F32)
                 + jnp.dot(w[2 * jb + 1], jnp.where(first, zero, vb), preferred_element_type=F32))
            gated = u_ref[rows, cols].astype(F32) * (z + bsp_ref[:, cols])
            cat_ref[rows, DIFF_WIDTH + jb * LANES:DIFF_WIDTH + (jb + 1) * LANES] = gated.astype(BF16)

    x1s = []
    for rs in halves:
        mixed = jnp.dot(cat_ref[rs, :], wo_ref[...], preferred_element_type=F32)
        x1 = _layer_norm(DEEPNORM_ALPHA * x_ref[rs, :] + mixed, g1_ref[...], b1_ref[...])
        x1_ref[rs, :] = x1
        x1s.append(x1)

    curs = []
    for x1 in x1s:
        xh = x1.astype(BF16)
        xl = (x1 - xh.astype(F32)).astype(BF16)
        both = _nt_dot(whl, xh)
        curs.append(both[:N_EXPERTS] + both[N_EXPERTS:] + _nt_dot(wh, xl) + br_ref[...])

    vals = [[] for _ in halves]
    idxs = [[] for _ in halves]
    sels = [[] for _ in halves]
    for _ in range(TOP_K):
        for h in range(MIX_SUB):
            mx = jnp.max(curs[h], axis=0, keepdims=True)
            idx = jnp.min(jnp.where(curs[h] == mx, eio, float(N_EXPERTS)), axis=0, keepdims=True)
            sel = eio == idx
            vals[h].append(mx)
            idxs[h].append(idx)
            sels[h].append(sel)
            curs[h] = jnp.where(sel, -jnp.inf, curs[h])

    carry = carry_ref[:, 0:1]
    for h, rs in enumerate(halves):
        ex = [jnp.exp(vk - vals[h][0]) for vk in vals[h]]
        denom = ex[0] + ex[1] + ex[2] + ex[3]
        gate_ref[:, rs] = jnp.concatenate([e / denom for e in ex], axis=0)
        eid_ref[:, rs] = jnp.concatenate(idxs[h], axis=0).astype(jnp.int32)

        chosen = (sels[h][0] | sels[h][1] | sels[h][2] | sels[h][3])
        onehot = jnp.where(chosen, 1.0, 0.0)
        cnt_before = jnp.dot(onehot.astype(BF16), before, preferred_element_type=F32) + carry
        ranks = [jnp.sum(jnp.where(s, cnt_before, 0.0), axis=0, keepdims=True) for s in sels[h]]
        rank_ref[:, rs] = jnp.concatenate(ranks, axis=0).astype(jnp.int32)
        tile_cnt = jnp.sum(onehot, axis=1, keepdims=True)
        tcnt_ref[h * N_EXPERTS:(h + 1) * N_EXPERTS, :] = jnp.broadcast_to(tile_cnt, (N_EXPERTS, LANES))
        carry = carry + tile_cnt

    carry_ref[...] = jnp.broadcast_to(carry, carry_ref.shape)
    cnt_ref[...] = jnp.broadcast_to(carry, cnt_ref.shape)


def _mix(attn, u, vn, x2, wsp, bsp, wo, g1, b1, wrt, br):
    n, d = x2.shape
    tm = MOE_TM * MIX_SUB
    row = lambda i: (i, 0)
    col = lambda i: (0, i)
    const2 = lambda i: (0, 0)
    const3 = lambda i: (0, 0, 0)
    tok = lambda dt: jax.ShapeDtypeStruct((TOP_K, n), dt)
    return pl.pallas_call(
        _mix_kernel,
        out_shape=(jax.ShapeDtypeStruct((n, d), F32), tok(jnp.int32), tok(F32), tok(jnp.int32),
                   jax.ShapeDtypeStruct((N_EXPERTS, LANES), F32),
                   jax.ShapeDtypeStruct((n // MOE_TM * N_EXPERTS, LANES), F32)),
        grid=(n // tm,),
        in_specs=[
            pl.BlockSpec((tm, DIFF_WIDTH), row),
            pl.BlockSpec((tm, GMLP_WIDTH), row),
            pl.BlockSpec((tm, GMLP_WIDTH), row),
            pl.BlockSpec((tm, d), row),
            pl.BlockSpec(wsp.shape, const3),
            pl.BlockSpec(bsp.shape, const2),
            pl.BlockSpec(wo.shape, const2),
            pl.BlockSpec(g1.shape, const2),
            pl.BlockSpec(b1.shape, const2),
            pl.BlockSpec(wrt.shape, const2),
            pl.BlockSpec(br.shape, const2),
        ],
        out_specs=[
            pl.BlockSpec((tm, d), row),
            pl.BlockSpec((TOP_K, tm), col),
            pl.BlockSpec((TOP_K, tm), col),
            pl.BlockSpec((TOP_K, tm), col),
            pl.BlockSpec((N_EXPERTS, LANES), const2),
            pl.BlockSpec((MIX_SUB * N_EXPERTS, LANES), row),
        ],
        scratch_shapes=[pltpu.VMEM((tm, DIFF_WIDTH + GMLP_WIDTH), BF16),
                        pltpu.VMEM((N_EXPERTS, LANES), F32)],
        compiler_params=pltpu.CompilerParams(
            dimension_semantics=("arbitrary",), vmem_limit_bytes=VMEM_LIMIT),
        name="mix",
    )(attn, u, vn, x2, wsp, bsp, wo, g1, b1, wrt, br)


def _for_each_run(tile, tcnt_ref, off_ref, dst_ref, make_copy):
    def per_expert(e, carry):
        n = tcnt_ref[tile * N_EXPERTS + e]
        off = off_ref[tile * N_EXPERTS + e]
        dst = dst_ref[tile * N_EXPERTS + e]
        for bit in range(RUN_BITS):
            done = (n >> (bit + 1)) << (bit + 1)

            @pl.when(((n >> bit) & 1) == 1)
            def _():
                make_copy(off + done, dst + done, 1 << bit).start()
        return carry

    lax.fori_loop(0, N_EXPERTS, per_expert, 0, unroll=4)


def _dispatch_kernel(tcnt_ref, off_ref, dst_ref, pad_start_ref, pad_len_ref, nused_ref,
                     pos_ref, x1_ref, xs_hbm, sorted_ref, zero_ref, sem, zsem):
    tm = MOE_TM
    rows = TOP_K * tm
    d = x1_ref.shape[1]
    sub = d // LANES
    nblk = xs_hbm.shape[0] // (FFN_TM * sub)
    i = pl.program_id(0)
    last = pl.num_programs(0) - 1
    slot = lax.rem(i, 2)

    def sorted_row(s, h, local_row):
        return pl.multiple_of(((s * DISPATCH_SUB + h) * rows + local_row) * sub, sub)

    def wait_runs(s):
        pltpu.make_async_copy(sorted_ref.at[pl.ds(sorted_row(s, 0, 0), DISPATCH_SUB * rows * sub)],
                              xs_hbm.at[pl.ds(0, DISPATCH_SUB * rows * sub)], sem.at[s]).wait()

    @pl.when(i >= 2)
    def _():
        wait_runs(slot)

    ri = lax.broadcasted_iota(jnp.int32, (rows, tm), 0)
    perms = []
    for h in range(DISPATCH_SUB):
        pos = pos_ref[:, h * tm:(h + 1) * tm]
        hit = (ri == pos[0:1]) | (ri == pos[1:2]) | (ri == pos[2:3]) | (ri == pos[3:4])
        perms.append(jnp.where(hit, 1.0, 0.0).astype(BF16))
    srts = [jnp.dot(perms[h], x1_ref[h * tm:(h + 1) * tm, :].astype(BF16),
                    preferred_element_type=F32) for h in range(DISPATCH_SUB)]
    for h in range(DISPATCH_SUB):
        _rows_to_slabs(sorted_ref, sorted_row(slot, h, 0), srts[h], sub)

    for h in range(DISPATCH_SUB):
        def run_copy(local_row, global_row, nrows, h=h):
            return pltpu.make_async_copy(
                sorted_ref.at[pl.ds(sorted_row(slot, h, local_row), nrows * sub)],
                xs_hbm.at[pl.ds(pl.multiple_of(global_row * sub, sub), nrows * sub)], sem.at[slot])

        _for_each_run(i * DISPATCH_SUB + h, tcnt_ref, off_ref, dst_ref, run_copy)

    @pl.when(i == 0)
    def _():
        zero_ref[...] = jnp.zeros_like(zero_ref)

        def zero_copy(r, nrows):
            return pltpu.make_async_copy(
                zero_ref.at[pl.ds(0, nrows * sub)],
                xs_hbm.at[pl.ds(pl.multiple_of(r * sub, sub), nrows * sub)], zsem)

        def tail_copy(blk):
            return pltpu.make_async_copy(
                zero_ref, xs_hbm.at[pl.ds(pl.multiple_of(blk * (FFN_TM * sub), FFN_TM * sub),
                                          FFN_TM * sub)], zsem)

        def tail_start(blk, c):
            tail_copy(blk).start()
            return c

        def tail_wait(blk, c):
            tail_copy(blk).wait()
            return c

        lax.fori_loop(nused_ref[0], nblk, tail_start, 0)
        lax.fori_loop(nused_ref[0], nblk, tail_wait, 0)

        def per_expert(e, _):
            s = pad_start_ref[e]
            cnt = pad_len_ref[e]
            for wait in (False, True):
                for bit in range(FFN_TM.bit_length() - 1):
                    done = (cnt >> (bit + 1)) << (bit + 1)

                    @pl.when(((cnt >> bit) & 1) == 1)
                    def _():
                        piece = zero_copy(s + done, 1 << bit)
                        piece.wait() if wait else piece.start()
            return 0

        lax.fori_loop(0, N_EXPERTS, per_expert, 0)

    @pl.when(i == last)
    def _():
        wait_runs(slot)

        @pl.when(last >= 1)
        def _():
            wait_runs(1 - slot)


def _dispatch(x1, pos, tcnt, off, dst, pad_start, pad_len, nused, p_rows):
    n, d = x1.shape
    tm = MOE_TM * DISPATCH_SUB
    sub = d // LANES
    return pl.pallas_call(
        _dispatch_kernel,
        out_shape=jax.ShapeDtypeStruct((p_rows * sub, LANES), F32),
        grid_spec=pltpu.PrefetchScalarGridSpec(
            num_scalar_prefetch=6,
            grid=(n // tm,),
            in_specs=[
                pl.BlockSpec((TOP_K, tm), lambda i, *_: (0, i)),
                pl.BlockSpec((tm, d), lambda i, *_: (i, 0)),
            ],
            out_specs=pl.BlockSpec(memory_space=pl.ANY),
            scratch_shapes=[pltpu.VMEM((2 * TOP_K * tm * sub, LANES), F32),
                            pltpu.VMEM((FFN_TM * sub, LANES), F32),
                            pltpu.SemaphoreType.DMA((2,)), pltpu.SemaphoreType.DMA(())],
        ),
        compiler_params=pltpu.CompilerParams(
            dimension_semantics=("arbitrary",), vmem_limit_bytes=VMEM_LIMIT),
        name="dispatch",
    )(tcnt, off, dst, pad_start, pad_len, nused, pos, x1)


def _ffn_kernel(blk_e_ref, nused_ref, next_e_ref, valid_ref, xs_ref, wup_hbm, bup_ref, wdn_hbm,
                bdn_ref, y_ref, wup_f, wdn_f, wup_b, wdn_b, wsem):
    tm = FFN_TM
    dff = wdn_f.shape[0]
    sub = wdn_f.shape[1] // LANES
    i = pl.program_id(0)
    used = i < nused_ref[0]

    @pl.when(jnp.logical_not(used))
    def _():
        y_ref[...] = jnp.zeros_like(y_ref)

    def weight_copies(e):
        return (pltpu.make_async_copy(wup_hbm.at[e], wup_f, wsem.at[0]),
                pltpu.make_async_copy(wdn_hbm.at[e], wdn_f, wsem.at[1]))

    e = blk_e_ref[i]
    new_expert = jnp.logical_or(i == 0, e != blk_e_ref[jnp.maximum(i - 1, 0)])

    @pl.when(jnp.logical_and(used, new_expert))
    def _():
        @pl.when(i == 0)
        def _():
            for cp in weight_copies(e):
                cp.start()

        for cp in weight_copies(e):
            cp.wait()
        for c in range(0, 2 * dff, WEIGHT_CAST_COLS):
            wup_b[:, c:c + WEIGHT_CAST_COLS] = wup_f[:, c:c + WEIGHT_CAST_COLS].astype(BF16)
        for c in range(0, wdn_f.shape[1], WEIGHT_CAST_COLS):
            wdn_b[:, c:c + WEIGHT_CAST_COLS] = wdn_f[:, c:c + WEIGHT_CAST_COLS].astype(BF16)

        @pl.when(next_e_ref[e] != e)
        def _():
            for cp in weight_copies(next_e_ref[e]):
                cp.start()

    def expert_mlp(nrows):
        xb = _slabs_to_rows(xs_ref, 0, nrows, sub).astype(BF16)
        h = jnp.dot(xb, wup_b[...], preferred_element_type=F32) + bup_ref[0]
        gate = jnp.minimum(h[:, :dff], SWIGLU_LIMIT)
        lin = jnp.clip(h[:, dff:], -SWIGLU_LIMIT, SWIGLU_LIMIT)
        act = (lin + 1.0) * gate * jax.nn.sigmoid(SWIGLU_ALPHA * gate)
        y = jnp.dot(act.astype(BF16), wdn_b[...], preferred_element_type=F32) + bdn_ref[0]
        _rows_to_slabs(y_ref, 0, y, sub)

    half = tm // 2
    more_than_half = valid_ref[i] > half

    @pl.when(jnp.logical_and(used, more_than_half))
    def _():
        expert_mlp(tm)

    @pl.when(jnp.logical_and(used, jnp.logical_not(more_than_half)))
    def _():
        expert_mlp(half)
        y_ref[half * sub:, :] = jnp.zeros((half * sub, LANES), F32)


def _ffn(xs, blk_e, nused, next_e, blk_valid, wup, bup, wdn, bdn):
    tm = FFN_TM
    e, d, dff2 = wup.shape
    dff = dff2 // 2
    sub = d // LANES
    rowblk = lambda i, be, nu, ne, bv: (i, 0)
    bsel = lambda i, be, nu, ne, bv: (be[i], 0, 0)
    return pl.pallas_call(
        _ffn_kernel,
        out_shape=jax.ShapeDtypeStruct(xs.shape, F32),
        grid_spec=pltpu.PrefetchScalarGridSpec(
            num_scalar_prefetch=4,
            grid=(xs.shape[0] // (tm * sub),),
            in_specs=[
                pl.BlockSpec((tm * sub, LANES), rowblk),
                pl.BlockSpec(memory_space=pl.ANY),
                pl.BlockSpec((1, 1, dff2), bsel),
                pl.BlockSpec(memory_space=pl.ANY),
                pl.BlockSpec((1, 1, d), bsel),
            ],
            out_specs=pl.BlockSpec((tm * sub, LANES), rowblk),
            scratch_shapes=[pltpu.VMEM((d, dff2), F32), pltpu.VMEM((dff, d), F32),
                            pltpu.VMEM((d, dff2), BF16), pltpu.VMEM((dff, d), BF16),
                            pltpu.SemaphoreType.DMA((2,))],
        ),
        compiler_params=pltpu.CompilerParams(
            dimension_semantics=("arbitrary",), vmem_limit_bytes=VMEM_LIMIT),
        name="ffn",
    )(blk_e, nused, next_e, blk_valid, xs, wup, bup, wdn, bdn)


def _combine_kernel(tcnt_ref, off_ref, dst_ref, pos_ref, gate_ref, x1_ref, g2_ref, b2_ref, y_hbm,
                    o_ref, stage_ref, sem):
    tm = MOE_TM
    rows = TOP_K * tm
    sub = x1_ref.shape[1] // LANES
    i = pl.program_id(0)
    nsteps = pl.num_programs(0)
    slot = lax.rem(i, 2)

    def stage_row(s, h, local_row):
        return pl.multiple_of(((s * COMBINE_SUB + h) * rows + local_row) * sub, sub)

    def fetch_runs(step, s):
        for h in range(COMBINE_SUB):
            def run_copy(local_row, global_row, nrows, h=h):
                return pltpu.make_async_copy(
                    y_hbm.at[pl.ds(pl.multiple_of(global_row * sub, sub), nrows * sub)],
                    stage_ref.at[pl.ds(stage_row(s, h, local_row), nrows * sub)], sem.at[s])
            _for_each_run(step * COMBINE_SUB + h, tcnt_ref, off_ref, dst_ref, run_copy)

    @pl.when(i == 0)
    def _():
        fetch_runs(0, 0)

    @pl.when(i + 1 < nsteps)
    def _():
        fetch_runs(i + 1, 1 - slot)

    pltpu.make_async_copy(y_hbm.at[pl.ds(0, COMBINE_SUB * rows * sub)],
                          stage_ref.at[pl.ds(stage_row(slot, 0, 0), COMBINE_SUB * rows * sub)],
                          sem.at[slot]).wait()

    tiles = range(COMBINE_SUB)
    ci = lax.broadcasted_iota(jnp.int32, (tm, rows), 1)
    ws = []
    for h in tiles:
        pos = pos_ref[h * tm:(h + 1) * tm, :]
        gates = gate_ref[h * tm:(h + 1) * tm, :]
        w = jnp.where(ci == pos[:, 0:1], gates[:, 0:1], 0.0)
        for k in range(1, TOP_K):
            w = w + jnp.where(ci == pos[:, k:k + 1], gates[:, k:k + 1], 0.0)
        ws.append(w)
    wh = [w.astype(BF16) for w in ws]
    wl = [(w - hi.astype(F32)).astype(BF16) for w, hi in zip(ws, wh)]
    ys = [_slabs_to_rows(stage_ref, stage_row(slot, h, 0), rows, sub) for h in tiles]
    yh = [y.astype(BF16) for y in ys]
    yl = [(y - hi.astype(F32)).astype(BF16) for y, hi in zip(ys, yh)]
    for h in tiles:
        ffn = (jnp.dot(wh[h], yh[h], preferred_element_type=F32)
               + jnp.dot(wl[h], yh[h], preferred_element_type=F32)
               + jnp.dot(wh[h], yl[h], preferred_element_type=F32))
        rs = slice(h * tm, (h + 1) * tm)
        o_ref[rs, :] = _layer_norm(DEEPNORM_ALPHA * x1_ref[rs, :] + ffn, g2_ref[...], b2_ref[...])


def _combine(y, pos_tok, gates_tok, tcnt, off, dst, x1, g2, b2):
    n, d = x1.shape
    tm = MOE_TM * COMBINE_SUB
    sub = d // LANES
    row = lambda i, *_: (i, 0)
    const = lambda i, *_: (0, 0)
    return pl.pallas_call(
        _combine_kernel,
        out_shape=jax.ShapeDtypeStruct((n, d), F32),
        grid_spec=pltpu.PrefetchScalarGridSpec(
            num_scalar_prefetch=3,
            grid=(n // tm,),
            in_specs=[
                pl.BlockSpec((tm, TOP_K), row),
                pl.BlockSpec((tm, TOP_K), row),
                pl.BlockSpec((tm, d), row),
                pl.BlockSpec(g2.shape, const),
                pl.BlockSpec(b2.shape, const),
                pl.BlockSpec(memory_space=pl.ANY),
            ],
            out_specs=pl.BlockSpec((tm, d), row),
            scratch_shapes=[pltpu.VMEM((2 * TOP_K * tm * sub, LANES), F32),
                            pltpu.SemaphoreType.DMA((2,))],
        ),
        compiler_params=pltpu.CompilerParams(
            dimension_semantics=("arbitrary",), vmem_limit_bytes=VMEM_LIMIT),
        name="combine",
    )(tcnt, off, dst, pos_tok, gates_tok, x1, g2, b2, y)


def _head_pairs(w_a, w_b):
    d = w_a.shape[0]
    a = w_a.reshape(d, DIFF_HEADS, DIFF_HEAD_DIM)
    b = w_b.reshape(d, DIFF_HEADS, DIFF_HEAD_DIM)
    return jnp.concatenate([a, b], axis=-1).reshape(d, DIFF_HEADS * V_HEAD_DIM)


def _lookup(table, eid):
    eids = jnp.arange(N_EXPERTS, dtype=jnp.int32)[:, None, None]
    return jnp.sum(jnp.where(eid[None] == eids, table[:, None, :], 0), axis=0)


def kernel(x, w_in, lambda_q1, lambda_k1, lambda_q2, lambda_k2, subln_g, gmlp_ln_g, gmlp_ln_b,
           w_spatial, b_spatial, w_o, ln1_g, ln1_b, w_router, b_router, w_up, b_up,
           w_down, b_down, ln2_g, ln2_b):
    b, s, d = x.shape
    n = b * s
    x2 = x.reshape(n, d)

    w = w_in[0]
    c = QK_WIDTH
    scale = DIFF_HEAD_DIM ** -0.5 * LOG2_E
    wq = _head_pairs(w[:, 0:c], w[:, c:2 * c]) * scale
    wk = _head_pairs(w[:, 2 * c:3 * c], w[:, 3 * c:4 * c])
    wqk = jnp.concatenate([wq, wk], axis=1).astype(BF16)
    wvt = w[:, 4 * c:4 * c + DIFF_WIDTH].T.astype(BF16)
    wg = w[:, 4 * c + DIFF_WIDTH:].astype(BF16)
    lamv = jnp.concatenate([lambda_q1, lambda_k1, lambda_q2, lambda_k2], axis=0)
    bsp = jnp.repeat(b_spatial[0].T, GMLP_GROUP_DIM, axis=1)

    assert w_in.shape[0] == DEPTH and d % (SUBLANES * LANES) == 0
    assert PROJ_TM == ATTN_T and s % (ATTN_T * ATTN_GROUP) == 0
    assert n % (MOE_TM * max(MIX_SUB, COMBINE_SUB, DISPATCH_SUB)) == 0
    qq, kk, vt, u, vn = _proj(x2, wqk, wvt, wg, gmlp_ln_g, gmlp_ln_b)
    attn = _attention(qq.reshape(b, s, -1), kk.reshape(b, s, -1),
                      vt.reshape(b, s // ATTN_T, DIFF_WIDTH, ATTN_T),
                      lamv, subln_g.reshape(V_HEAD_DIM, 1)).reshape(n, DIFF_WIDTH)
    x1, eid, gates, rank, cnt, tcnt_l = _mix(attn, u, vn, x2, w_spatial[0], bsp,
                                             w_o[0].astype(BF16), ln1_g, ln1_b,
                                             w_router[0].T, b_router[0][:, None])

    ntiles = n // MOE_TM
    counts = cnt[:, 0].astype(jnp.int32)
    padded = ((counts + FFN_TM - 1) // FFN_TM) * FFN_TM
    end_padded = jnp.cumsum(padded)
    start_padded = end_padded - padded
    tcnt = tcnt_l.reshape(ntiles, N_EXPERTS, LANES)[:, :, 0].astype(jnp.int32)
    before = jnp.cumsum(tcnt, axis=0) - tcnt
    off = jnp.cumsum(tcnt, axis=1) - tcnt
    dst = start_padded[None, :] + before
    tile_base = jnp.repeat((off - before).T, MOE_TM, axis=1)
    pos = _lookup(tile_base, eid) + rank

    p_rows = n * TOP_K + N_EXPERTS * FFN_TM
    nblk = p_rows // FFN_TM
    nused = (end_padded[-1:] // FFN_TM).astype(jnp.int32)
    blk_start = jnp.arange(nblk, dtype=jnp.int32) * FFN_TM
    blk_e = jnp.minimum(jnp.sum((end_padded[None, :] <= blk_start[:, None]).astype(jnp.int32), axis=1),
                        N_EXPERTS - 1)
    tcnt_f, off_f, dst_f = tcnt.reshape(-1), off.reshape(-1), dst.reshape(-1)

    xs = _dispatch(x1, pos, tcnt_f, off_f, dst_f, start_padded + counts, padded - counts, nused,
                   p_rows)
    e_ids = jnp.arange(N_EXPERTS, dtype=jnp.int32)
    later_used = jnp.logical_and(e_ids[None, :] > e_ids[:, None], padded[None, :] > 0)
    next_e = jnp.min(jnp.where(later_used, e_ids[None, :], N_EXPERTS), axis=1)
    next_e = jnp.where(next_e == N_EXPERTS, e_ids, next_e)
    is_e = blk_e[:, None] == e_ids[None, :]
    filled = jnp.sum(jnp.where(is_e, (start_padded + counts)[None, :], 0), axis=1) - blk_start
    blk_valid = jnp.clip(filled, 0, FFN_TM).astype(jnp.int32)
    y = _ffn(xs, blk_e, nused, next_e, blk_valid, w_up[0], b_up[0][:, None, :], w_down[0],
             b_down[0][:, None, :])
    out = _combine(y, pos.T, gates.T, tcnt_f, off_f, dst_f, x1, ln2_g, ln2_b)
    return out.reshape(b, s, d)
```
